```python
import math
import jax, jax.numpy as jnp
from jax import lax
import numpy as np

D_MODEL = 1024
BATCH = 2
SEQ = 8192
DEPTH = 4
DEC_BATCH = 128
DEC_SEQ = 4
PAST_LEN = 2048
PAGE_SIZE = 128

N_A_LAYERS = (DEPTH + 1) // 2
N_C_LAYERS = DEPTH // 2
NSA_HEADS = 8
NSA_KV_HEADS = 2
HEAD_DIM = 64
HPG = NSA_HEADS // NSA_KV_HEADS
NSA_WIDTH = NSA_HEADS * HEAD_DIM
KV_WIDTH = NSA_KV_HEADS * HEAD_DIM
CMP_BLOCK = 64
N_SEL = 16
WINDOW = 512
Q_BLOCK = 128
GMLP_GROUPS = 4
GMLP_CHUNK = 128
GMLP_WIDTH = D_MODEL // 2
GMLP_GROUP_DIM = GMLP_WIDTH // GMLP_GROUPS
LRU_WIDTH = D_MODEL
LRU_HEADS = 8
LRU_BLOCK = LRU_WIDTH // LRU_HEADS
CONV_WIDTH = 4
LRU_C = 8.0
REL_BUCKETS = 32
REL_MAX_DIST = 1024
EPS = 1e-6
NEG = -1e30
SPLIT_A = (NSA_WIDTH, 6 * KV_WIDTH, 3 * NSA_HEADS, NSA_WIDTH, GMLP_WIDTH, GMLP_WIDTH, GMLP_WIDTH)
PROJ_A = sum(SPLIT_A)
OUT_A = NSA_WIDTH + GMLP_WIDTH

kernel_name = 'hybrid_nsa_gmlp_rglru_step'


def rms_norm(x, g):
    xf = x.astype(jnp.float32)
    y = xf * lax.rsqrt(jnp.mean(xf * xf, axis=-1, keepdims=True) + EPS)
    return (y * g.astype(jnp.float32)).astype(x.dtype)


def rel_bucket(d):
    max_exact = REL_BUCKETS // 2
    df = jnp.maximum(d, 1).astype(jnp.float32)
    large = max_exact + (jnp.log(df / max_exact) / math.log(REL_MAX_DIST / max_exact)
                         * (REL_BUCKETS - max_exact)).astype(jnp.int32)
    return jnp.where(d < max_exact, d, jnp.minimum(large, REL_BUCKETS - 1))


def nsa_block(qb, qpos, kc, vc, kc_end, ks_t, vs_t, kw, vw, wpos, rel_bias):
    B, Tb = qb.shape[0], qb.shape[1]
    f32 = jnp.float32
    scale = HEAD_DIM ** -0.5
    qg = qb.reshape(B, Tb, NSA_KV_HEADS, HPG, HEAD_DIM)
    tab = rel_bias.astype(f32).reshape(REL_BUCKETS, NSA_KV_HEADS, HPG)
    s_c = jnp.einsum('btghd,bngd->bghtn', qg, kc).astype(f32) * scale
    d_c = qpos[:, None] - kc_end[None, :]
    ok_c = d_c >= 0
    s_c = s_c + jnp.transpose(tab[rel_bucket(jnp.maximum(d_c, 0))], (2, 3, 0, 1))
    p_c = jax.nn.softmax(jnp.where(ok_c, s_c, NEG), axis=-1) * ok_c
    o_c = jnp.einsum('bghtn,bngd->btghd', p_c.astype(vc.dtype), vc)
    n_blk = kc.shape[1]
    blk = jnp.arange(n_blk, dtype=jnp.int32)[None, :]
    cur = (qpos // CMP_BLOCK)[:, None]
    forced = (blk == 0) | (blk == cur) | (blk == cur - 1)
    score = jnp.where(forced, jnp.inf, jnp.where(blk < cur, p_c.sum(axis=2), -jnp.inf))
    top_s, top_i = lax.top_k(score, min(N_SEL, n_blk))
    bi = jnp.arange(B)[:, None, None, None]
    gi = jnp.arange(NSA_KV_HEADS)[None, :, None, None]
    k_s = ks_t[bi, gi, top_i]
    v_s = vs_t[bi, gi, top_i]
    kpos = top_i[..., None] * CMP_BLOCK + jnp.arange(CMP_BLOCK, dtype=jnp.int32)
    d_s = qpos[None, None, :, None, None] - kpos
    ok_s = (top_s > -jnp.inf)[..., None] & (d_s >= 0)
    tab_g = jnp.transpose(tab, (1, 0, 2))
    b_s = jnp.moveaxis(tab_g[gi[..., None], rel_bucket(jnp.maximum(d_s, 0))], -1, 2)
    s_s = jnp.einsum('btghd,bgtkcd->bghtkc', qg, k_s).astype(f32) * scale + b_s
    s_s = jnp.where(ok_s[:, :, None], s_s, NEG)
    n_key = s_s.shape[-2] * CMP_BLOCK
    p_s = jax.nn.softmax(s_s.reshape(B, NSA_KV_HEADS, HPG, Tb, n_key), axis=-1)
    o_s = jnp.einsum('bghtk,bgtkd->btghd', p_s.astype(v_s.dtype),
                     v_s.reshape(B, NSA_KV_HEADS, Tb, n_key, HEAD_DIM))
    s_w = jnp.einsum('btghd,bkgd->bghtk', qg, kw).astype(f32) * scale
    d_w = qpos[:, None] - wpos[None, :]
    ok_w = (d_w >= 0) & (d_w <= WINDOW) & (wpos >= 0)[None, :]
    s_w = s_w + jnp.transpose(tab[rel_bucket(jnp.maximum(d_w, 0))], (2, 3, 0, 1))
    p_w = jax.nn.softmax(jnp.where(ok_w, s_w, NEG), axis=-1)
    o_w = jnp.einsum('bghtk,bkgd->btghd', p_w.astype(vw.dtype), vw)
    shp = (B, Tb, NSA_HEADS, HEAD_DIM)
    return (o_c.reshape(shp), o_s.reshape(shp), o_w.reshape(shp))


def mixer_ab(h, pos0, past_kv, win_kv, w_in, b_gate, pe_cmp, w_cmp, ln_g, ln_b, w_sp, b_sp, w_out, rel_bias):
    B, T, _ = h.shape
    f32 = jnp.float32
    idx = [int(i) for i in np.cumsum(SPLIT_A)[:-1]]
    proj = jnp.einsum('btd,de->bte', h, w_in)
    q, kv6, g_logit, z_nsa, u, v, z_g = jnp.split(proj, idx, axis=-1)
    q = q.reshape(B, T, NSA_HEADS, HEAD_DIM)
    kv6 = kv6.reshape(B, T, 6, NSA_KV_HEADS, HEAD_DIM)
    new_rows = kv6[:, :, :4]
    win_rows = kv6[:, :, 4:]
    full = new_rows if past_kv is None else jnp.concatenate([past_kv.astype(new_rows.dtype), new_rows], axis=1)
    L = full.shape[1]
    n_blk = -(-L // CMP_BLOCK)
    fb = jnp.pad(full, ((0, 0), (0, n_blk * CMP_BLOCK - L), (0, 0), (0, 0), (0, 0)))
    fb = fb.reshape(B, n_blk, CMP_BLOCK, 4, NSA_KV_HEADS, HEAD_DIM)
    kc = jnp.einsum('bngd,de->bnge', jnp.mean(fb[:, :, :, 0] + pe_cmp[0][None, None, :, None, :], axis=2), w_cmp[0])
    vc = jnp.einsum('bngd,de->bnge', jnp.mean(fb[:, :, :, 1] + pe_cmp[1][None, None, :, None, :], axis=2), w_cmp[1])
    kc_end = jnp.arange(n_blk, dtype=jnp.int32) * CMP_BLOCK + (CMP_BLOCK - 1)
    ks_t = jnp.transpose(fb[:, :, :, 2], (0, 3, 1, 2, 4))
    vs_t = jnp.transpose(fb[:, :, :, 3], (0, 3, 1, 2, 4))
    if win_kv is None:
        nb = T // Q_BLOCK
        wpad = jnp.pad(win_rows, ((0, 0), (WINDOW, 0), (0, 0), (0, 0), (0, 0)))

        def body(i):
            start = i * Q_BLOCK
            qb = lax.dynamic_slice_in_dim(q, start, Q_BLOCK, axis=1)
            wb = lax.dynamic_slice_in_dim(wpad, start, WINDOW + Q_BLOCK, axis=1)
            qpos = pos0 + start + jnp.arange(Q_BLOCK, dtype=jnp.int32)
            wpos = pos0 + start - WINDOW + jnp.arange(WINDOW + Q_BLOCK, dtype=jnp.int32)
            return nsa_block(qb, qpos, kc, vc, kc_end, ks_t, vs_t, wb[:, :, 0], wb[:, :, 1], wpos, rel_bias)

        oc, osl, ow = lax.map(body, jnp.arange(nb, dtype=jnp.int32))
        oc = jnp.swapaxes(oc, 0, 1).reshape(B, T, NSA_HEADS, HEAD_DIM)
        osl = jnp.swapaxes(osl, 0, 1).reshape(B, T, NSA_HEADS, HEAD_DIM)
        ow = jnp.swapaxes(ow, 0, 1).reshape(B, T, NSA_HEADS, HEAD_DIM)
        win_state = win_rows[:, T - min(WINDOW, T):]
    else:
        wb_len = win_kv.shape[1]
        wk = jnp.concatenate([win_kv.astype(win_rows.dtype), win_rows], axis=1)
        wpos = pos0 - wb_len + jnp.arange(wb_len + T, dtype=jnp.int32)
        qpos = pos0 + jnp.arange(T, dtype=jnp.int32)
        oc, osl, ow = nsa_block(q, qpos, kc, vc, kc_end, ks_t, vs_t, wk[:, :, 0], wk[:, :, 1], wpos, rel_bias)
        win_state = win_rows
    g = jax.nn.sigmoid((g_logit + b_gate).astype(f32)).reshape(B, T, 3, NSA_HEADS, 1)
    o = (g[:, :, 0] * oc + g[:, :, 1] * osl + g[:, :, 2] * ow).astype(h.dtype).reshape(B, T, NSA_WIDTH)
    y_nsa = o * jax.nn.silu(z_nsa)
    vf = v.astype(f32)
    mu = jnp.mean(vf, axis=-1, keepdims=True)
    var = jnp.mean(jnp.square(vf - mu), axis=-1, keepdims=True)
    vn = ((vf - mu) * lax.rsqrt(var + EPS) * ln_g.astype(f32) + ln_b.astype(f32)).astype(v.dtype)
    tc = min(T, GMLP_CHUNK)
    nc = T // tc
    wm = jnp.tril(w_sp[:, :tc, :tc])
    vb = vn.reshape(B, nc, tc, GMLP_GROUPS, GMLP_GROUP_DIM)
    mix = jnp.einsum('gij,bnjgd->bnigd', wm, vb) + jnp.transpose(b_sp[:, :tc])[None, None, :, :, None]
    y_g = (u.reshape(B, nc, tc, GMLP_GROUPS, GMLP_GROUP_DIM) * mix).reshape(B, T, GMLP_WIDTH) * jax.nn.silu(z_g)
    out = jnp.einsum('bte,ed->btd', jnp.concatenate([y_nsa, y_g], axis=-1), w_out)
    return out, new_rows, win_state, vn


def mixer_c(h, h0, conv0, w_in, conv_w, conv_b, w_ra, b_ra, w_rx, b_rx, lam, w_out):
    B, T, _ = h.shape
    f32 = jnp.float32
    xb, z = jnp.split(jnp.einsum('btd,de->bte', h, w_in), 2, axis=-1)
    xin = jnp.concatenate([conv0.astype(xb.dtype), xb], axis=1)
    xc = conv_b + xin[:, 0:T] * conv_w[0]
    for k in range(1, CONV_WIDTH):
        xc = xc + xin[:, k:k + T] * conv_w[k]
    xr = xc.reshape(B, T, LRU_HEADS, LRU_BLOCK)
    r = jax.nn.sigmoid((jnp.einsum('bthi,hij->bthj', xr, w_ra).reshape(B, T, LRU_WIDTH) + b_ra).astype(f32))
    ig = jax.nn.sigmoid((jnp.einsum('bthi,hij->bthj', xr, w_rx).reshape(B, T, LRU_WIDTH) + b_rx).astype(f32))
    log_a = -LRU_C * r * jax.nn.softplus(-lam.astype(f32))
    a = jnp.exp(log_a)
    bt = jnp.sqrt(-jnp.expm1(2.0 * log_a)) * ig * xc.astype(f32)

    def step(hc, ab):
        hn = ab[0] * hc + ab[1]
        return hn, hn

    h_last, hs = lax.scan(step, h0.astype(f32), (jnp.swapaxes(a, 0, 1), jnp.swapaxes(bt, 0, 1)))
    y = jnp.swapaxes(hs, 0, 1).astype(h.dtype) * jax.nn.silu(z)
    return jnp.einsum('btw,wd->btd', y, w_out), h_last, xin[:, -(CONV_WIDTH - 1):]


def setup_inputs(seed: int = 0) -> dict:
    key = jax.random.key(seed)
    ks = jax.random.split(key, 32)
    n_pages = PAST_LEN // PAGE_SIZE
    n_pool = (5 * DEC_BATCH * n_pages + 3) // 4

    def nrm(k, shape, scale):
        return jax.random.normal(k, shape, jnp.float32) * scale

    u = jax.random.uniform(ks[25], (N_C_LAYERS, LRU_WIDTH), jnp.float32, minval=0.9, maxval=0.999)
    return {
        'x_prompt': nrm(ks[0], (BATCH, SEQ, D_MODEL), 1.0),
        'x_sample': nrm(ks[1], (DEC_BATCH, DEC_SEQ, D_MODEL), 1.0),
        'cache_nsa': nrm(ks[2], (N_A_LAYERS, n_pool, PAGE_SIZE, 4, NSA_KV_HEADS, HEAD_DIM), 1.0),
        'cache_win': nrm(ks[3], (N_A_LAYERS, DEC_BATCH, min(WINDOW, PAST_LEN), 2, NSA_KV_HEADS, HEAD_DIM), 1.0),
        'state_lru_h': nrm(ks[4], (N_C_LAYERS, DEC_BATCH, LRU_WIDTH), 0.5),
        'state_lru_conv': nrm(ks[5], (N_C_LAYERS, DEC_BATCH, CONV_WIDTH - 1, LRU_WIDTH), 1.0),
        'page_table': jax.random.permutation(ks[6], n_pool)[:DEC_BATCH * n_pages].reshape(DEC_BATCH, n_pages).astype(jnp.int32),
        'norm_a': 1.0 + nrm(ks[7], (N_A_LAYERS, D_MODEL), 0.05),
        'w_in_a': nrm(ks[8], (N_A_LAYERS, D_MODEL, PROJ_A), D_MODEL ** -0.5),
        'b_gate_a': nrm(ks[9], (N_A_LAYERS, 3 * NSA_HEADS), 0.1),
        'pe_cmp': nrm(ks[10], (N_A_LAYERS, 2, CMP_BLOCK, HEAD_DIM), 0.1),
        'w_cmp': nrm(ks[11], (N_A_LAYERS, 2, HEAD_DIM, HEAD_DIM), math.sqrt(CMP_BLOCK / HEAD_DIM)),
        'gmlp_ln_g': 1.0 + nrm(ks[12], (N_A_LAYERS, GMLP_WIDTH), 0.05),
        'gmlp_ln_b': nrm(ks[13], (N_A_LAYERS, GMLP_WIDTH), 0.02),
        'w_spatial': nrm(ks[14], (N_A_LAYERS, GMLP_GROUPS, GMLP_CHUNK, GMLP_CHUNK), GMLP_CHUNK ** -0.5),
        'b_spatial': 1.0 + nrm(ks[15], (N_A_LAYERS, GMLP_GROUPS, GMLP_CHUNK), 0.1),
        'w_out_a': nrm(ks[16], (N_A_LAYERS, OUT_A, D_MODEL), OUT_A ** -0.5),
        'norm_c': 1.0 + nrm(ks[17], (N_C_LAYERS, D_MODEL), 0.05),
        'w_in_c': nrm(ks[18], (N_C_LAYERS, D_MODEL, 2 * LRU_WIDTH), D_MODEL ** -0.5),
        'conv_w': nrm(ks[19], (N_C_LAYERS, CONV_WIDTH, LRU_WIDTH), CONV_WIDTH ** -0.5),
        'conv_b': nrm(ks[20], (N_C_LAYERS, LRU_WIDTH), 0.02),
        'w_rg_a': nrm(ks[21], (N_C_LAYERS, LRU_HEADS, LRU_BLOCK, LRU_BLOCK), LRU_BLOCK ** -0.5),
        'b_rg_a': nrm(ks[22], (N_C_LAYERS, LRU_WIDTH), 0.1),
        'w_rg_x': nrm(ks[23], (N_C_LAYERS, LRU_HEADS, LRU_BLOCK, LRU_BLOCK), LRU_BLOCK ** -0.5),
        'b_rg_x': nrm(ks[24], (N_C_LAYERS, LRU_WIDTH), 0.1),
        'lru_lambda': jnp.log(u) - jnp.log1p(-u),
        'w_out_c': nrm(ks[26], (N_C_LAYERS, LRU_WIDTH, D_MODEL), LRU_WIDTH ** -0.5),
        'rel_bias': nrm(ks[27], (REL_BUCKETS, NSA_HEADS), 0.2),
        'final_norm': 1.0 + nrm(ks[28], (D_MODEL,), 0.05),
    }


def reference(x_prompt, x_sample, cache_nsa, cache_win, state_lru_h, state_lru_conv, page_table,
              norm_a, w_in_a, b_gate_a, pe_cmp, w_cmp, gmlp_ln_g, gmlp_ln_b, w_spatial, b_spatial, w_out_a,
              norm_c, w_in_c, conv_w, conv_b, w_rg_a, b_rg_a, w_rg_x, b_rg_x, lru_lambda, w_out_c,
              rel_bias, final_norm):
    dec_b, n_pages = page_table.shape
    past_len = n_pages * PAGE_SIZE
    bp = x_prompt.shape[0]
    yp, ys = x_prompt, x_sample
    nsa_p, nsa_s, win_p, win_s, gv_s = [], [], [], [], []
    h_p, h_s, cv_p, cv_s = [], [], [], []
    for layer in range(DEPTH):
        if layer % 2 == 0:
            a = layer // 2
            op, rows_p, wp, _ = mixer_ab(rms_norm(yp, norm_a[a]), 0, None, None,
                                         w_in_a[a], b_gate_a[a], pe_cmp[a], w_cmp[a], gmlp_ln_g[a], gmlp_ln_b[a],
                                         w_spatial[a], b_spatial[a], w_out_a[a], rel_bias)
            past = cache_nsa[a][page_table].reshape(dec_b, past_len, 4, NSA_KV_HEADS, HEAD_DIM)
            os_, rows_s, ws, vn_s = mixer_ab(rms_norm(ys, norm_a[a]), past_len, past, cache_win[a],
                                             w_in_a[a], b_gate_a[a], pe_cmp[a], w_cmp[a], gmlp_ln_g[a], gmlp_ln_b[a],
                                             w_spatial[a], b_spatial[a], w_out_a[a], rel_bias)
            nsa_p.append(rows_p); nsa_s.append(rows_s); win_p.append(wp); win_s.append(ws); gv_s.append(vn_s)
        else:
            c = layer // 2
            op, hp_, cp_ = mixer_c(rms_norm(yp, norm_c[c]), jnp.zeros((bp, LRU_WIDTH), jnp.float32),
                                   jnp.zeros((bp, CONV_WIDTH - 1, LRU_WIDTH), yp.dtype),
                                   w_in_c[c], conv_w[c], conv_b[c], w_rg_a[c], b_rg_a[c], w_rg_x[c], b_rg_x[c],
                                   lru_lambda[c], w_out_c[c])
            os_, hs_, cs_ = mixer_c(rms_norm(ys, norm_c[c]), state_lru_h[c], state_lru_conv[c],
                                    w_in_c[c], conv_w[c], conv_b[c], w_rg_a[c], b_rg_a[c], w_rg_x[c], b_rg_x[c],
                                    lru_lambda[c], w_out_c[c])
            h_p.append(hp_); h_s.append(hs_); cv_p.append(cp_); cv_s.append(cs_)
        yp = yp + op
        ys = ys + os_
    y_prompt = rms_norm(yp, final_norm)
    y_sample = rms_norm(ys, final_norm)
    return (y_prompt, y_sample, jnp.stack(nsa_p), jnp.stack(nsa_s), jnp.stack(win_p), jnp.stack(win_s),
            jnp.stack(gv_s), jnp.stack(h_p), jnp.stack(h_s), jnp.stack(cv_p), jnp.stack(cv_s))
```

```python
import functools
import math

import numpy as np
import jax
import jax.numpy as jnp
from jax import lax
from jax.experimental import pallas as pl
from jax.experimental.pallas import tpu as pltpu

F32 = jnp.float32
BF16 = jnp.bfloat16

D_MODEL = 1024
N_HEADS = 8
N_KV = 2
HPG = N_HEADS // N_KV
HEAD_DIM = 64
NSA_WIDTH = N_HEADS * HEAD_DIM
KV_WIDTH = N_KV * HEAD_DIM
CMP_BLOCK = 64
N_SEL = 16
WINDOW = 512
GMLP_GROUPS = 4
GMLP_WIDTH = 512
GMLP_CHUNK = 128
LRU_WIDTH = 1024
LRU_HEADS = 8
LRU_BLOCK = 128
CONV_WIDTH = 4
LRU_C = 8.0
REL_BUCKETS = 32
REL_MAX_DIST = 1024
EPS = 1e-6
NEG = -1e30
PAGE = 128

LANE = 128
Q_BLOCK = 128
KV_TILE = 256
NEAR_SPAN = 1024
WIN_KEYS = WINDOW + Q_BLOCK
VMEM_LIMIT = 56 * 1024 * 1024

QPAD = N_HEADS * LANE
C_KV = QPAD
C_Z = C_KV + 6 * KV_WIDTH
C_U = C_Z + NSA_WIDTH
C_V = C_U + GMLP_WIDTH
C_ZG = C_V + GMLP_WIDTH
C_G = C_ZG + GMLP_WIDTH
PROJ_COLS = C_G + LANE


def _nt(a, b):
    return lax.dot_general(a, b, (((1,), (1,)), ((), ())), preferred_element_type=F32)


def _mm(a, b):
    return jnp.dot(a, b, preferred_element_type=F32)


def _params(sem):
    return pltpu.CompilerParams(dimension_semantics=sem, vmem_limit_bytes=VMEM_LIMIT)


def _const_spec(shape):
    n = len(shape)
    return pl.BlockSpec(shape, lambda *_: (0,) * n)


def _bucket_upper_bounds():
    max_exact = REL_BUCKETS // 2
    d = np.arange(0, 4 * REL_MAX_DIST, dtype=np.int64)

    def buckets(ft):
        df = np.maximum(d, 1).astype(ft)
        large = max_exact + (np.log(df / ft(max_exact)) / ft(math.log(REL_MAX_DIST / max_exact))
                             * ft(REL_BUCKETS - max_exact)).astype(np.int32)
        return np.where(d < max_exact, d, np.minimum(large, REL_BUCKETS - 1))

    b32, b64 = buckets(np.float32), buckets(np.float64)
    assert (b32 == b64).all() and (np.diff(b32) >= 0).all() and b32[-1] == REL_BUCKETS - 1
    return [int(np.argmax(b32 > k)) for k in range(REL_BUCKETS - 1)]


_UPPER = _bucket_upper_bounds()


def _bias_tmpl_body(tab_ref, o_ref, *, rows, cs, off, hi, cmax):
    h = pl.program_id(0)
    c0 = pl.program_id(1) * LANE
    t = lax.broadcasted_iota(jnp.int32, (rows, LANE), 0)
    c = lax.broadcasted_iota(jnp.int32, (rows, LANE), 1) + c0
    d = t + cs * c + off
    dd = jnp.maximum(d, 0)
    val = jnp.full((rows, LANE), tab_ref[REL_BUCKETS - 1, h], F32)
    for k in range(REL_BUCKETS - 2, -1, -1):
        val = jnp.where(dd < _UPPER[k], tab_ref[k, h], val)
    ok = (d >= 0) & (d <= hi) & (c < cmax)
    o_ref[0] = jnp.where(ok, val, NEG)


def _bias_template(rel_bias, rows, width, cs, off, hi=1 << 30, cmax=1 << 30):
    return pl.pallas_call(
        functools.partial(_bias_tmpl_body, rows=rows, cs=cs, off=off, hi=hi, cmax=cmax),
        grid=(N_HEADS, width // LANE),
        in_specs=[pl.BlockSpec(memory_space=pltpu.SMEM)],
        out_specs=pl.BlockSpec((1, rows, LANE), lambda h, c: (h, 0, c)),
        out_shape=jax.ShapeDtypeStruct((N_HEADS, rows, width), F32),
        compiler_params=_params(("arbitrary", "arbitrary")),
        name="bias_template",
    )(rel_bias)


def _rms(x, g):
    return x * lax.rsqrt(jnp.mean(x * x, axis=-1, keepdims=True) + EPS) * g


def _proj_a_body(x_ref, g_ref, w_ref, lng_ref, lnb_ref,
                 q_ref, kvt_ref, kvtb_ref, kv_ref, gl_ref, z_ref, u_ref, vn_ref, zg_ref):
    xn = _rms(x_ref[0], g_ref[...])
    y = _mm(xn.astype(BF16), w_ref[...])
    q_ref[0] = y[:, :QPAD] * (HEAD_DIM ** -0.5)
    kv = y[:, C_KV:C_Z]
    kvt = kv.T
    kv_ref[0] = kv
    kvt_ref[0] = kvt
    kvtb_ref[0] = kvt.astype(BF16)
    z_ref[0] = y[:, C_Z:C_U]
    u_ref[0] = y[:, C_U:C_V]
    v = y[:, C_V:C_ZG]
    mu = jnp.mean(v, axis=-1, keepdims=True)
    var = jnp.mean(jnp.square(v - mu), axis=-1, keepdims=True)
    vn_ref[0] = (v - mu) * lax.rsqrt(var + EPS) * lng_ref[...] + lnb_ref[...]
    zg_ref[0] = y[:, C_ZG:C_G]
    gl_ref[0] = y[:, C_G:]


def _proj_a(x, norm_g, w, ln_g, ln_b, tm):
    nb, nt, _ = x.shape
    row = lambda width: pl.BlockSpec((1, tm, width), lambda b, i: (b, i, 0))
    colT = pl.BlockSpec((1, 6 * KV_WIDTH, tm), lambda b, i: (b, 0, i))
    sds = lambda shape, dt=F32: jax.ShapeDtypeStruct(shape, dt)
    return pl.pallas_call(
        _proj_a_body,
        grid=(nb, nt // tm),
        in_specs=[row(D_MODEL), _const_spec((1, D_MODEL)), _const_spec((D_MODEL, PROJ_COLS)),
                  _const_spec((1, GMLP_WIDTH)), _const_spec((1, GMLP_WIDTH))],
        out_specs=[row(QPAD), colT, colT, row(6 * KV_WIDTH), row(LANE), row(NSA_WIDTH),
                   row(GMLP_WIDTH), row(GMLP_WIDTH), row(GMLP_WIDTH)],
        out_shape=[sds((nb, nt, QPAD)), sds((nb, 6 * KV_WIDTH, nt)), sds((nb, 6 * KV_WIDTH, nt), BF16),
                   sds((nb, nt, 6 * KV_WIDTH)), sds((nb, nt, LANE)), sds((nb, nt, NSA_WIDTH)),
                   sds((nb, nt, GMLP_WIDTH)), sds((nb, nt, GMLP_WIDTH)), sds((nb, nt, GMLP_WIDTH))],
        compiler_params=_params(("arbitrary", "arbitrary")),
        name="proj_a",
    )(x, norm_g, w, ln_g, ln_b)


def _split_bf16(x):
    hi = x.astype(BF16)
    return hi, (x - hi.astype(F32)).astype(BF16)


def _cmp_prompt_body(kvt_ref, pe_ref, w_ref, o_ref):
    nt = kvt_ref.shape[2]
    blk = lax.broadcasted_iota(jnp.int32, (LANE, nt), 0)
    pos = lax.broadcasted_iota(jnp.int32, (LANE, nt), 1)
    pool = jnp.where(pos // CMP_BLOCK == blk, 1.0 / CMP_BLOCK, 0.0).astype(BF16)
    hi, lo = _split_bf16(kvt_ref[0])
    mean = _nt(pool, hi) + _nt(pool, lo)
    mean = mean + jnp.mean(pe_ref[...], axis=0, keepdims=True)
    o_ref[0] = _mm(mean.astype(BF16), w_ref[...])


def _cmp_prompt(kvt, pe_rows, w_bd):
    nb, _, nt = kvt.shape
    return pl.pallas_call(
        _cmp_prompt_body,
        grid=(nb,),
        in_specs=[pl.BlockSpec((1, 2 * KV_WIDTH, nt), lambda b: (b, 0, 0)),
                  _const_spec((CMP_BLOCK, 2 * KV_WIDTH)), _const_spec((2 * KV_WIDTH, 2 * KV_WIDTH))],
        out_specs=pl.BlockSpec((1, LANE, 2 * KV_WIDTH), lambda b: (b, 0, 0)),
        out_shape=jax.ShapeDtypeStruct((nb, LANE, 2 * KV_WIDTH), F32),
        compiler_params=_params(("arbitrary",)),
        name="cmp_prompt",
    )(kvt, pe_rows, w_bd)


def _select_blocks(score_t, cur):
    shape = score_t.shape
    blk = lax.broadcasted_iota(jnp.int32, shape, 0)
    forced = (blk == 0) | (blk == cur) | (blk == cur - 1)
    score = jnp.where(forced, jnp.inf, jnp.where(blk < cur, score_t, -jnp.inf))

    def pick(_, carry):
        score, unsel = carry
        mx = jnp.max(score, axis=0, keepdims=True)
        idx = jnp.min(jnp.where(score == mx, blk, shape[0]), axis=0, keepdims=True)
        hit = blk == idx
        unsel = jnp.where(hit & (mx > -jnp.inf), 0.0, unsel)
        return jnp.where(hit, -jnp.inf, score), unsel

    _, unsel = lax.fori_loop(0, N_SEL, pick, (score, jnp.ones(shape, F32)))
    return unsel


def _softmax_rows(s):
    m = jnp.max(s, axis=-1, keepdims=True)
    e = jnp.exp(s - m)
    return e / jnp.sum(e, axis=-1, keepdims=True)


def _attn_prompt_body(tab_ref, q_ref, cmp_ref, ks_ref, vs_ref, kw_ref, vw_ref, gl_ref, bg_ref,
                      e_ref, tc_ref, l_ref, lw_ref, o_ref, m_scr, s_scr, acc_scr):
    i = pl.program_id(1)
    q = q_ref[0].astype(BF16)
    qh = [q[:, h * LANE:(h + 1) * LANE] for h in range(N_HEADS)]
    lane = lax.broadcasted_iota(jnp.int32, (Q_BLOCK, LANE), 1)

    kc = cmp_ref[0, :, :KV_WIDTH].astype(BF16)
    vc = cmp_ref[0, :, KV_WIDTH:].astype(BF16)
    seen = lane <= 2 * i + 1
    shift = (2 * i + 2) % LANE
    o_cmp, p_sum = [], [jnp.zeros((Q_BLOCK, LANE), F32) for _ in range(N_KV)]
    for h in range(N_HEADS):
        tmpl = pltpu.roll(tc_ref[h], shift, axis=1)
        ok = seen & (tmpl > 0.5 * NEG)
        p = _softmax_rows(_nt(qh[h], kc) + jnp.where(ok, tmpl, NEG))
        p = jnp.where(ok, p, 0.0)
        o_cmp.append(_mm(p.astype(BF16), vc))
        p_sum[h // HPG] = p_sum[h // HPG] + p

    tpos = i * Q_BLOCK + lax.broadcasted_iota(jnp.int32, (LANE, Q_BLOCK), 1)
    unsel = [_select_blocks(p_sum[g].T, tpos // CMP_BLOCK).T.astype(BF16) for g in range(N_KV)]

    m_scr[...] = jnp.full(m_scr.shape, NEG, F32)
    s_scr[...] = jnp.zeros(s_scr.shape, F32)
    acc_scr[...] = jnp.zeros(acc_scr.shape, F32)

    def tile(j, near):
        k0 = pl.multiple_of(j * KV_TILE, KV_TILE)
        kt = ks_ref[0, :, pl.ds(k0, KV_TILE)]
        vt = vs_ref[0, :, pl.ds(k0, KV_TILE)]
        ej = e_ref[:, pl.ds(k0, KV_TILE)]
        if near:
            off = pl.multiple_of(NEAR_SPAN - (i * Q_BLOCK - j * KV_TILE), LANE)
        for g in range(N_KV):
            madd = _mm(unsel[g], ej)
            for h in range(g * HPG, (g + 1) * HPG):
                bias = l_ref[h, :, pl.ds(off, KV_TILE)] if near else tab_ref[REL_BUCKETS - 1, h]
                s = _mm(qh[h], kt) + madd + bias
                m_old = m_scr[h]
                m_new = jnp.maximum(m_old, jnp.max(s, axis=-1, keepdims=True))
                alpha = jnp.exp(m_old - m_new)
                p = jnp.exp(s - m_new)
                s_scr[h] = alpha * s_scr[h] + jnp.sum(p, axis=-1, keepdims=True)
                acc_scr[h] = alpha * acc_scr[h] + _nt(p.astype(BF16), vt)
                m_scr[h] = m_new

    n_far = jnp.maximum(i * Q_BLOCK - (NEAR_SPAN - LANE), 0) // KV_TILE
    n_tiles = i // (KV_TILE // Q_BLOCK) + 1
    lax.fori_loop(0, n_far, lambda j, c: (tile(j, False), c)[1], 0)
    lax.fori_loop(n_far, n_tiles, lambda j, c: (tile(j, True), c)[1], 0)

    w0 = pl.multiple_of(jnp.maximum(i * Q_BLOCK - WINDOW, 0), LANE)
    woff = pl.multiple_of(WINDOW - (i * Q_BLOCK - w0), LANE)
    kwt = kw_ref[0, :, pl.ds(w0, WIN_KEYS)]
    vwt = vw_ref[0, :, pl.ds(w0, WIN_KEYS)]
    o_win = []
    for h in range(N_HEADS):
        p = _softmax_rows(_mm(qh[h], kwt) + lw_ref[h, :, pl.ds(woff, WIN_KEYS)])
        o_win.append(_nt(p.astype(BF16), vwt))

    gate = jax.nn.sigmoid(gl_ref[0] + bg_ref[...])

    def merged(h):
        col = lambda k: gate[:, k * N_HEADS + h:k * N_HEADS + h + 1]
        return col(0) * o_cmp[h] + col(1) * (acc_scr[h] / s_scr[h]) + col(2) * o_win[h]

    for c in range(HPG):
        o_ref[0, :, c * LANE:(c + 1) * LANE] = jnp.where(lane < HEAD_DIM, merged(c), merged(HPG + c))


def _attn_prompt(rel_bias, q, cmp, kvtb, gl, bg, emask, t_cmp, t_near, t_win):
    nb, nt, _ = q.shape
    kv_spec = lambda r: pl.BlockSpec((1, KV_WIDTH, nt), lambda b, i: (b, r, 0))
    whole = lambda a: pl.BlockSpec(a.shape, lambda b, i: (0,) * a.ndim, pipeline_mode=pl.Buffered(1))
    return pl.pallas_call(
        _attn_prompt_body,
        grid=(nb, nt // Q_BLOCK),
        in_specs=[pl.BlockSpec(memory_space=pltpu.SMEM),
                  pl.BlockSpec((1, Q_BLOCK, QPAD), lambda b, i: (b, i, 0)),
                  pl.BlockSpec((1, LANE, 2 * KV_WIDTH), lambda b, i: (b, 0, 0)),
                  kv_spec(2), kv_spec(3), kv_spec(4), kv_spec(5),
                  pl.BlockSpec((1, Q_BLOCK, LANE), lambda b, i: (b, i, 0)),
                  _const_spec((1, LANE)),
                  whole(emask), whole(t_cmp), whole(t_near), whole(t_win)],
        out_specs=pl.BlockSpec((1, Q_BLOCK, NSA_WIDTH), lambda b, i: (b, i, 0)),
        out_shape=jax.ShapeDtypeStruct((nb, nt, NSA_WIDTH), F32),
        scratch_shapes=[pltpu.VMEM((N_HEADS, Q_BLOCK, 1), F32), pltpu.VMEM((N_HEADS, Q_BLOCK, 1), F32),
                        pltpu.VMEM((N_HEADS, Q_BLOCK, KV_WIDTH), F32)],
        compiler_params=_params(("arbitrary", "arbitrary")),
        name="attn_prompt",
    )(rel_bias, q, cmp, kvtb, kvtb, kvtb, kvtb, gl, bg, emask, t_cmp, t_near, t_win)


def _silu(x):
    return x * jax.nn.sigmoid(x)


def _post_a_body(o_ref, z_ref, u_ref, vn_ref, zg_ref, x_ref, wsp_ref, bsp_ref, wo_ref, y_ref):
    chunk = wsp_ref.shape[1]
    r = lax.broadcasted_iota(jnp.int32, (chunk, chunk), 0)
    c = lax.broadcasted_iota(jnp.int32, (chunk, chunk), 1)
    vn = vn_ref[0].astype(BF16)
    mix = []
    for g in range(GMLP_GROUPS):
        wm = jnp.where(c <= r, wsp_ref[g], 0.0).astype(BF16)
        mix.append(_mm(wm, vn[:, g * LANE:(g + 1) * LANE]) + bsp_ref[:, g:g + 1])
    y_g = u_ref[0] * jnp.concatenate(mix, axis=-1) * _silu(zg_ref[0])
    y_nsa = o_ref[0] * _silu(z_ref[0])
    cat = jnp.concatenate([y_nsa, y_g], axis=-1).astype(BF16)
    y_ref[0] = x_ref[0] + _mm(cat, wo_ref[...])


def _post_a(o, z, u, vn, zg, x, wsp, bsp, wo):
    nb, nt, _ = x.shape
    chunk = wsp.shape[1]
    row = lambda width: pl.BlockSpec((1, chunk, width), lambda b, i: (b, i, 0))
    return pl.pallas_call(
        _post_a_body,
        grid=(nb, nt // chunk),
        in_specs=[row(NSA_WIDTH), row(NSA_WIDTH), row(GMLP_WIDTH), row(GMLP_WIDTH), row(GMLP_WIDTH),
                  row(D_MODEL), _const_spec(wsp.shape), _const_spec(bsp.shape), _const_spec(wo.shape)],
        out_specs=row(D_MODEL),
        out_shape=jax.ShapeDtypeStruct(x.shape, F32),
        compiler_params=_params(("arbitrary", "arbitrary")),
        name="post_a",
    )(o, z, u, vn, zg, x, wsp, bsp, wo)


SAMPLE_ROWS = 8
N_PAGES_MAX = 16


def _stack_heads(q_ref):
    return jnp.concatenate([q_ref[0, :, h * LANE:(h + 1) * LANE] for h in range(N_HEADS)], axis=0).astype(BF16)


def _pad_rows(x, rows):
    return jnp.concatenate([x, jnp.zeros((rows - x.shape[0], x.shape[1]), x.dtype)], axis=0)


def _cmp_sample_body(pt_ref, q_ref, kv_ref, pe_ref, wk_ref, wv_ref, tc_ref, *rest):
    n_pages = len(rest) - 2
    pages, (oc_ref, ps_ref) = rest[:n_pages], rest[n_pages:]
    lane = lax.broadcasted_iota(jnp.int32, (KV_WIDTH, LANE), 1)
    eye = lax.broadcasted_iota(jnp.int32, (KV_WIDTH, LANE), 0) == lane
    qs = _stack_heads(q_ref)
    summ = []
    for kind in range(2):
        sums = jnp.zeros((KV_WIDTH, LANE), F32)
        for p, pg in enumerate(pages):
            x = pg[kind].reshape(KV_WIDTH, PAGE)
            lo = jnp.sum(jnp.where(lane < CMP_BLOCK, x, 0.0), axis=1, keepdims=True)
            hi = jnp.sum(jnp.where(lane < CMP_BLOCK, 0.0, x), axis=1, keepdims=True)
            sums = jnp.where(lane == 2 * p, lo, jnp.where(lane == 2 * p + 1, hi, sums))
        new = jnp.sum(kv_ref[0, :, kind * KV_WIDTH:(kind + 1) * KV_WIDTH], axis=0, keepdims=True)
        new_col = jnp.sum(jnp.where(eye, new, 0.0), axis=1, keepdims=True)
        sums = jnp.where(lane == 2 * n_pages, new_col, sums)
        mean_t = sums * (1.0 / CMP_BLOCK) + jnp.mean(pe_ref[kind], axis=1, keepdims=True)
        w_ref = wk_ref if kind == 0 else wv_ref
        summ.append(_mm(w_ref[...], mean_t.astype(BF16)).astype(BF16))
    tmpl = tc_ref[...]
    ok = tmpl > 0.5 * NEG
    p = jnp.where(ok, _softmax_rows(_mm(qs, summ[0]) + tmpl), 0.0)
    oc_ref[0] = _nt(p.astype(BF16), summ[1])
    ps_ref[0] = p.reshape(N_KV, HPG, SAMPLE_ROWS, LANE).sum(axis=1).reshape(N_KV * SAMPLE_ROWS, LANE)


def _page_specs(layer, n_pages, kind_block):
    def spec(p):
        return pl.BlockSpec((None, None, 2, N_KV, HEAD_DIM, PAGE),
                            lambda b, pt: (layer, pt[b * n_pages + p], kind_block, 0, 0, 0))
    return [spec(p) for p in range(n_pages)]


def _cmp_sample(layer, pt_flat, n_pages, cache_t, q, kv, pe_t, wk_t, wv_t, t_cmp):
    nseq = q.shape[0]
    rows = N_HEADS * SAMPLE_ROWS
    seq = lambda r, w: pl.BlockSpec((1, r, w), lambda b, pt: (b, 0, 0))
    cst = lambda a: pl.BlockSpec(a.shape, lambda b, pt: (0,) * a.ndim)
    return pl.pallas_call(
        _cmp_sample_body,
        grid_spec=pltpu.PrefetchScalarGridSpec(
            num_scalar_prefetch=1, grid=(nseq,),
            in_specs=[seq(SAMPLE_ROWS, QPAD), seq(SAMPLE_ROWS, 6 * KV_WIDTH), cst(pe_t), cst(wk_t), cst(wv_t),
                      cst(t_cmp)] + _page_specs(layer, n_pages, 0),
            out_specs=[seq(rows, KV_WIDTH), seq(N_KV * SAMPLE_ROWS, LANE)]),
        out_shape=[jax.ShapeDtypeStruct((nseq, rows, KV_WIDTH), F32),
                   jax.ShapeDtypeStruct((nseq, N_KV * SAMPLE_ROWS, LANE), F32)],
        compiler_params=_params(("arbitrary",)),
        name="cmp_sample",
    )(pt_flat, q, kv, pe_t, wk_t, wv_t, t_cmp, *([cache_t] * n_pages))


def _select_sample_body(ps_ref, o_ref, *, past_len):
    cols = ps_ref.shape[1]
    t = lax.broadcasted_iota(jnp.int32, (1, cols), 1) % SAMPLE_ROWS
    o_ref[...] = _select_blocks(ps_ref[...], (past_len + t) // CMP_BLOCK)


def _select_sample(ps_t, past_len):
    cols = ps_t.shape[1]
    tile = min(cols, 2 * LANE)
    spec = pl.BlockSpec((LANE, tile), lambda i: (0, i))
    return pl.pallas_call(
        functools.partial(_select_sample_body, past_len=past_len),
        grid=(cols // tile,), in_specs=[spec], out_specs=spec,
        out_shape=jax.ShapeDtypeStruct(ps_t.shape, F32),
        compiler_params=_params(("arbitrary",)),
        name="select_sample",
    )(ps_t)


def _attn_sample_body(pt_ref, q_ref, kv_ref, un_ref, oc_ref, gl_ref, bg_ref, e_ref, ts_ref, tn_ref, tw_ref,
                      win_ref, *rest):
    pages, o_ref = rest[:-1], rest[-1]
    qs = _stack_heads(q_ref)
    lane = lax.broadcasted_iota(jnp.int32, (SAMPLE_ROWS, LANE), 1)
    tn = tn_ref[...]

    def new_rows(col):
        return _pad_rows(kv_ref[0, :, col * KV_WIDTH:(col + 1) * KV_WIDTH], LANE).astype(BF16)

    def branch(k_past, v_past, bias, k_col, v_col):
        s = _mm(qs, k_past) + bias
        s_new = _nt(qs, new_rows(k_col)) + tn
        m = jnp.maximum(jnp.max(s, axis=-1, keepdims=True), jnp.max(s_new, axis=-1, keepdims=True))
        p, p_new = jnp.exp(s - m), jnp.exp(s_new - m)
        den = jnp.sum(p, axis=-1, keepdims=True) + jnp.sum(p_new, axis=-1, keepdims=True)
        return (_nt(p.astype(BF16), v_past) + _mm(p_new.astype(BF16), new_rows(v_col))) / den

    cat = lambda kind: jnp.concatenate([pg[kind].reshape(KV_WIDTH, PAGE) for pg in pages], axis=1).astype(BF16)
    madd = _mm(un_ref[0].astype(BF16), e_ref[...])
    madd = jnp.concatenate([madd[g * SAMPLE_ROWS:(g + 1) * SAMPLE_ROWS] for g in range(N_KV) for _ in range(HPG)],
                           axis=0)
    o_sel = branch(cat(0), cat(1), ts_ref[...] + madd, 2, 3)
    win = lambda kind: win_ref[kind].reshape(KV_WIDTH, win_ref.shape[-1]).astype(BF16)
    o_win = branch(win(0), win(1), tw_ref[...], 4, 5)

    gate = jax.nn.sigmoid(gl_ref[0] + bg_ref[...])
    oc = oc_ref[0]

    def merged(h):
        rows = slice(h * SAMPLE_ROWS, (h + 1) * SAMPLE_ROWS)
        col = lambda k: gate[:, k * N_HEADS + h:k * N_HEADS + h + 1]
        return col(0) * oc[rows] + col(1) * o_sel[rows] + col(2) * o_win[rows]

    for c in range(HPG):
        o_ref[0, :, c * LANE:(c + 1) * LANE] = jnp.where(lane < HEAD_DIM, merged(c), merged(HPG + c))


def _attn_sample(layer, pt_flat, n_pages, cache_t, win_t, q, kv, unsel, oc, gl, bg, emask, t_sel, t_new, t_win):
    nseq = q.shape[0]
    seq = lambda a: pl.BlockSpec((1,) + a.shape[1:], lambda b, pt: (b,) + (0,) * (a.ndim - 1))
    cst = lambda a: pl.BlockSpec(a.shape, lambda b, pt: (0,) * a.ndim)
    win_spec = pl.BlockSpec((None, None, 2, N_KV, HEAD_DIM, win_t.shape[-1]),
                            lambda b, pt: (layer, b, 0, 0, 0, 0))
    return pl.pallas_call(
        _attn_sample_body,
        grid_spec=pltpu.PrefetchScalarGridSpec(
            num_scalar_prefetch=1, grid=(nseq,),
            in_specs=[seq(q), seq(kv), seq(unsel), seq(oc), seq(gl), cst(bg), cst(emask), cst(t_sel), cst(t_new),
                      cst(t_win), win_spec] + _page_specs(layer, n_pages, 1),
            out_specs=pl.BlockSpec((1, SAMPLE_ROWS, NSA_WIDTH), lambda b, pt: (b, 0, 0))),
        out_shape=jax.ShapeDtypeStruct((nseq, SAMPLE_ROWS, NSA_WIDTH), F32),
        compiler_params=_params(("arbitrary",)),
        name="attn_sample",
    )(pt_flat, q, kv, unsel, oc, gl, bg, emask, t_sel, t_new, t_win, win_t, *([cache_t] * n_pages))


def _proj_c_body(x_ref, g_ref, w_ref, xb_ref, z_ref):
    y = _mm(_rms(x_ref[...], g_ref[...]).astype(BF16), w_ref[...])
    xb_ref[...] = y[:, :LRU_WIDTH]
    z_ref[...] = y[:, LRU_WIDTH:]


def _proj_c(x, norm_g, w, tm):
    n = x.shape[0]
    row = pl.BlockSpec((tm, LRU_WIDTH), lambda i: (i, 0))
    return pl.pallas_call(
        _proj_c_body,
        grid=(n // tm,),
        in_specs=[row, _const_spec((1, D_MODEL)), _const_spec((D_MODEL, 2 * LRU_WIDTH))],
        out_specs=[row, row],
        out_shape=[jax.ShapeDtypeStruct((n, LRU_WIDTH), F32)] * 2,
        compiler_params=_params(("arbitrary",)),
        name="proj_c",
    )(x, norm_g, w)


def _lru_coeffs(xc, wra_ref, bra_ref, wrx_ref, brx_ref, lam_ref):
    xb = xc.astype(BF16)
    blocks = lambda w_ref: jnp.concatenate(
        [_mm(xb[:, k * LRU_BLOCK:(k + 1) * LRU_BLOCK], w_ref[k]) for k in range(LRU_HEADS)], axis=-1)
    r = jax.nn.sigmoid(blocks(wra_ref) + bra_ref[...])
    ig = jax.nn.sigmoid(blocks(wrx_ref) + brx_ref[...])
    log_a = -LRU_C * r * jax.nn.softplus(-lam_ref[...])
    return jnp.exp(log_a), jnp.sqrt(-_expm1(2.0 * log_a)) * ig * xc


def _expm1(x):
    u = jnp.exp(x)
    small = x > -1.0
    us = jnp.where(small & (u != 1.0), u, 0.5)
    return jnp.where(small, jnp.where(u == 1.0, x, (us - 1.0) * x / jnp.log(us)), u - 1.0)


def _mix_c_prompt_body(xb_ref, z_ref, x_ref, cw_ref, cb_ref, wra_ref, bra_ref, wrx_ref, brx_ref, lam_ref, wo_ref,
                       fg_ref, y_ref, hl_ref, tail_ref, xin_scr, a_scr, b_scr, h_scr, *, final_norm):
    tt = xb_ref.shape[1]
    pad = 8
    hist = CONV_WIDTH - 1

    @pl.when(pl.program_id(1) == 0)
    def _():
        xin_scr[0:pad] = jnp.zeros((pad, LRU_WIDTH), F32)
        h_scr[...] = jnp.zeros(h_scr.shape, F32)

    xin_scr[pad:pad + tt] = xb_ref[0]
    xc = cb_ref[...] + xin_scr[pad - hist:pad - hist + tt] * cw_ref[0:1]
    for k in range(1, CONV_WIDTH):
        xc = xc + xin_scr[pad - hist + k:pad - hist + k + tt] * cw_ref[k:k + 1]
    a, b = _lru_coeffs(xc, wra_ref, bra_ref, wrx_ref, brx_ref, lam_ref)
    a_scr[...] = a
    b_scr[...] = b

    row = lax.broadcasted_iota(jnp.int32, (8, LRU_WIDTH), 0)

    def step(k, h_prev):
        r0 = pl.multiple_of(k * 8, 8)
        ca, cbv = a_scr[pl.ds(r0, 8)], b_scr[pl.ds(r0, 8)]
        for s in (1, 2, 4):
            keep = row >= s
            cbv = cbv + ca * jnp.where(keep, pltpu.roll(cbv, s, axis=0), 0.0)
            ca = ca * jnp.where(keep, pltpu.roll(ca, s, axis=0), 1.0)
        hs = ca * h_prev + cbv
        b_scr[pl.ds(r0, 8)] = hs
        return jnp.broadcast_to(hs[7:8], (8, LRU_WIDTH))

    h_scr[...] = lax.fori_loop(0, tt // 8, step, h_scr[...])
    y = (b_scr[...] * _silu(z_ref[0])).astype(BF16)
    out = x_ref[0] + _mm(y, wo_ref[...])
    y_ref[0] = _rms(out, fg_ref[...]) if final_norm else out
    hl_ref[0] = h_scr[0:1]
    tail_ref[0] = xin_scr[pad + tt - hist:pad + tt]
    xin_scr[0:pad] = xin_scr[tt:tt + pad]


def _mix_c_prompt(xb, z, x, cw, cb, wra, bra, wrx, brx, lam, wo, fg, final_norm, tt=256):
    nb, nt, _ = x.shape
    row = pl.BlockSpec((1, tt, LRU_WIDTH), lambda b, i: (b, i, 0))
    cst = lambda a: _const_spec(a.shape)
    hist = CONV_WIDTH - 1
    return pl.pallas_call(
        functools.partial(_mix_c_prompt_body, final_norm=final_norm),
        grid=(nb, nt // tt),
        in_specs=[row, row, row, cst(cw), cst(cb), cst(wra), cst(bra), cst(wrx), cst(brx), cst(lam), cst(wo),
                  cst(fg)],
        out_specs=[row, pl.BlockSpec((1, 1, LRU_WIDTH), lambda b, i: (b, 0, 0)),
                   pl.BlockSpec((1, hist, LRU_WIDTH), lambda b, i: (b, 0, 0))],
        out_shape=[jax.ShapeDtypeStruct(x.shape, F32), jax.ShapeDtypeStruct((nb, 1, LRU_WIDTH), F32),
                   jax.ShapeDtypeStruct((nb, hist, LRU_WIDTH), F32)],
        scratch_shapes=[pltpu.VMEM((tt + 8, LRU_WIDTH), F32), pltpu.VMEM((tt, LRU_WIDTH), F32),
                        pltpu.VMEM((tt, LRU_WIDTH), F32), pltpu.VMEM((8, LRU_WIDTH), F32)],
        compiler_params=_params(("arbitrary", "arbitrary")),
        name="mix_c_prompt",
    )(xb, z, x, cw, cb, wra, bra, wrx, brx, lam, wo, fg)


def _mix_c_sample_body(xb_ref, z_ref, x_ref, h0_ref, c0_ref, cw_ref, cb_ref, wra_ref, bra_ref, wrx_ref, brx_ref,
                       lam_ref, wo_ref, fg_ref, y_ref, hl_ref, tail_ref, *, final_norm):
    nt = xb_ref.shape[0]
    hist = CONV_WIDTH - 1
    xin = [c0_ref[k] for k in range(hist)] + [xb_ref[t] for t in range(nt)]
    h = h0_ref[...]
    for t in range(nt):
        xc = cb_ref[...] + xin[t] * cw_ref[0:1]
        for k in range(1, CONV_WIDTH):
            xc = xc + xin[t + k] * cw_ref[k:k + 1]
        a, b = _lru_coeffs(xc, wra_ref, bra_ref, wrx_ref, brx_ref, lam_ref)
        h = a * h + b
        out = x_ref[t] + _mm((h * _silu(z_ref[t])).astype(BF16), wo_ref[...])
        y_ref[t] = _rms(out, fg_ref[...]) if final_norm else out
    hl_ref[...] = h
    for k in range(hist):
        tail_ref[k] = xin[nt + k]


def _mix_c_sample(xb, z, x, h0, c0, cw, cb, wra, bra, wrx, brx, lam, wo, fg, final_norm):
    args = (xb, z, x, h0, c0, cw, cb, wra, bra, wrx, brx, lam, wo, fg)
    return pl.pallas_call(
        functools.partial(_mix_c_sample_body, final_norm=final_norm),
        grid=(1,),
        in_specs=[_const_spec(a.shape) for a in args],
        out_specs=[_const_spec(x.shape), _const_spec(h0.shape), _const_spec(c0.shape)],
        out_shape=[jax.ShapeDtypeStruct(x.shape, F32), jax.ShapeDtypeStruct(h0.shape, F32),
                   jax.ShapeDtypeStruct(c0.shape, F32)],
        compiler_params=_params(("arbitrary",)),
        name="mix_c_sample",
    )(*args)


def _pair_perm():
    cols = []
    for c in range(HPG):
        cols += list(range(c * HEAD_DIM, (c + 1) * HEAD_DIM))
        cols += list(range((HPG + c) * HEAD_DIM, (HPG + c + 1) * HEAD_DIM))
    return np.asarray(cols)


def _layout_w_in_a(w):
    o_kv = NSA_WIDTH
    o_g = o_kv + 6 * KV_WIDTH
    o_z = o_g + 3 * N_HEADS
    o_rest = o_z + NSA_WIDTH
    wq = w[:, :NSA_WIDTH].reshape(D_MODEL, N_HEADS, HEAD_DIM)
    zeros = jnp.zeros_like(wq)
    first = (np.arange(N_HEADS) < HPG)[None, :, None]
    wq = jnp.concatenate([jnp.where(first, wq, zeros), jnp.where(first, zeros, wq)], axis=-1)
    wg = jnp.pad(w[:, o_g:o_z], ((0, 0), (0, LANE - 3 * N_HEADS)))
    wz = w[:, o_z:o_rest][:, _pair_perm()]
    return jnp.concatenate([wq.reshape(D_MODEL, QPAD), w[:, o_kv:o_g], wz, w[:, o_rest:], wg], axis=1).astype(BF16)


def _block_diag(w, n):
    return jnp.kron(jnp.eye(n, dtype=w.dtype), w)


def _even_layer(a, yp, ys, cache_t, win_t, pt_flat, n_pages, tmpl, p):
    nb, nt, _ = yp.shape
    nseq, dec_t, _ = ys.shape
    past_len = n_pages * PAGE
    w_in = _layout_w_in_a(p['w_in_a'][a])
    norm_g = p['norm_a'][a][None]
    ln_g, ln_b = p['gmlp_ln_g'][a][None], p['gmlp_ln_b'][a][None]
    bg = jnp.pad(p['b_gate_a'][a], (0, LANE - 3 * N_HEADS))[None]
    w_out = jnp.concatenate([p['w_out_a'][a][:NSA_WIDTH][_pair_perm()], p['w_out_a'][a][NSA_WIDTH:]], axis=0).astype(BF16)
    pe, wc = p['pe_cmp'][a], p['w_cmp'][a]

    q, kvt, kvtb, _, gl, z, u, vn, zg = _proj_a(yp, norm_g, w_in, ln_g, ln_b, tm=256)
    pe_rows = jnp.concatenate([pe[0], pe[0], pe[1], pe[1]], axis=1)
    w_bd = jnp.zeros((2 * KV_WIDTH, 2 * KV_WIDTH), F32)
    for k in range(4):
        w_bd = w_bd.at[k * HEAD_DIM:(k + 1) * HEAD_DIM, k * HEAD_DIM:(k + 1) * HEAD_DIM].set(wc[k // 2])
    cmp = _cmp_prompt(kvt, pe_rows, w_bd.astype(BF16))
    o = _attn_prompt(p['rel_bias'], q, cmp, kvtb, gl, bg, tmpl['e_prompt'], tmpl['cmp'], tmpl['near'], tmpl['win'])
    yp_new = _post_a(o, z, u, vn, zg, yp, p['w_spatial'][a], p['b_spatial'][a].T, w_out)
    rows_p = jnp.transpose(kvt[:, :4 * KV_WIDTH].reshape(nb, 4, N_KV, HEAD_DIM, nt), (0, 4, 1, 2, 3))
    wlen = min(WINDOW, nt)
    win_p = jnp.transpose(kvt[:, 4 * KV_WIDTH:, nt - wlen:].reshape(nb, 2, N_KV, HEAD_DIM, wlen), (0, 4, 1, 2, 3))

    xs = ys.reshape(1, nseq * dec_t, D_MODEL)
    q, _, _, kv, gl, z, u, vn, zg = _proj_a(xs, norm_g, w_in, ln_g, ln_b, tm=nseq * dec_t)
    pad_t = lambda x: jnp.pad(x.reshape(nseq, dec_t, -1), ((0, 0), (0, SAMPLE_ROWS - dec_t), (0, 0)))
    q8, kv8, gl8 = pad_t(q), pad_t(kv), pad_t(gl)
    pe_t = jnp.concatenate([jnp.swapaxes(pe, 1, 2)] * N_KV, axis=1)
    wk_t = _block_diag(wc[0].T, N_KV).astype(BF16)
    wv_t = _block_diag(wc[1].T, N_KV).astype(BF16)
    oc, ps = _cmp_sample(a, pt_flat, n_pages, cache_t, q8, kv8, pe_t, wk_t, wv_t, tmpl['s_cmp'])
    unsel_t = _select_sample(ps.reshape(nseq * N_KV * SAMPLE_ROWS, LANE).T, past_len)
    unsel = unsel_t.T.reshape(nseq, N_KV * SAMPLE_ROWS, LANE)
    o8 = _attn_sample(a, pt_flat, n_pages, cache_t, win_t, q8, kv8, unsel, oc, gl8, bg, tmpl['e_sample'],
                      tmpl['s_sel'], tmpl['s_new'], tmpl['s_win'])
    o = o8[:, :dec_t].reshape(1, nseq * dec_t, NSA_WIDTH)
    w_sp = _block_diag_groups(jnp.tril(p['w_spatial'][a][:, :dec_t, :dec_t]), nseq)
    b_sp = jnp.tile(p['b_spatial'][a][:, :dec_t].T, (nseq, 1))
    ys_new = _post_a(o, z, u, vn, zg, xs, w_sp, b_sp, w_out).reshape(nseq, dec_t, D_MODEL)
    rows_s = kv[0, :, :4 * KV_WIDTH].reshape(nseq, dec_t, 4, N_KV, HEAD_DIM)
    win_s = kv[0, :, 4 * KV_WIDTH:].reshape(nseq, dec_t, 2, N_KV, HEAD_DIM)
    return yp_new, ys_new, rows_p, rows_s, win_p, win_s, vn.reshape(nseq, dec_t, GMLP_WIDTH)


def _block_diag_groups(w, n):
    return jax.vmap(lambda m: _block_diag(m, n))(w)


def _odd_layer(c, yp, ys, h0, conv0, p, final):
    nb, nt, _ = yp.shape
    nseq, dec_t, _ = ys.shape
    norm_g = p['norm_c'][c][None]
    w_in = p['w_in_c'][c].astype(BF16)
    consts = (p['conv_w'][c], p['conv_b'][c][None], p['w_rg_a'][c].astype(BF16), p['b_rg_a'][c][None],
              p['w_rg_x'][c].astype(BF16), p['b_rg_x'][c][None], p['lru_lambda'][c][None],
              p['w_out_c'][c].astype(BF16), p['final_norm'][None])
    xb, z = _proj_c(yp.reshape(nb * nt, D_MODEL), norm_g, w_in, tm=512)
    yp_new, h_p, tail_p = _mix_c_prompt(xb.reshape(nb, nt, -1), z.reshape(nb, nt, -1), yp, *consts, final_norm=final)
    xs = jnp.swapaxes(ys, 0, 1)
    xb, z = _proj_c(xs.reshape(dec_t * nseq, D_MODEL), norm_g, w_in, tm=dec_t * nseq)
    tm3 = lambda x: x.reshape(dec_t, nseq, -1)
    ys_new, h_s, tail_s = _mix_c_sample(tm3(xb), tm3(z), xs, h0, jnp.swapaxes(conv0, 0, 1), *consts,
                                        final_norm=final)
    return yp_new, jnp.swapaxes(ys_new, 0, 1), h_p[:, 0], h_s, tail_p, jnp.swapaxes(tail_s, 0, 1)


def kernel(x_prompt, x_sample, cache_nsa, cache_win, state_lru_h, state_lru_conv, page_table, norm_a, w_in_a, b_gate_a, pe_cmp, w_cmp, gmlp_ln_g, gmlp_ln_b, w_spatial, b_spatial, w_out_a, norm_c, w_in_c, conv_w, conv_b, w_rg_a, b_rg_a, w_rg_x, b_rg_x, lru_lambda, w_out_c, rel_bias, final_norm):
    p = dict(norm_a=norm_a, w_in_a=w_in_a, b_gate_a=b_gate_a, pe_cmp=pe_cmp, w_cmp=w_cmp, gmlp_ln_g=gmlp_ln_g,
             gmlp_ln_b=gmlp_ln_b, w_spatial=w_spatial, b_spatial=b_spatial, w_out_a=w_out_a, norm_c=norm_c,
             w_in_c=w_in_c, conv_w=conv_w, conv_b=conv_b, w_rg_a=w_rg_a, b_rg_a=b_rg_a, w_rg_x=w_rg_x,
             b_rg_x=b_rg_x, lru_lambda=lru_lambda, w_out_c=w_out_c, rel_bias=rel_bias, final_norm=final_norm)
    nb, nt, _ = x_prompt.shape
    nseq, n_pages = page_table.shape
    past_len = n_pages * PAGE
    wlen = cache_win.shape[2]
    depth = norm_a.shape[0] + norm_c.shape[0]
    assert nt % KV_TILE == 0 and nt >= WIN_KEYS and nt // CMP_BLOCK <= LANE
    assert past_len // CMP_BLOCK < LANE and x_sample.shape[1] <= SAMPLE_ROWS and n_pages <= N_PAGES_MAX

    cache_t = jnp.transpose(cache_nsa, (0, 1, 3, 4, 5, 2))
    win_t = jnp.transpose(cache_win, (0, 1, 3, 4, 5, 2))
    pt_flat = page_table.reshape(-1).astype(jnp.int32)

    blk_of_key = lambda n: (np.arange(n) // CMP_BLOCK)[None, :] == np.arange(LANE)[:, None]
    rows_s = SAMPLE_ROWS
    stack = lambda t: t.reshape(N_HEADS * rows_s, t.shape[-1])
    tmpl = dict(
        e_prompt=jnp.asarray(np.where(blk_of_key(nt), NEG, 0.0), BF16),
        e_sample=jnp.asarray(np.where(blk_of_key(past_len), NEG, 0.0), BF16),
        cmp=_bias_template(rel_bias, Q_BLOCK, LANE, -CMP_BLOCK, CMP_BLOCK * (LANE - 2) - (CMP_BLOCK - 1)),
        near=_bias_template(rel_bias, Q_BLOCK, NEAR_SPAN + KV_TILE, -1, NEAR_SPAN),
        win=_bias_template(rel_bias, Q_BLOCK, WINDOW + WIN_KEYS, -1, WINDOW, hi=WINDOW),
        s_cmp=stack(_bias_template(rel_bias, rows_s, LANE, -CMP_BLOCK, past_len - (CMP_BLOCK - 1))),
        s_sel=stack(_bias_template(rel_bias, rows_s, past_len, -1, past_len)),
        s_new=stack(_bias_template(rel_bias, rows_s, LANE, -1, 0, cmax=x_sample.shape[1])),
        s_win=stack(_bias_template(rel_bias, rows_s, wlen, -1, wlen, hi=WINDOW)),
    )

    yp, ys = x_prompt, x_sample
    outs = [[] for _ in range(9)]
    for layer in range(depth):
        if layer % 2 == 0:
            yp, ys, *leaves = _even_layer(layer // 2, yp, ys, cache_t, win_t, pt_flat, n_pages, tmpl, p)
            for dst, leaf in zip(outs[:5], leaves):
                dst.append(leaf)
        else:
            c = layer // 2
            yp, ys, *leaves = _odd_layer(c, yp, ys, state_lru_h[c], state_lru_conv[c], p, final=layer == depth - 1)
            for dst, leaf in zip(outs[5:], leaves):
                dst.append(leaf)
    return (yp, ys) + tuple(jnp.stack(o) for o in outs)
```

```python
import functools
import math

import numpy as np
import jax
import jax.numpy as jnp
from jax import lax
from jax.experimental import pallas as pl
from jax.experimental.pallas import tpu as pltpu

F32 = jnp.float32
BF16 = jnp.bfloat16

D_MODEL = 1024
N_HEADS = 8
N_KV = 2
HPG = N_HEADS // N_KV
HEAD_DIM = 64
NSA_WIDTH = N_HEADS * HEAD_DIM
KV_WIDTH = N_KV * HEAD_DIM
CMP_BLOCK = 64
N_SEL = 16
WINDOW = 512
GMLP_GROUPS = 4
GMLP_WIDTH = 512
GMLP_CHUNK = 128
LRU_WIDTH = 1024
LRU_HEADS = 8
LRU_BLOCK = 128
CONV_WIDTH = 4
LRU_C = 8.0
REL_BUCKETS = 32
REL_MAX_DIST = 1024
EPS = 1e-6
NEG = -1e30
PAGE = 128

LANE = 128
Q_BLOCK = 128
KV_TILE = 512
NEAR_SPAN = 1280
WIN_KEYS = WINDOW + Q_BLOCK
VMEM_LIMIT = 56 * 1024 * 1024

QPAD = N_HEADS * LANE
C_KV = QPAD
C_Z = C_KV + 6 * KV_WIDTH
C_U = C_Z + NSA_WIDTH
C_V = C_U + GMLP_WIDTH
C_ZG = C_V + GMLP_WIDTH
C_G = C_ZG + GMLP_WIDTH
PROJ_COLS = C_G + LANE


def _nt(a, b):
    return lax.dot_general(a, b, (((1,), (1,)), ((), ())), preferred_element_type=F32)


def _mm(a, b):
    return jnp.dot(a, b, preferred_element_type=F32)


def _params(sem):
    return pltpu.CompilerParams(dimension_semantics=sem, vmem_limit_bytes=VMEM_LIMIT)


def _const_spec(shape):
    n = len(shape)
    return pl.BlockSpec(shape, lambda *_: (0,) * n)


def _bucket_upper_bounds():
    max_exact = REL_BUCKETS // 2
    d = np.arange(0, 4 * REL_MAX_DIST, dtype=np.int64)

    def buckets(ft):
        df = np.maximum(d, 1).astype(ft)
        large = max_exact + (np.log(df / ft(max_exact)) / ft(math.log(REL_MAX_DIST / max_exact))
                             * ft(REL_BUCKETS - max_exact)).astype(np.int32)
        return np.where(d < max_exact, d, np.minimum(large, REL_BUCKETS - 1))

    b32, b64 = buckets(np.float32), buckets(np.float64)
    assert (b32 == b64).all() and (np.diff(b32) >= 0).all() and b32[-1] == REL_BUCKETS - 1
    return [int(np.argmax(b32 > k)) for k in range(REL_BUCKETS - 1)]


_UPPER = _bucket_upper_bounds()


def _bias_tmpl_body(tab_ref, o_ref, *, rows, cs, off, hi, cmax):
    h = pl.program_id(0)
    c0 = pl.program_id(1) * LANE
    t = lax.broadcasted_iota(jnp.int32, (rows, LANE), 0)
    c = lax.broadcasted_iota(jnp.int32, (rows, LANE), 1) + c0
    d = t + cs * c + off
    dd = jnp.maximum(d, 0)
    val = jnp.full((rows, LANE), tab_ref[REL_BUCKETS - 1, h], F32)
    for k in range(REL_BUCKETS - 2, -1, -1):
        val = jnp.where(dd < _UPPER[k], tab_ref[k, h], val)
    ok = (d >= 0) & (d <= hi) & (c < cmax)
    o_ref[0] = jnp.where(ok, val, NEG)


def _bias_template(rel_bias, rows, width, cs, off, hi=1 << 30, cmax=1 << 30):
    return pl.pallas_call(
        functools.partial(_bias_tmpl_body, rows=rows, cs=cs, off=off, hi=hi, cmax=cmax),
        grid=(N_HEADS, width // LANE),
        in_specs=[pl.BlockSpec(memory_space=pltpu.SMEM)],
        out_specs=pl.BlockSpec((1, rows, LANE), lambda h, c: (h, 0, c)),
        out_shape=jax.ShapeDtypeStruct((N_HEADS, rows, width), F32),
        compiler_params=_params(("arbitrary", "arbitrary")),
        name="bias_template",
    )(rel_bias)


def _rms(x, g):
    return x * lax.rsqrt(jnp.mean(x * x, axis=-1, keepdims=True) + EPS) * g


def _proj_a_body(x_ref, g_ref, w_ref, lng_ref, lnb_ref,
                 q_ref, kvt_ref, kvtb_ref, kv_ref, gl_ref, z_ref, u_ref, vn_ref, zg_ref):
    xn = _rms(x_ref[0], g_ref[...])
    y = _mm(xn.astype(BF16), w_ref[...])
    q_ref[0] = y[:, :QPAD] * (HEAD_DIM ** -0.5)
    kv = y[:, C_KV:C_Z]
    kvt = kv.T
    kv_ref[0] = kv
    kvt_ref[0] = kvt
    kvtb_ref[0] = kvt.astype(BF16)
    z_ref[0] = y[:, C_Z:C_U]
    u_ref[0] = y[:, C_U:C_V]
    v = y[:, C_V:C_ZG]
    mu = jnp.mean(v, axis=-1, keepdims=True)
    var = jnp.mean(jnp.square(v - mu), axis=-1, keepdims=True)
    vn_ref[0] = (v - mu) * lax.rsqrt(var + EPS) * lng_ref[...] + lnb_ref[...]
    zg_ref[0] = y[:, C_ZG:C_G]
    gl_ref[0] = y[:, C_G:]


def _proj_a(x, norm_g, w, ln_g, ln_b, tm):
    nb, nt, _ = x.shape
    row = lambda width: pl.BlockSpec((1, tm, width), lambda b, i: (b, i, 0))
    colT = pl.BlockSpec((1, 6 * KV_WIDTH, tm), lambda b, i: (b, 0, i))
    sds = lambda shape, dt=F32: jax.ShapeDtypeStruct(shape, dt)
    return pl.pallas_call(
        _proj_a_body,
        grid=(nb, nt // tm),
        in_specs=[row(D_MODEL), _const_spec((1, D_MODEL)), _const_spec((D_MODEL, PROJ_COLS)),
                  _const_spec((1, GMLP_WIDTH)), _const_spec((1, GMLP_WIDTH))],
        out_specs=[row(QPAD), colT, colT, row(6 * KV_WIDTH), row(LANE), row(NSA_WIDTH),
                   row(GMLP_WIDTH), row(GMLP_WIDTH), row(GMLP_WIDTH)],
        out_shape=[sds((nb, nt, QPAD)), sds((nb, 6 * KV_WIDTH, nt)), sds((nb, 6 * KV_WIDTH, nt), BF16),
                   sds((nb, nt, 6 * KV_WIDTH)), sds((nb, nt, LANE)), sds((nb, nt, NSA_WIDTH)),
                   sds((nb, nt, GMLP_WIDTH)), sds((nb, nt, GMLP_WIDTH)), sds((nb, nt, GMLP_WIDTH))],
        compiler_params=_params(("arbitrary", "arbitrary")),
        name="proj_a",
    )(x, norm_g, w, ln_g, ln_b)


def _split_bf16(x):
    hi = x.astype(BF16)
    return hi, (x - hi.astype(F32)).astype(BF16)


def _cmp_prompt_body(kvt_ref, pe_ref, w_ref, o_ref):
    nt = kvt_ref.shape[2]
    blk = lax.broadcasted_iota(jnp.int32, (LANE, nt), 0)
    pos = lax.broadcasted_iota(jnp.int32, (LANE, nt), 1)
    pool = jnp.where(pos // CMP_BLOCK == blk, 1.0 / CMP_BLOCK, 0.0).astype(BF16)
    hi, lo = _split_bf16(kvt_ref[0])
    mean = _nt(pool, hi) + _nt(pool, lo)
    mean = mean + jnp.mean(pe_ref[...], axis=0, keepdims=True)
    o_ref[0] = _mm(mean.astype(BF16), w_ref[...])


def _cmp_prompt(kvt, pe_rows, w_bd):
    nb, _, nt = kvt.shape
    return pl.pallas_call(
        _cmp_prompt_body,
        grid=(nb,),
        in_specs=[pl.BlockSpec((1, 2 * KV_WIDTH, nt), lambda b: (b, 0, 0)),
                  _const_spec((CMP_BLOCK, 2 * KV_WIDTH)), _const_spec((2 * KV_WIDTH, 2 * KV_WIDTH))],
        out_specs=pl.BlockSpec((1, LANE, 2 * KV_WIDTH), lambda b: (b, 0, 0)),
        out_shape=jax.ShapeDtypeStruct((nb, LANE, 2 * KV_WIDTH), F32),
        compiler_params=_params(("arbitrary",)),
        name="cmp_prompt",
    )(kvt, pe_rows, w_bd)


def _select_blocks(score_t, cur):
    shape = score_t.shape
    blk = lax.broadcasted_iota(jnp.int32, shape, 0)
    forced = (blk == 0) | (blk == cur) | (blk == cur - 1)
    score = jnp.where(forced, jnp.inf, jnp.where(blk < cur, score_t, -jnp.inf))

    def pick(_, carry):
        score, unsel = carry
        mx = jnp.max(score, axis=0, keepdims=True)
        idx = jnp.min(jnp.where(score == mx, blk, shape[0]), axis=0, keepdims=True)
        hit = blk == idx
        unsel = jnp.where(hit & (mx > -jnp.inf), 0.0, unsel)
        return jnp.where(hit, -jnp.inf, score), unsel

    _, unsel = lax.fori_loop(0, N_SEL, pick, (score, jnp.ones(shape, F32)))
    return unsel


def _softmax_rows(s):
    m = jnp.max(s, axis=-1, keepdims=True)
    e = jnp.exp(s - m)
    return e / jnp.sum(e, axis=-1, keepdims=True)


GROUP_ROWS = HPG * Q_BLOCK


def _rowmax(s):
    m = s[:, :LANE]
    for k in range(1, s.shape[1] // LANE):
        m = jnp.maximum(m, s[:, k * LANE:(k + 1) * LANE])
    return jnp.broadcast_to(jnp.max(m, axis=-1, keepdims=True), (s.shape[0], LANE))


def _tile_lanes(x, n):
    return jnp.concatenate([x] * n, axis=1)


def _attn_prompt_body(q_ref, bl_ref, cmp_ref, ka_ref, va_ref, kw_ref, vw_ref, gl_ref, bg_ref,
                      tc_ref, tn_ref, tw_ref, o_ref, lhs_scr, m_scr, acc_scr):
    i = pl.program_id(1)
    qf = q_ref[0]
    lane = lax.broadcasted_iota(jnp.int32, (Q_BLOCK, LANE), 1)
    stack = lambda x, g: jnp.concatenate([x[:, h * LANE:(h + 1) * LANE] for h in range(g * HPG, (g + 1) * HPG)], axis=0)
    qs = [stack(qf, g).astype(BF16) for g in range(N_KV)]
    qs_far = [stack(qf + bl_ref[...], g).astype(BF16) for g in range(N_KV)]

    kc = cmp_ref[0, :, :KV_WIDTH].astype(BF16)
    vc = cmp_ref[0, :, KV_WIDTH:].astype(BF16)
    seen = lane <= 2 * i + 1
    shift = (2 * i + 2) % LANE
    o_cmp, p_sum = [], []
    for g in range(N_KV):
        tmpl = jnp.concatenate([jnp.where(seen, pltpu.roll(tc_ref[h], shift, axis=1), NEG)
                                for h in range(g * HPG, (g + 1) * HPG)], axis=0)
        ok = tmpl > 0.5 * NEG
        p = jnp.where(ok, _softmax_rows(_nt(qs[g], kc) + tmpl), 0.0)
        o_cmp.append(_mm(p.astype(BF16), vc))
        p_sum.append(p.reshape(HPG, Q_BLOCK, LANE).sum(axis=0))

    tpos = i * Q_BLOCK + lax.broadcasted_iota(jnp.int32, (1, N_KV * Q_BLOCK), 1) % Q_BLOCK
    unsel_t = _select_blocks(jnp.concatenate([ps.T for ps in p_sum], axis=1), tpos // CMP_BLOCK)
    for g in range(N_KV):
        un = _tile_rows(unsel_t[:, g * Q_BLOCK:(g + 1) * Q_BLOCK].T.astype(BF16), HPG)
        lhs_scr[0, g] = jnp.concatenate([qs_far[g], un], axis=1)
        lhs_scr[1, g] = jnp.concatenate([qs[g], un], axis=1)

    m_scr[...] = jnp.full(m_scr.shape, NEG, F32)
    acc_scr[...] = jnp.zeros(acc_scr.shape, F32)

    def tile(j, near):
        k0 = pl.multiple_of(j * KV_TILE, KV_TILE)
        for g in range(N_KV):
            s = _mm(lhs_scr[1 if near else 0, g], ka_ref[0, g, :, pl.ds(k0, KV_TILE)])
            if near:
                off = pl.multiple_of(NEAR_SPAN - (i * Q_BLOCK - j * KV_TILE), LANE)
                s = s + tn_ref[g, :, pl.ds(off, KV_TILE)]
            m_old = m_scr[g]
            m_new = jnp.maximum(m_old, _rowmax(s))
            p = jnp.exp(s - _tile_lanes(m_new, KV_TILE // LANE)).astype(BF16)
            acc_scr[g] = jnp.exp(m_old - m_new) * acc_scr[g] + _nt(p, va_ref[0, g, :, pl.ds(k0, KV_TILE)])
            m_scr[g] = m_new

    n_far = jnp.maximum(i * Q_BLOCK - (NEAR_SPAN + LANE) + KV_TILE, 0) // KV_TILE
    n_tiles = i // (KV_TILE // Q_BLOCK) + 1
    lax.fori_loop(0, n_far, lambda j, c: (tile(j, False), c)[1], 0)
    lax.fori_loop(n_far, n_tiles, lambda j, c: (tile(j, True), c)[1], 0)

    w0 = pl.multiple_of(jnp.maximum(i * Q_BLOCK - WINDOW, 0), LANE)
    woff = pl.multiple_of(WINDOW - (i * Q_BLOCK - w0), LANE)
    kwt = kw_ref[0, :, pl.ds(w0, WIN_KEYS)]
    o_sel, o_win = [], []
    for g in range(N_KV):
        s = _mm(qs[g], kwt) + tw_ref[g, :, pl.ds(woff, WIN_KEYS)]
        e = jnp.exp(s - _tile_lanes(_rowmax(s), WIN_KEYS // LANE)).astype(BF16)
        pv = _nt(e, vw_ref[0, g, :, pl.ds(w0, WIN_KEYS)])
        o_win.append(pv / pltpu.roll(pv, HEAD_DIM, axis=1))
        acc = acc_scr[g]
        o_sel.append(acc / pltpu.roll(acc, HEAD_DIM, axis=1))

    gate = jax.nn.sigmoid(gl_ref[0] + bg_ref[...])

    def merged(h):
        g, rows = h // HPG, slice((h % HPG) * Q_BLOCK, (h % HPG + 1) * Q_BLOCK)
        col = lambda k: gate[:, k * N_HEADS + h:k * N_HEADS + h + 1]
        return col(0) * o_cmp[g][rows] + col(1) * o_sel[g][rows] + col(2) * o_win[g][rows]

    for c in range(HPG):
        o_ref[0, :, c * LANE:(c + 1) * LANE] = jnp.where(lane < HEAD_DIM, merged(c), merged(HPG + c))


def _tile_rows(x, n):
    return jnp.concatenate([x] * n, axis=0)


def _attn_prompt(q, bias_lanes, cmp, kaug, vaug, kwin, vwaug, gl, bg, t_cmp, t_near, t_win):
    nb, nt, _ = q.shape
    once = pl.Buffered(1)
    per_b = lambda a: pl.BlockSpec((1,) + a.shape[1:], lambda b, i: (b,) + (0,) * (a.ndim - 1), pipeline_mode=once)
    whole = lambda a: pl.BlockSpec(a.shape, lambda b, i: (0,) * a.ndim, pipeline_mode=once)
    return pl.pallas_call(
        _attn_prompt_body,
        grid=(nb, nt // Q_BLOCK),
        in_specs=[pl.BlockSpec((1, Q_BLOCK, QPAD), lambda b, i: (b, i, 0)),
                  whole(bias_lanes), per_b(cmp), per_b(kaug), per_b(vaug), per_b(kwin), per_b(vwaug),
                  pl.BlockSpec((1, Q_BLOCK, LANE), lambda b, i: (b, i, 0)),
                  _const_spec((1, LANE)), whole(t_cmp), whole(t_near), whole(t_win)],
        out_specs=pl.BlockSpec((1, Q_BLOCK, NSA_WIDTH), lambda b, i: (b, i, 0)),
        out_shape=jax.ShapeDtypeStruct((nb, nt, NSA_WIDTH), F32),
        scratch_shapes=[pltpu.VMEM((2, N_KV, GROUP_ROWS, 2 * LANE), BF16),
                        pltpu.VMEM((N_KV, GROUP_ROWS, LANE), F32), pltpu.VMEM((N_KV, GROUP_ROWS, LANE), F32)],
        compiler_params=_params(("arbitrary", "arbitrary")),
        name="attn_prompt",
    )(q, bias_lanes, cmp, kaug, vaug, kwin, vwaug, gl, bg, t_cmp, t_near, t_win)


def _prompt_kv_operands(kvtb, emask):
    nb, _, nt = kvtb.shape
    half = lambda kind, g: kvtb[:, kind * KV_WIDTH + g * HEAD_DIM:kind * KV_WIDTH + (g + 1) * HEAD_DIM]
    ones = jnp.ones((nb, HEAD_DIM, nt), BF16)
    extra = ones.at[:, 2:].set(0)
    e = jnp.broadcast_to(emask[None], (nb, LANE, nt))
    kaug = jnp.stack([jnp.concatenate([half(2, 0), extra, e], axis=1),
                      jnp.concatenate([extra, half(2, 1), e], axis=1)], axis=1)
    aug = lambda kind: jnp.stack([jnp.concatenate([half(kind, 0), ones], axis=1),
                                  jnp.concatenate([ones, half(kind, 1)], axis=1)], axis=1)
    return kaug, aug(3), kvtb[:, 4 * KV_WIDTH:5 * KV_WIDTH], aug(5)


def _far_bias_lanes(rel_bias):
    b = rel_bias[REL_BUCKETS - 1]
    hi = b.astype(BF16).astype(F32)
    pair = jnp.stack([hi, b - hi], axis=1)
    tile = jnp.zeros((N_HEADS, LANE), F32)
    first = tile.at[:, HEAD_DIM:HEAD_DIM + 2].set(pair)
    second = tile.at[:, 0:2].set(pair)
    return jnp.where((np.arange(N_HEADS) < HPG)[:, None], first, second).reshape(1, QPAD)


def _silu(x):
    return x * jax.nn.sigmoid(x)


def _post_a_body(o_ref, z_ref, u_ref, vn_ref, zg_ref, x_ref, wsp_ref, bsp_ref, wo_ref, y_ref, *, short_chunks):
    mix = []
    if short_chunks:
        vn = vn_ref[0]
        for g in range(GMLP_GROUPS):
            vg = vn[:, g * LANE:(g + 1) * LANE]
            acc = bsp_ref[:, g:g + 1] + wsp_ref[0][:, g:g + 1] * vg
            for k in range(1, wsp_ref.shape[0]):
                acc = acc + wsp_ref[k][:, g:g + 1] * pltpu.roll(vg, k, axis=0)
            mix.append(acc)
    else:
        chunk = wsp_ref.shape[1]
        r = lax.broadcasted_iota(jnp.int32, (chunk, chunk), 0)
        c = lax.broadcasted_iota(jnp.int32, (chunk, chunk), 1)
        vn = vn_ref[0].astype(BF16)
        for g in range(GMLP_GROUPS):
            wm = jnp.where(c <= r, wsp_ref[g], 0.0).astype(BF16)
            mix.append(_mm(wm, vn[:, g * LANE:(g + 1) * LANE]) + bsp_ref[:, g:g + 1])
    y_g = u_ref[0] * jnp.concatenate(mix, axis=-1) * _silu(zg_ref[0])
    y_nsa = o_ref[0] * _silu(z_ref[0])
    cat = jnp.concatenate([y_nsa, y_g], axis=-1).astype(BF16)
    y_ref[0] = x_ref[0] + _mm(cat, wo_ref[...])


def _post_a(o, z, u, vn, zg, x, wsp, bsp, wo, short_chunks=False):
    nb, nt, _ = x.shape
    chunk = wsp.shape[1]
    row = lambda width: pl.BlockSpec((1, chunk, width), lambda b, i: (b, i, 0))
    return pl.pallas_call(
        functools.partial(_post_a_body, short_chunks=short_chunks),
        grid=(nb, nt // chunk),
        in_specs=[row(NSA_WIDTH), row(NSA_WIDTH), row(GMLP_WIDTH), row(GMLP_WIDTH), row(GMLP_WIDTH),
                  row(D_MODEL), _const_spec(wsp.shape), _const_spec(bsp.shape), _const_spec(wo.shape)],
        out_specs=row(D_MODEL),
        out_shape=jax.ShapeDtypeStruct(x.shape, F32),
        compiler_params=_params(("arbitrary", "arbitrary")),
        name="post_a",
    )(o, z, u, vn, zg, x, wsp, bsp, wo)


SAMPLE_ROWS = 8
N_PAGES_MAX = 16


def _stack_heads(q_ref):
    return jnp.concatenate([q_ref[0, :, h * LANE:(h + 1) * LANE] for h in range(N_HEADS)], axis=0).astype(BF16)


def _pad_rows(x, rows):
    return jnp.concatenate([x, jnp.zeros((rows - x.shape[0], x.shape[1]), x.dtype)], axis=0)


def _cmp_sample_body(pt_ref, q_ref, kv_ref, pe_ref, wk_ref, wv_ref, tc_ref, *rest):
    n_pages = len(rest) - 2
    pages, (oc_ref, ps_ref) = rest[:n_pages], rest[n_pages:]
    lane = lax.broadcasted_iota(jnp.int32, (KV_WIDTH, LANE), 1)
    eye = lax.broadcasted_iota(jnp.int32, (KV_WIDTH, LANE), 0) == lane
    qs = _stack_heads(q_ref)
    summ = []
    for kind in range(2):
        sums = jnp.zeros((KV_WIDTH, LANE), F32)
        for p, pg in enumerate(pages):
            x = pg[kind].reshape(KV_WIDTH, PAGE)
            lo = jnp.sum(jnp.where(lane < CMP_BLOCK, x, 0.0), axis=1, keepdims=True)
            hi = jnp.sum(jnp.where(lane < CMP_BLOCK, 0.0, x), axis=1, keepdims=True)
            sums = jnp.where(lane == 2 * p, lo, jnp.where(lane == 2 * p + 1, hi, sums))
        new = jnp.sum(kv_ref[0, :, kind * KV_WIDTH:(kind + 1) * KV_WIDTH], axis=0, keepdims=True)
        new_col = jnp.sum(jnp.where(eye, new, 0.0), axis=1, keepdims=True)
        sums = jnp.where(lane == 2 * n_pages, new_col, sums)
        mean_t = sums * (1.0 / CMP_BLOCK) + jnp.mean(pe_ref[kind], axis=1, keepdims=True)
        w_ref = wk_ref if kind == 0 else wv_ref
        summ.append(_mm(w_ref[...], mean_t.astype(BF16)).astype(BF16))
    tmpl = tc_ref[...]
    ok = tmpl > 0.5 * NEG
    p = jnp.where(ok, _softmax_rows(_mm(qs, summ[0]) + tmpl), 0.0)
    oc_ref[0] = _nt(p.astype(BF16), summ[1])
    ps_ref[0] = p.reshape(N_KV, HPG, SAMPLE_ROWS, LANE).sum(axis=1).reshape(N_KV * SAMPLE_ROWS, LANE)


def _page_specs(layer, n_pages, kind_block):
    def spec(p):
        return pl.BlockSpec((None, None, 2, N_KV, HEAD_DIM, PAGE),
                            lambda b, pt: (layer, pt[b * n_pages + p], kind_block, 0, 0, 0))
    return [spec(p) for p in range(n_pages)]


def _cmp_sample(layer, pt_flat, n_pages, cache_t, q, kv, pe_t, wk_t, wv_t, t_cmp):
    nseq = q.shape[0]
    rows = N_HEADS * SAMPLE_ROWS
    seq = lambda r, w: pl.BlockSpec((1, r, w), lambda b, pt: (b, 0, 0))
    cst = lambda a: pl.BlockSpec(a.shape, lambda b, pt: (0,) * a.ndim)
    return pl.pallas_call(
        _cmp_sample_body,
        grid_spec=pltpu.PrefetchScalarGridSpec(
            num_scalar_prefetch=1, grid=(nseq,),
            in_specs=[seq(SAMPLE_ROWS, QPAD), seq(SAMPLE_ROWS, 6 * KV_WIDTH), cst(pe_t), cst(wk_t), cst(wv_t),
                      cst(t_cmp)] + _page_specs(layer, n_pages, 0),
            out_specs=[seq(rows, KV_WIDTH), seq(N_KV * SAMPLE_ROWS, LANE)]),
        out_shape=[jax.ShapeDtypeStruct((nseq, rows, KV_WIDTH), F32),
                   jax.ShapeDtypeStruct((nseq, N_KV * SAMPLE_ROWS, LANE), F32)],
        compiler_params=_params(("arbitrary",)),
        name="cmp_sample",
    )(pt_flat, q, kv, pe_t, wk_t, wv_t, t_cmp, *([cache_t] * n_pages))


def _select_sample_body(ps_ref, o_ref, *, past_len):
    cols = ps_ref.shape[1]
    t = lax.broadcasted_iota(jnp.int32, (1, cols), 1) % SAMPLE_ROWS
    o_ref[...] = _select_blocks(ps_ref[...], (past_len + t) // CMP_BLOCK)


def _select_sample(ps_t, past_len):
    cols = ps_t.shape[1]
    tile = min(cols, 2 * LANE)
    spec = pl.BlockSpec((LANE, tile), lambda i: (0, i))
    return pl.pallas_call(
        functools.partial(_select_sample_body, past_len=past_len),
        grid=(cols // tile,), in_specs=[spec], out_specs=spec,
        out_shape=jax.ShapeDtypeStruct(ps_t.shape, F32),
        compiler_params=_params(("arbitrary",)),
        name="select_sample",
    )(ps_t)


def _attn_sample_body(pt_ref, q_ref, kv_ref, un_ref, oc_ref, gl_ref, bg_ref, e_ref, ts_ref, tn_ref, tw_ref,
                      win_ref, *rest):
    pages, o_ref = rest[:-1], rest[-1]
    qs = _stack_heads(q_ref)
    lane = lax.broadcasted_iota(jnp.int32, (SAMPLE_ROWS, LANE), 1)
    tn = tn_ref[...]

    def new_rows(col):
        return _pad_rows(kv_ref[0, :, col * KV_WIDTH:(col + 1) * KV_WIDTH], LANE).astype(BF16)

    def branch(k_past, v_past, bias, k_col, v_col):
        s = _mm(qs, k_past) + bias
        s_new = _nt(qs, new_rows(k_col)) + tn
        m = jnp.maximum(jnp.max(s, axis=-1, keepdims=True), jnp.max(s_new, axis=-1, keepdims=True))
        p, p_new = jnp.exp(s - m), jnp.exp(s_new - m)
        den = jnp.sum(p, axis=-1, keepdims=True) + jnp.sum(p_new, axis=-1, keepdims=True)
        return (_nt(p.astype(BF16), v_past) + _mm(p_new.astype(BF16), new_rows(v_col))) / den

    cat = lambda kind: jnp.concatenate([pg[kind].reshape(KV_WIDTH, PAGE) for pg in pages], axis=1).astype(BF16)
    madd = _mm(un_ref[0].astype(BF16), e_ref[...])
    madd = jnp.concatenate([madd[g * SAMPLE_ROWS:(g + 1) * SAMPLE_ROWS] for g in range(N_KV) for _ in range(HPG)],
                           axis=0)
    o_sel = branch(cat(0), cat(1), ts_ref[...] + madd, 2, 3)
    win = lambda kind: win_ref[kind].reshape(KV_WIDTH, win_ref.shape[-1]).astype(BF16)
    o_win = branch(win(0), win(1), tw_ref[...], 4, 5)

    gate = jax.nn.sigmoid(gl_ref[0] + bg_ref[...])
    oc = oc_ref[0]

    def merged(h):
        rows = slice(h * SAMPLE_ROWS, (h + 1) * SAMPLE_ROWS)
        col = lambda k: gate[:, k * N_HEADS + h:k * N_HEADS + h + 1]
        return col(0) * oc[rows] + col(1) * o_sel[rows] + col(2) * o_win[rows]

    for c in range(HPG):
        o_ref[0, :, c * LANE:(c + 1) * LANE] = jnp.where(lane < HEAD_DIM, merged(c), merged(HPG + c))


def _attn_sample(layer, pt_flat, n_pages, cache_t, win_t, q, kv, unsel, oc, gl, bg, emask, t_sel, t_new, t_win):
    nseq = q.shape[0]
    seq = lambda a: pl.BlockSpec((1,) + a.shape[1:], lambda b, pt: (b,) + (0,) * (a.ndim - 1))
    cst = lambda a: pl.BlockSpec(a.shape, lambda b, pt: (0,) * a.ndim)
    win_spec = pl.BlockSpec((None, None, 2, N_KV, HEAD_DIM, win_t.shape[-1]),
                            lambda b, pt: (layer, b, 0, 0, 0, 0))
    return pl.pallas_call(
        _attn_sample_body,
        grid_spec=pltpu.PrefetchScalarGridSpec(
            num_scalar_prefetch=1, grid=(nseq,),
            in_specs=[seq(q), seq(kv), seq(unsel), seq(oc), seq(gl), cst(bg), cst(emask), cst(t_sel), cst(t_new),
                      cst(t_win), win_spec] + _page_specs(layer, n_pages, 1),
            out_specs=pl.BlockSpec((1, SAMPLE_ROWS, NSA_WIDTH), lambda b, pt: (b, 0, 0))),
        out_shape=jax.ShapeDtypeStruct((nseq, SAMPLE_ROWS, NSA_WIDTH), F32),
        compiler_params=_params(("arbitrary",)),
        name="attn_sample",
    )(pt_flat, q, kv, unsel, oc, gl, bg, emask, t_sel, t_new, t_win, win_t, *([cache_t] * n_pages))


def _proj_c_body(x_ref, g_ref, w_ref, xb_ref, z_ref):
    y = _mm(_rms(x_ref[...], g_ref[...]).astype(BF16), w_ref[...])
    xb_ref[...] = y[:, :LRU_WIDTH]
    z_ref[...] = y[:, LRU_WIDTH:]


def _proj_c(x, norm_g, w, tm):
    n = x.shape[0]
    row = pl.BlockSpec((tm, LRU_WIDTH), lambda i: (i, 0))
    return pl.pallas_call(
        _proj_c_body,
        grid=(n // tm,),
        in_specs=[row, _const_spec((1, D_MODEL)), _const_spec((D_MODEL, 2 * LRU_WIDTH))],
        out_specs=[row, row],
        out_shape=[jax.ShapeDtypeStruct((n, LRU_WIDTH), F32)] * 2,
        compiler_params=_params(("arbitrary",)),
        name="proj_c",
    )(x, norm_g, w)


def _lru_coeffs(xc, wra_ref, bra_ref, wrx_ref, brx_ref, lam_ref):
    xb = xc.astype(BF16)
    blocks = lambda w_ref: jnp.concatenate(
        [_mm(xb[:, k * LRU_BLOCK:(k + 1) * LRU_BLOCK], w_ref[k]) for k in range(LRU_HEADS)], axis=-1)
    r = jax.nn.sigmoid(blocks(wra_ref) + bra_ref[...])
    ig = jax.nn.sigmoid(blocks(wrx_ref) + brx_ref[...])
    log_a = -LRU_C * r * jax.nn.softplus(-lam_ref[...])
    return jnp.exp(log_a), jnp.sqrt(-_expm1(2.0 * log_a)) * ig * xc


def _expm1(x):
    u = jnp.exp(x)
    small = x > -1.0
    us = jnp.where(small & (u != 1.0), u, 0.5)
    return jnp.where(small, jnp.where(u == 1.0, x, (us - 1.0) * x / jnp.log(us)), u - 1.0)


def _mix_c_prompt_body(xb_ref, z_ref, x_ref, cw_ref, cb_ref, wra_ref, bra_ref, wrx_ref, brx_ref, lam_ref, wo_ref,
                       fg_ref, y_ref, hl_ref, tail_ref, xin_scr, a_scr, b_scr, h_scr, *, final_norm):
    tt = xb_ref.shape[1]
    pad = 8
    hist = CONV_WIDTH - 1

    @pl.when(pl.program_id(1) == 0)
    def _():
        xin_scr[0:pad] = jnp.zeros((pad, LRU_WIDTH), F32)
        h_scr[...] = jnp.zeros(h_scr.shape, F32)

    xin_scr[pad:pad + tt] = xb_ref[0]
    xc = cb_ref[...] + xin_scr[pad - hist:pad - hist + tt] * cw_ref[0:1]
    for k in range(1, CONV_WIDTH):
        xc = xc + xin_scr[pad - hist + k:pad - hist + k + tt] * cw_ref[k:k + 1]
    a, b = _lru_coeffs(xc, wra_ref, bra_ref, wrx_ref, brx_ref, lam_ref)
    a_scr[...] = a
    b_scr[...] = b

    row = lax.broadcasted_iota(jnp.int32, (8, LRU_WIDTH), 0)

    def step(k, h_prev):
        r0 = pl.multiple_of(k * 8, 8)
        ca, cbv = a_scr[pl.ds(r0, 8)], b_scr[pl.ds(r0, 8)]
        for s in (1, 2, 4):
            keep = row >= s
            cbv = cbv + ca * jnp.where(keep, pltpu.roll(cbv, s, axis=0), 0.0)
            ca = ca * jnp.where(keep, pltpu.roll(ca, s, axis=0), 1.0)
        hs = ca * h_prev + cbv
        b_scr[pl.ds(r0, 8)] = hs
        return jnp.broadcast_to(hs[7:8], (8, LRU_WIDTH))

    h_scr[...] = lax.fori_loop(0, tt // 8, step, h_scr[...])
    y = (b_scr[...] * _silu(z_ref[0])).astype(BF16)
    out = x_ref[0] + _mm(y, wo_ref[...])
    y_ref[0] = _rms(out, fg_ref[...]) if final_norm else out
    hl_ref[0] = h_scr[0:1]
    tail_ref[0] = xin_scr[pad + tt - hist:pad + tt]
    xin_scr[0:pad] = xin_scr[tt:tt + pad]


def _mix_c_prompt(xb, z, x, cw, cb, wra, bra, wrx, brx, lam, wo, fg, final_norm, tt=256):
    nb, nt, _ = x.shape
    row = pl.BlockSpec((1, tt, LRU_WIDTH), lambda b, i: (b, i, 0))
    cst = lambda a: _const_spec(a.shape)
    hist = CONV_WIDTH - 1
    return pl.pallas_call(
        functools.partial(_mix_c_prompt_body, final_norm=final_norm),
        grid=(nb, nt // tt),
        in_specs=[row, row, row, cst(cw), cst(cb), cst(wra), cst(bra), cst(wrx), cst(brx), cst(lam), cst(wo),
                  cst(fg)],
        out_specs=[row, pl.BlockSpec((1, 1, LRU_WIDTH), lambda b, i: (b, 0, 0)),
                   pl.BlockSpec((1, hist, LRU_WIDTH), lambda b, i: (b, 0, 0))],
        out_shape=[jax.ShapeDtypeStruct(x.shape, F32), jax.ShapeDtypeStruct((nb, 1, LRU_WIDTH), F32),
                   jax.ShapeDtypeStruct((nb, hist, LRU_WIDTH), F32)],
        scratch_shapes=[pltpu.VMEM((tt + 8, LRU_WIDTH), F32), pltpu.VMEM((tt, LRU_WIDTH), F32),
                        pltpu.VMEM((tt, LRU_WIDTH), F32), pltpu.VMEM((8, LRU_WIDTH), F32)],
        compiler_params=_params(("arbitrary", "arbitrary")),
        name="mix_c_prompt",
    )(xb, z, x, cw, cb, wra, bra, wrx, brx, lam, wo, fg)


def _mix_c_sample_body(xb_ref, z_ref, x_ref, h0_ref, c0_ref, cw_ref, cb_ref, wra_ref, bra_ref, wrx_ref, brx_ref,
                       lam_ref, wo_ref, fg_ref, y_ref, hl_ref, tail_ref, *, final_norm):
    nt = xb_ref.shape[0]
    hist = CONV_WIDTH - 1
    xin = [c0_ref[k] for k in range(hist)] + [xb_ref[t] for t in range(nt)]
    h = h0_ref[...]
    for t in range(nt):
        xc = cb_ref[...] + xin[t] * cw_ref[0:1]
        for k in range(1, CONV_WIDTH):
            xc = xc + xin[t + k] * cw_ref[k:k + 1]
        a, b = _lru_coeffs(xc, wra_ref, bra_ref, wrx_ref, brx_ref, lam_ref)
        h = a * h + b
        out = x_ref[t] + _mm((h * _silu(z_ref[t])).astype(BF16), wo_ref[...])
        y_ref[t] = _rms(out, fg_ref[...]) if final_norm else out
    hl_ref[...] = h
    for k in range(hist):
        tail_ref[k] = xin[nt + k]


def _mix_c_sample(xb, z, x, h0, c0, cw, cb, wra, bra, wrx, brx, lam, wo, fg, final_norm):
    args = (xb, z, x, h0, c0, cw, cb, wra, bra, wrx, brx, lam, wo, fg)
    return pl.pallas_call(
        functools.partial(_mix_c_sample_body, final_norm=final_norm),
        grid=(1,),
        in_specs=[_const_spec(a.shape) for a in args],
        out_specs=[_const_spec(x.shape), _const_spec(h0.shape), _const_spec(c0.shape)],
        out_shape=[jax.ShapeDtypeStruct(x.shape, F32), jax.ShapeDtypeStruct(h0.shape, F32),
                   jax.ShapeDtypeStruct(c0.shape, F32)],
        compiler_params=_params(("arbitrary",)),
        name="mix_c_sample",
    )(*args)


def _pair_perm():
    cols = []
    for c in range(HPG):
        cols += list(range(c * HEAD_DIM, (c + 1) * HEAD_DIM))
        cols += list(range((HPG + c) * HEAD_DIM, (HPG + c + 1) * HEAD_DIM))
    return np.asarray(cols)


def _layout_w_in_a(w):
    o_kv = NSA_WIDTH
    o_g = o_kv + 6 * KV_WIDTH
    o_z = o_g + 3 * N_HEADS
    o_rest = o_z + NSA_WIDTH
    wq = w[:, :NSA_WIDTH].reshape(D_MODEL, N_HEADS, HEAD_DIM)
    zeros = jnp.zeros_like(wq)
    first = (np.arange(N_HEADS) < HPG)[None, :, None]
    wq = jnp.concatenate([jnp.where(first, wq, zeros), jnp.where(first, zeros, wq)], axis=-1)
    wg = jnp.pad(w[:, o_g:o_z], ((0, 0), (0, LANE - 3 * N_HEADS)))
    wz = w[:, o_z:o_rest][:, _pair_perm()]
    return jnp.concatenate([wq.reshape(D_MODEL, QPAD), w[:, o_kv:o_g], wz, w[:, o_rest:], wg], axis=1).astype(BF16)


def _block_diag(w, n):
    return jnp.kron(jnp.eye(n, dtype=w.dtype), w)


def _even_layer(a, yp, ys, cache_t, win_t, pt_flat, n_pages, tmpl, p):
    nb, nt, _ = yp.shape
    nseq, dec_t, _ = ys.shape
    past_len = n_pages * PAGE
    w_in = _layout_w_in_a(p['w_in_a'][a])
    norm_g = p['norm_a'][a][None]
    ln_g, ln_b = p['gmlp_ln_g'][a][None], p['gmlp_ln_b'][a][None]
    bg = jnp.pad(p['b_gate_a'][a], (0, LANE - 3 * N_HEADS))[None]
    w_out = jnp.concatenate([p['w_out_a'][a][:NSA_WIDTH][_pair_perm()], p['w_out_a'][a][NSA_WIDTH:]], axis=0).astype(BF16)
    pe, wc = p['pe_cmp'][a], p['w_cmp'][a]

    q, kvt, kvtb, _, gl, z, u, vn, zg = _proj_a(yp, norm_g, w_in, ln_g, ln_b, tm=256)
    pe_rows = jnp.concatenate([pe[0], pe[0], pe[1], pe[1]], axis=1)
    w_bd = jnp.zeros((2 * KV_WIDTH, 2 * KV_WIDTH), F32)
    for k in range(4):
        w_bd = w_bd.at[k * HEAD_DIM:(k + 1) * HEAD_DIM, k * HEAD_DIM:(k + 1) * HEAD_DIM].set(wc[k // 2])
    cmp = _cmp_prompt(kvt, pe_rows, w_bd.astype(BF16))
    kaug, vaug, kwin, vwaug = _prompt_kv_operands(kvtb, tmpl['e_prompt'])
    o = _attn_prompt(q, tmpl['far_lanes'], cmp, kaug, vaug, kwin, vwaug, gl, bg, tmpl['cmp'], tmpl['near'], tmpl['win'])
    yp_new = _post_a(o, z, u, vn, zg, yp, p['w_spatial'][a], p['b_spatial'][a].T, w_out)
    rows_p = jnp.transpose(kvt[:, :4 * KV_WIDTH].reshape(nb, 4, N_KV, HEAD_DIM, nt), (0, 4, 1, 2, 3))
    wlen = min(WINDOW, nt)
    win_p = jnp.transpose(kvt[:, 4 * KV_WIDTH:, nt - wlen:].reshape(nb, 2, N_KV, HEAD_DIM, wlen), (0, 4, 1, 2, 3))

    xs = ys.reshape(1, nseq * dec_t, D_MODEL)
    q, _, _, kv, gl, z, u, vn, zg = _proj_a(xs, norm_g, w_in, ln_g, ln_b, tm=nseq * dec_t)
    pad_t = lambda x: jnp.pad(x.reshape(nseq, dec_t, -1), ((0, 0), (0, SAMPLE_ROWS - dec_t), (0, 0)))
    q8, kv8, gl8 = pad_t(q), pad_t(kv), pad_t(gl)
    pe_t = jnp.concatenate([jnp.swapaxes(pe, 1, 2)] * N_KV, axis=1)
    wk_t = _block_diag(wc[0].T, N_KV).astype(BF16)
    wv_t = _block_diag(wc[1].T, N_KV).astype(BF16)
    oc, ps = _cmp_sample(a, pt_flat, n_pages, cache_t, q8, kv8, pe_t, wk_t, wv_t, tmpl['s_cmp'])
    unsel_t = _select_sample(ps.reshape(nseq * N_KV * SAMPLE_ROWS, LANE).T, past_len)
    unsel = unsel_t.T.reshape(nseq, N_KV * SAMPLE_ROWS, LANE)
    o8 = _attn_sample(a, pt_flat, n_pages, cache_t, win_t, q8, kv8, unsel, oc, gl8, bg, tmpl['e_sample'],
                      tmpl['s_sel'], tmpl['s_new'], tmpl['s_win'])
    o = o8[:, :dec_t].reshape(1, nseq * dec_t, NSA_WIDTH)
    w_small = p['w_spatial'][a][:, :dec_t, :dec_t]
    above = lambda k: jnp.pad(jnp.diagonal(w_small, offset=-k, axis1=1, axis2=2), ((0, 0), (k, 0))).T
    w_sp = jnp.stack([jnp.tile(above(k), (nseq, 1)) for k in range(dec_t)])
    b_sp = jnp.tile(p['b_spatial'][a][:, :dec_t].T, (nseq, 1))
    ys_new = _post_a(o, z, u, vn, zg, xs, w_sp, b_sp, w_out, short_chunks=True).reshape(nseq, dec_t, D_MODEL)
    rows_s = kv[0, :, :4 * KV_WIDTH].reshape(nseq, dec_t, 4, N_KV, HEAD_DIM)
    win_s = kv[0, :, 4 * KV_WIDTH:].reshape(nseq, dec_t, 2, N_KV, HEAD_DIM)
    return yp_new, ys_new, rows_p, rows_s, win_p, win_s, vn.reshape(nseq, dec_t, GMLP_WIDTH)


def _odd_layer(c, yp, ys, h0, conv0, p, final):
    nb, nt, _ = yp.shape
    nseq, dec_t, _ = ys.shape
    norm_g = p['norm_c'][c][None]
    w_in = p['w_in_c'][c].astype(BF16)
    consts = (p['conv_w'][c], p['conv_b'][c][None], p['w_rg_a'][c].astype(BF16), p['b_rg_a'][c][None],
              p['w_rg_x'][c].astype(BF16), p['b_rg_x'][c][None], p['lru_lambda'][c][None],
              p['w_out_c'][c].astype(BF16), p['final_norm'][None])
    xb, z = _proj_c(yp.reshape(nb * nt, D_MODEL), norm_g, w_in, tm=512)
    yp_new, h_p, tail_p = _mix_c_prompt(xb.reshape(nb, nt, -1), z.reshape(nb, nt, -1), yp, *consts, final_norm=final)
    xs = jnp.swapaxes(ys, 0, 1)
    xb, z = _proj_c(xs.reshape(dec_t * nseq, D_MODEL), norm_g, w_in, tm=dec_t * nseq)
    tm3 = lambda x: x.reshape(dec_t, nseq, -1)
    ys_new, h_s, tail_s = _mix_c_sample(tm3(xb), tm3(z), xs, h0, jnp.swapaxes(conv0, 0, 1), *consts,
                                        final_norm=final)
    return yp_new, jnp.swapaxes(ys_new, 0, 1), h_p[:, 0], h_s, tail_p, jnp.swapaxes(tail_s, 0, 1)


def kernel(x_prompt, x_sample, cache_nsa, cache_win, state_lru_h, state_lru_conv, page_table, norm_a, w_in_a, b_gate_a, pe_cmp, w_cmp, gmlp_ln_g, gmlp_ln_b, w_spatial, b_spatial, w_out_a, norm_c, w_in_c, conv_w, conv_b, w_rg_a, b_rg_a, w_rg_x, b_rg_x, lru_lambda, w_out_c, rel_bias, final_norm):
    p = dict(norm_a=norm_a, w_in_a=w_in_a, b_gate_a=b_gate_a, pe_cmp=pe_cmp, w_cmp=w_cmp, gmlp_ln_g=gmlp_ln_g,
             gmlp_ln_b=gmlp_ln_b, w_spatial=w_spatial, b_spatial=b_spatial, w_out_a=w_out_a, norm_c=norm_c,
             w_in_c=w_in_c, conv_w=conv_w, conv_b=conv_b, w_rg_a=w_rg_a, b_rg_a=b_rg_a, w_rg_x=w_rg_x,
             b_rg_x=b_rg_x, lru_lambda=lru_lambda, w_out_c=w_out_c, rel_bias=rel_bias, final_norm=final_norm)
    nb, nt, _ = x_prompt.shape
    nseq, n_pages = page_table.shape
    past_len = n_pages * PAGE
    wlen = cache_win.shape[2]
    depth = norm_a.shape[0] + norm_c.shape[0]
    assert nt % KV_TILE == 0 and nt >= WIN_KEYS and nt // CMP_BLOCK <= LANE
    assert past_len // CMP_BLOCK < LANE and x_sample.shape[1] <= SAMPLE_ROWS and n_pages <= N_PAGES_MAX

    cache_t = jnp.transpose(cache_nsa, (0, 1, 3, 4, 5, 2))
    win_t = jnp.transpose(cache_win, (0, 1, 3, 4, 5, 2))
    pt_flat = page_table.reshape(-1).astype(jnp.int32)

    blk_of_key = lambda n: (np.arange(n) // CMP_BLOCK)[None, :] == np.arange(LANE)[:, None]
    rows_s = SAMPLE_ROWS
    stack = lambda t: t.reshape(N_HEADS * rows_s, t.shape[-1])
    by_group = lambda t: t.reshape(N_KV, GROUP_ROWS, t.shape[-1])
    assert NEAR_SPAN + LANE - (KV_TILE - 1) >= _UPPER[-1] and nt % KV_TILE == 0
    tmpl = dict(
        far_lanes=_far_bias_lanes(rel_bias),
        e_prompt=jnp.asarray(np.where(blk_of_key(nt), NEG, 0.0), BF16),
        e_sample=jnp.asarray(np.where(blk_of_key(past_len), NEG, 0.0), BF16),
        cmp=_bias_template(rel_bias, Q_BLOCK, LANE, -CMP_BLOCK, CMP_BLOCK * (LANE - 2) - (CMP_BLOCK - 1)),
        near=by_group(_bias_template(rel_bias, Q_BLOCK, NEAR_SPAN + KV_TILE, -1, NEAR_SPAN)),
        win=by_group(_bias_template(rel_bias, Q_BLOCK, WINDOW + WIN_KEYS, -1, WINDOW, hi=WINDOW)),
        s_cmp=stack(_bias_template(rel_bias, rows_s, LANE, -CMP_BLOCK, past_len - (CMP_BLOCK - 1))),
        s_sel=stack(_bias_template(rel_bias, rows_s, past_len, -1, past_len)),
        s_new=stack(_bias_template(rel_bias, rows_s, LANE, -1, 0, cmax=x_sample.shape[1])),
        s_win=stack(_bias_template(rel_bias, rows_s, wlen, -1, wlen, hi=WINDOW)),
    )

    yp, ys = x_prompt, x_sample
    outs = [[] for _ in range(9)]
    for layer in range(depth):
        if layer % 2 == 0:
            yp, ys, *leaves = _even_layer(layer // 2, yp, ys, cache_t, win_t, pt_flat, n_pages, tmpl, p)
            for dst, leaf in zip(outs[:5], leaves):
                dst.append(leaf)
        else:
            c = layer // 2
            yp, ys, *leaves = _odd_layer(c, yp, ys, state_lru_h[c], state_lru_conv[c], p, final=layer == depth - 1)
            for dst, leaf in zip(outs[5:], leaves):
                dst.append(leaf)
    return (yp, ys) + tuple(jnp.stack(o) for o in outs)
```

```python
import functools
import math

import numpy as np
import jax
import jax.numpy as jnp
from jax import lax
from jax.experimental import pallas as pl
from jax.experimental.pallas import tpu as pltpu

F32 = jnp.float32
BF16 = jnp.bfloat16

D_MODEL = 1024
N_HEADS = 8
N_KV = 2
HPG = N_HEADS // N_KV
HEAD_DIM = 64
NSA_WIDTH = N_HEADS * HEAD_DIM
KV_WIDTH = N_KV * HEAD_DIM
CMP_BLOCK = 64
N_SEL = 16
WINDOW = 512
GMLP_GROUPS = 4
GMLP_WIDTH = 512
GMLP_CHUNK = 128
LRU_WIDTH = 1024
LRU_HEADS = 8
LRU_BLOCK = 128
CONV_WIDTH = 4
LRU_C = 8.0
REL_BUCKETS = 32
REL_MAX_DIST = 1024
EPS = 1e-6
NEG = -1e30
PAGE = 128

LANE = 128
Q_BLOCK = 128
KV_TILE = 512
NEAR_SPAN = 1280
WIN_KEYS = WINDOW + Q_BLOCK
VMEM_LIMIT = 56 * 1024 * 1024

QPAD = N_HEADS * LANE
C_KV = QPAD
C_Z = C_KV + 6 * KV_WIDTH
C_U = C_Z + NSA_WIDTH
C_V = C_U + GMLP_WIDTH
C_ZG = C_V + GMLP_WIDTH
C_G = C_ZG + GMLP_WIDTH
PROJ_COLS = C_G + LANE


def _nt(a, b):
    return lax.dot_general(a, b, (((1,), (1,)), ((), ())), preferred_element_type=F32)


def _mm(a, b):
    return jnp.dot(a, b, preferred_element_type=F32)


def _params(sem):
    return pltpu.CompilerParams(dimension_semantics=sem, vmem_limit_bytes=VMEM_LIMIT)


def _const_spec(shape):
    n = len(shape)
    return pl.BlockSpec(shape, lambda *_: (0,) * n)


def _bucket_upper_bounds():
    max_exact = REL_BUCKETS // 2
    d = np.arange(0, 4 * REL_MAX_DIST, dtype=np.int64)

    def buckets(ft):
        df = np.maximum(d, 1).astype(ft)
        large = max_exact + (np.log(df / ft(max_exact)) / ft(math.log(REL_MAX_DIST / max_exact))
                             * ft(REL_BUCKETS - max_exact)).astype(np.int32)
        return np.where(d < max_exact, d, np.minimum(large, REL_BUCKETS - 1))

    b32, b64 = buckets(np.float32), buckets(np.float64)
    assert (b32 == b64).all() and (np.diff(b32) >= 0).all() and b32[-1] == REL_BUCKETS - 1
    return [int(np.argmax(b32 > k)) for k in range(REL_BUCKETS - 1)]


_UPPER = _bucket_upper_bounds()


def _bias_tmpl_body(tab_ref, o_ref, *, rows, cs, off, hi, cmax):
    h = pl.program_id(0)
    t = lax.broadcasted_iota(jnp.int32, (rows, LANE), 0)
    lane = lax.broadcasted_iota(jnp.int32, (rows, LANE), 1)

    def chunk(k, carry):
        c0 = pl.multiple_of(k * LANE, LANE)
        c = lane + c0
        d = t + cs * c + off
        dd = jnp.maximum(d, 0)
        val = jnp.full((rows, LANE), tab_ref[REL_BUCKETS - 1, h], F32)
        for b in range(REL_BUCKETS - 2, -1, -1):
            val = jnp.where(dd < _UPPER[b], tab_ref[b, h], val)
        ok = (d >= 0) & (d <= hi) & (c < cmax)
        o_ref[0, :, pl.ds(c0, LANE)] = jnp.where(ok, val, NEG)
        return carry

    lax.fori_loop(0, o_ref.shape[2] // LANE, chunk, 0)


def _bias_template(rel_bias, rows, width, cs, off, hi=1 << 30, cmax=1 << 30):
    return pl.pallas_call(
        functools.partial(_bias_tmpl_body, rows=rows, cs=cs, off=off, hi=hi, cmax=cmax),
        grid=(N_HEADS,),
        in_specs=[pl.BlockSpec(memory_space=pltpu.SMEM)],
        out_specs=pl.BlockSpec((1, rows, width), lambda h: (h, 0, 0)),
        out_shape=jax.ShapeDtypeStruct((N_HEADS, rows, width), F32),
        compiler_params=_params(("arbitrary",)),
        name="bias_template",
    )(rel_bias)


def _rms(x, g):
    return x * lax.rsqrt(jnp.mean(x * x, axis=-1, keepdims=True) + EPS) * g


def _proj_a_body(x_ref, g_ref, w_ref, lng_ref, lnb_ref, q_ref, *rest, feature_major):
    xn = _rms(x_ref[0], g_ref[...])
    y = _mm(xn.astype(BF16), w_ref[...])
    q_ref[0] = (y[:, :QPAD] * (HEAD_DIM ** -0.5)).astype(q_ref.dtype)
    kv = y[:, C_KV:C_Z]
    if feature_major:
        kvt_ref, kvtb_ref, gl_ref, z_ref, u_ref, vn_ref, zg_ref = rest
        kvt = kv.T
        kvt_ref[0] = kvt
        kvtb_ref[0] = kvt.astype(BF16)
    else:
        kv_ref, gl_ref, z_ref, u_ref, vn_ref, zg_ref = rest
        kv_ref[0] = kv
    z_ref[0] = y[:, C_Z:C_U]
    u_ref[0] = y[:, C_U:C_V]
    v = y[:, C_V:C_ZG]
    mu = jnp.mean(v, axis=-1, keepdims=True)
    var = jnp.mean(jnp.square(v - mu), axis=-1, keepdims=True)
    vn_ref[0] = (v - mu) * lax.rsqrt(var + EPS) * lng_ref[...] + lnb_ref[...]
    zg_ref[0] = y[:, C_ZG:C_G]
    gl_ref[0] = y[:, C_G:]


def _proj_a(x, norm_g, w, ln_g, ln_b, tm, feature_major):
    nb, nt, _ = x.shape
    row = lambda width: pl.BlockSpec((1, tm, width), lambda b, i: (b, i, 0))
    colT = pl.BlockSpec((1, 6 * KV_WIDTH, tm), lambda b, i: (b, 0, i))
    sds = lambda shape, dt=F32: jax.ShapeDtypeStruct(shape, dt)
    if feature_major:
        kv_specs = [colT, colT]
        kv_shapes = [sds((nb, 6 * KV_WIDTH, nt)), sds((nb, 6 * KV_WIDTH, nt), BF16)]
    else:
        kv_specs, kv_shapes = [row(6 * KV_WIDTH)], [sds((nb, nt, 6 * KV_WIDTH))]
    return pl.pallas_call(
        functools.partial(_proj_a_body, feature_major=feature_major),
        grid=(nb, nt // tm),
        in_specs=[row(D_MODEL), _const_spec((1, D_MODEL)), _const_spec((D_MODEL, PROJ_COLS)),
                  _const_spec((1, GMLP_WIDTH)), _const_spec((1, GMLP_WIDTH))],
        out_specs=[row(QPAD)] + kv_specs + [row(LANE), row(NSA_WIDTH), row(GMLP_WIDTH), row(GMLP_WIDTH),
                                            row(GMLP_WIDTH)],
        out_shape=[sds((nb, nt, QPAD), BF16 if feature_major else F32)] + kv_shapes
                  + [sds((nb, nt, LANE)), sds((nb, nt, NSA_WIDTH)), sds((nb, nt, GMLP_WIDTH)),
                     sds((nb, nt, GMLP_WIDTH)), sds((nb, nt, GMLP_WIDTH))],
        compiler_params=_params(("arbitrary", "arbitrary")),
        name="proj_a",
    )(x, norm_g, w, ln_g, ln_b)


def _split_bf16(x):
    hi = x.astype(BF16)
    return hi, (x - hi.astype(F32)).astype(BF16)


def _cmp_prompt_body(kvt_ref, pe_ref, w_ref, o_ref):
    nt = kvt_ref.shape[2]
    blk = lax.broadcasted_iota(jnp.int32, (LANE, nt), 0)
    pos = lax.broadcasted_iota(jnp.int32, (LANE, nt), 1)
    pool = jnp.where(pos // CMP_BLOCK == blk, 1.0 / CMP_BLOCK, 0.0).astype(BF16)
    hi, lo = _split_bf16(kvt_ref[0])
    mean = _nt(pool, hi) + _nt(pool, lo)
    mean = mean + jnp.mean(pe_ref[...], axis=0, keepdims=True)
    o_ref[0] = _mm(mean.astype(BF16), w_ref[...])


def _cmp_prompt(kvt, pe_rows, w_bd):
    nb, _, nt = kvt.shape
    return pl.pallas_call(
        _cmp_prompt_body,
        grid=(nb,),
        in_specs=[pl.BlockSpec((1, 2 * KV_WIDTH, nt), lambda b: (b, 0, 0)),
                  _const_spec((CMP_BLOCK, 2 * KV_WIDTH)), _const_spec((2 * KV_WIDTH, 2 * KV_WIDTH))],
        out_specs=pl.BlockSpec((1, LANE, 2 * KV_WIDTH), lambda b: (b, 0, 0)),
        out_shape=jax.ShapeDtypeStruct((nb, LANE, 2 * KV_WIDTH), F32),
        compiler_params=_params(("arbitrary",)),
        name="cmp_prompt",
    )(kvt, pe_rows, w_bd)


def _select_blocks(score_t, cur):
    shape = score_t.shape
    blk = lax.broadcasted_iota(jnp.int32, shape, 0)
    forced = (blk == 0) | (blk == cur) | (blk == cur - 1)
    score = jnp.where(forced | (blk >= cur), -jnp.inf, score_t)

    def pick(_, carry):
        score, unsel = carry
        mx = jnp.max(score, axis=0, keepdims=True)
        idx = jnp.min(jnp.where(score == mx, blk, shape[0]), axis=0, keepdims=True)
        hit = blk == idx
        unsel = jnp.where(hit & (mx > -jnp.inf), 0.0, unsel)
        return jnp.where(hit, -jnp.inf, score), unsel

    _, unsel = lax.fori_loop(0, N_SEL - 3, pick, (score, jnp.where(forced, 0.0, 1.0)))
    return unsel


def _softmax_rows(s):
    m = jnp.max(s, axis=-1, keepdims=True)
    e = jnp.exp(s - m)
    return e / jnp.sum(e, axis=-1, keepdims=True)


GROUP_ROWS = HPG * Q_BLOCK


def _rowmax(s):
    m = s[:, :LANE]
    for k in range(1, s.shape[1] // LANE):
        m = jnp.maximum(m, s[:, k * LANE:(k + 1) * LANE])
    return jnp.broadcast_to(jnp.max(m, axis=-1, keepdims=True), (s.shape[0], LANE))


def _tile_lanes(x, n):
    return jnp.concatenate([x] * n, axis=1)


def _attn_prompt_body(q_ref, bl_ref, cmp_ref, ka_ref, va_ref, kw_ref, vw_ref, gl_ref, bg_ref,
                      tc_ref, tn_ref, tw_ref, o_ref, lhs_scr, m_scr, acc_scr):
    i = pl.program_id(1)
    qf = q_ref[0]
    lane = lax.broadcasted_iota(jnp.int32, (Q_BLOCK, LANE), 1)
    stack = lambda x, g: jnp.concatenate([x[:, h * LANE:(h + 1) * LANE] for h in range(g * HPG, (g + 1) * HPG)], axis=0)
    qs = [stack(qf, g) for g in range(N_KV)]
    qs_far = [stack(qf + bl_ref[...], g) for g in range(N_KV)]

    kc = cmp_ref[0, :, :KV_WIDTH].astype(BF16)
    vc = cmp_ref[0, :, KV_WIDTH:].astype(BF16)
    seen = lane <= 2 * i + 1
    shift = (2 * i + 2) % LANE
    o_cmp, p_sum = [], []
    for g in range(N_KV):
        tmpl = jnp.concatenate([jnp.where(seen, pltpu.roll(tc_ref[h], shift, axis=1), NEG)
                                for h in range(g * HPG, (g + 1) * HPG)], axis=0)
        ok = tmpl > 0.5 * NEG
        p = jnp.where(ok, _softmax_rows(_nt(qs[g], kc) + tmpl), 0.0)
        o_cmp.append(_mm(p.astype(BF16), vc))
        p_sum.append(p.reshape(HPG, Q_BLOCK, LANE).sum(axis=0))

    tpos = i * Q_BLOCK + lax.broadcasted_iota(jnp.int32, (1, N_KV * Q_BLOCK), 1) % Q_BLOCK
    unsel_t = _select_blocks(jnp.concatenate([ps.T for ps in p_sum], axis=1), tpos // CMP_BLOCK)
    for g in range(N_KV):
        un = _tile_rows(unsel_t[:, g * Q_BLOCK:(g + 1) * Q_BLOCK].T.astype(BF16), HPG)
        lhs_scr[0, g] = jnp.concatenate([qs_far[g], un], axis=1)
        lhs_scr[1, g] = jnp.concatenate([qs[g], un], axis=1)

    m_scr[...] = jnp.full(m_scr.shape, NEG, F32)
    acc_scr[...] = jnp.zeros(acc_scr.shape, F32)

    def tile(j, near):
        k0 = pl.multiple_of(j * KV_TILE, KV_TILE)
        s = [_mm(lhs_scr[1 if near else 0, g], ka_ref[0, g, :, pl.ds(k0, KV_TILE)]) for g in range(N_KV)]
        if near:
            off = pl.multiple_of(NEAR_SPAN - (i * Q_BLOCK - j * KV_TILE), LANE)
            s = [s[g] + tn_ref[g, :, pl.ds(off, KV_TILE)] for g in range(N_KV)]
        m_old = [m_scr[g] for g in range(N_KV)]
        m_new = [jnp.maximum(m_old[g], _rowmax(s[g])) for g in range(N_KV)]
        p = [jnp.exp(s[g] - _tile_lanes(m_new[g], KV_TILE // LANE)).astype(BF16) for g in range(N_KV)]
        for g in range(N_KV):
            acc_scr[g] = jnp.exp(m_old[g] - m_new[g]) * acc_scr[g] + _nt(p[g], va_ref[0, g, :, pl.ds(k0, KV_TILE)])
            m_scr[g] = m_new[g]

    n_far = jnp.maximum(i * Q_BLOCK - (NEAR_SPAN + LANE) + KV_TILE, 0) // KV_TILE
    n_tiles = i // (KV_TILE // Q_BLOCK) + 1
    def far_pair(jj, c):
        tile(2 * jj, False)
        tile(2 * jj + 1, False)
        return c

    lax.fori_loop(0, n_far // 2, far_pair, 0)

    @pl.when(n_far % 2 == 1)
    def _():
        tile(n_far - 1, False)

    lax.fori_loop(n_far, n_tiles, lambda j, c: (tile(j, True), c)[1], 0)

    w0 = pl.multiple_of(jnp.maximum(i * Q_BLOCK - WINDOW, 0), LANE)
    woff = pl.multiple_of(WINDOW - (i * Q_BLOCK - w0), LANE)
    kwt = kw_ref[0, :, pl.ds(w0, WIN_KEYS)]
    o_sel, o_win = [], []
    for g in range(N_KV):
        s = _mm(qs[g], kwt) + tw_ref[g, :, pl.ds(woff, WIN_KEYS)]
        e = jnp.exp(s - _tile_lanes(_rowmax(s), WIN_KEYS // LANE)).astype(BF16)
        pv = _nt(e, vw_ref[0, g, :, pl.ds(w0, WIN_KEYS)])
        o_win.append(pv / pltpu.roll(pv, HEAD_DIM, axis=1))
        acc = acc_scr[g]
        o_sel.append(acc / pltpu.roll(acc, HEAD_DIM, axis=1))

    gate = jax.nn.sigmoid(gl_ref[0] + bg_ref[...])

    def merged(h):
        g, rows = h // HPG, slice((h % HPG) * Q_BLOCK, (h % HPG + 1) * Q_BLOCK)
        col = lambda k: gate[:, k * N_HEADS + h:k * N_HEADS + h + 1]
        return col(0) * o_cmp[g][rows] + col(1) * o_sel[g][rows] + col(2) * o_win[g][rows]

    for c in range(HPG):
        o_ref[0, :, c * LANE:(c + 1) * LANE] = jnp.where(lane < HEAD_DIM, merged(c), merged(HPG + c))


def _tile_rows(x, n):
    return jnp.concatenate([x] * n, axis=0)


def _attn_prompt(q, bias_lanes, cmp, kaug, vaug, kwin, vwaug, gl, bg, t_cmp, t_near, t_win):
    nb, nt, _ = q.shape
    once = pl.Buffered(1)
    per_b = lambda a: pl.BlockSpec((1,) + a.shape[1:], lambda b, i: (b,) + (0,) * (a.ndim - 1), pipeline_mode=once)
    whole = lambda a: pl.BlockSpec(a.shape, lambda b, i: (0,) * a.ndim, pipeline_mode=once)
    return pl.pallas_call(
        _attn_prompt_body,
        grid=(nb, nt // Q_BLOCK),
        in_specs=[pl.BlockSpec((1, Q_BLOCK, QPAD), lambda b, i: (b, i, 0)),
                  whole(bias_lanes), per_b(cmp), per_b(kaug), per_b(vaug), per_b(kwin), per_b(vwaug),
                  pl.BlockSpec((1, Q_BLOCK, LANE), lambda b, i: (b, i, 0)),
                  _const_spec((1, LANE)), whole(t_cmp), whole(t_near), whole(t_win)],
        out_specs=pl.BlockSpec((1, Q_BLOCK, NSA_WIDTH), lambda b, i: (b, i, 0)),
        out_shape=jax.ShapeDtypeStruct((nb, nt, NSA_WIDTH), F32),
        scratch_shapes=[pltpu.VMEM((2, N_KV, GROUP_ROWS, 2 * LANE), BF16),
                        pltpu.VMEM((N_KV, GROUP_ROWS, LANE), F32), pltpu.VMEM((N_KV, GROUP_ROWS, LANE), F32)],
        compiler_params=_params(("arbitrary", "arbitrary")),
        name="attn_prompt",
    )(q, bias_lanes, cmp, kaug, vaug, kwin, vwaug, gl, bg, t_cmp, t_near, t_win)


def _prompt_kv_operands(kvtb, emask):
    nb, _, nt = kvtb.shape
    half = lambda kind, g: kvtb[:, kind * KV_WIDTH + g * HEAD_DIM:kind * KV_WIDTH + (g + 1) * HEAD_DIM]
    ones = jnp.ones((nb, HEAD_DIM, nt), BF16)
    extra = ones.at[:, 2:].set(0)
    e = jnp.broadcast_to(emask[None], (nb, LANE, nt))
    kaug = jnp.stack([jnp.concatenate([half(2, 0), extra, e], axis=1),
                      jnp.concatenate([extra, half(2, 1), e], axis=1)], axis=1)
    aug = lambda kind: jnp.stack([jnp.concatenate([half(kind, 0), ones], axis=1),
                                  jnp.concatenate([ones, half(kind, 1)], axis=1)], axis=1)
    return kaug, aug(3), kvtb[:, 4 * KV_WIDTH:5 * KV_WIDTH], aug(5)


def _far_bias_lanes(rel_bias):
    b = rel_bias[REL_BUCKETS - 1]
    hi = b.astype(BF16).astype(F32)
    pair = jnp.stack([hi, b - hi], axis=1)
    tile = jnp.zeros((N_HEADS, LANE), F32)
    first = tile.at[:, HEAD_DIM:HEAD_DIM + 2].set(pair)
    second = tile.at[:, 0:2].set(pair)
    return jnp.where((np.arange(N_HEADS) < HPG)[:, None], first, second).reshape(1, QPAD).astype(BF16)


def _silu(x):
    return x * jax.nn.sigmoid(x)


def _post_a_body(o_ref, z_ref, u_ref, vn_ref, zg_ref, x_ref, wsp_ref, bsp_ref, wo_ref, y_ref, *, short_chunks):
    mix = []
    if short_chunks:
        vn = vn_ref[0]
        for g in range(GMLP_GROUPS):
            vg = vn[:, g * LANE:(g + 1) * LANE]
            acc = bsp_ref[:, g:g + 1] + wsp_ref[0][:, g:g + 1] * vg
            for k in range(1, wsp_ref.shape[0]):
                acc = acc + wsp_ref[k][:, g:g + 1] * pltpu.roll(vg, k, axis=0)
            mix.append(acc)
    else:
        chunk = wsp_ref.shape[1]
        r = lax.broadcasted_iota(jnp.int32, (chunk, chunk), 0)
        c = lax.broadcasted_iota(jnp.int32, (chunk, chunk), 1)
        vn = vn_ref[0].astype(BF16)
        for g in range(GMLP_GROUPS):
            wm = jnp.where(c <= r, wsp_ref[g], 0.0).astype(BF16)
            mix.append(jnp.concatenate(
                [_mm(wm, vn[k * chunk:(k + 1) * chunk, g * LANE:(g + 1) * LANE]) + bsp_ref[:, g:g + 1]
                 for k in range(vn.shape[0] // chunk)], axis=0))
    y_g = u_ref[0] * jnp.concatenate(mix, axis=-1) * _silu(zg_ref[0])
    y_nsa = o_ref[0] * _silu(z_ref[0])
    cat = jnp.concatenate([y_nsa, y_g], axis=-1).astype(BF16)
    y_ref[0] = x_ref[0] + _mm(cat, wo_ref[...])


def _post_a(o, z, u, vn, zg, x, wsp, bsp, wo, short_chunks=False, chunks_per_step=1):
    nb, nt, _ = x.shape
    rows = wsp.shape[1] * chunks_per_step
    row = lambda width: pl.BlockSpec((1, rows, width), lambda b, i: (b, i, 0))
    return pl.pallas_call(
        functools.partial(_post_a_body, short_chunks=short_chunks),
        grid=(nb, nt // rows),
        in_specs=[row(NSA_WIDTH), row(NSA_WIDTH), row(GMLP_WIDTH), row(GMLP_WIDTH), row(GMLP_WIDTH),
                  row(D_MODEL), _const_spec(wsp.shape), _const_spec(bsp.shape), _const_spec(wo.shape)],
        out_specs=row(D_MODEL),
        out_shape=jax.ShapeDtypeStruct(x.shape, F32),
        compiler_params=_params(("arbitrary", "arbitrary")),
        name="post_a",
    )(o, z, u, vn, zg, x, wsp, bsp, wo)


SAMPLE_ROWS = 8
N_PAGES_MAX = 16


def _stack_heads(q_ref):
    return jnp.concatenate([q_ref[0, :, h * LANE:(h + 1) * LANE] for h in range(N_HEADS)], axis=0).astype(BF16)


def _pad_rows(x, rows):
    return jnp.concatenate([x, jnp.zeros((rows - x.shape[0], x.shape[1]), x.dtype)], axis=0)


def _cmp_sample_body(pt_ref, q_ref, kv_ref, pe_ref, wk_ref, wv_ref, tc_ref, pool_ref, *rest):
    n_pages = len(rest) - 2
    pages, (oc_ref, ps_ref) = rest[:n_pages], rest[n_pages:]
    lane = lax.broadcasted_iota(jnp.int32, (KV_WIDTH, LANE), 1)
    eye = lax.broadcasted_iota(jnp.int32, (KV_WIDTH, LANE), 0) == lane
    qs = _stack_heads(q_ref)
    summ = []
    for kind in range(2):
        sums = jnp.zeros((KV_WIDTH, LANE), F32)
        for p in range(0, n_pages, 2):
            x = jnp.concatenate([pages[p][kind].reshape(KV_WIDTH, PAGE), pages[p + 1][kind].reshape(KV_WIDTH, PAGE)],
                                axis=1)
            hi, lo = _split_bf16(x)
            pool = pool_ref[p * PAGE:(p + 2) * PAGE]
            sums = sums + _mm(hi, pool) + _mm(lo, pool)
        new = jnp.sum(kv_ref[0, :, kind * KV_WIDTH:(kind + 1) * KV_WIDTH], axis=0, keepdims=True)
        new_col = jnp.sum(jnp.where(eye, new, 0.0), axis=1, keepdims=True)
        sums = jnp.where(lane == 2 * n_pages, new_col, sums)
        mean_t = sums * (1.0 / CMP_BLOCK) + jnp.mean(pe_ref[kind], axis=1, keepdims=True)
        w_ref = wk_ref if kind == 0 else wv_ref
        summ.append(_mm(w_ref[...], mean_t.astype(BF16)).astype(BF16))
    tmpl = tc_ref[...]
    ok = tmpl > 0.5 * NEG
    p = jnp.where(ok, _softmax_rows(_mm(qs, summ[0]) + tmpl), 0.0)
    oc_ref[0] = _nt(p.astype(BF16), summ[1])
    ps_ref[0] = p.reshape(N_KV, HPG, SAMPLE_ROWS, LANE).sum(axis=1).reshape(N_KV * SAMPLE_ROWS, LANE)


def _page_specs(layer, n_pages, kind_block):
    def spec(p):
        return pl.BlockSpec((None, None, 2, N_KV, HEAD_DIM, PAGE),
                            lambda b, pt: (layer, pt[b * n_pages + p], kind_block, 0, 0, 0))
    return [spec(p) for p in range(n_pages)]


def _cmp_sample(layer, pt_flat, n_pages, cache_t, q, kv, pe_t, wk_t, wv_t, t_cmp):
    nseq = q.shape[0]
    rows = N_HEADS * SAMPLE_ROWS
    seq = lambda r, w: pl.BlockSpec((1, r, w), lambda b, pt: (b, 0, 0))
    cst = lambda a: pl.BlockSpec(a.shape, lambda b, pt: (0,) * a.ndim)
    assert n_pages % 2 == 0
    pool = jnp.asarray((np.arange(n_pages * PAGE) // CMP_BLOCK)[:, None] == np.arange(LANE)[None, :], BF16)
    return pl.pallas_call(
        _cmp_sample_body,
        grid_spec=pltpu.PrefetchScalarGridSpec(
            num_scalar_prefetch=1, grid=(nseq,),
            in_specs=[seq(SAMPLE_ROWS, QPAD), seq(SAMPLE_ROWS, 6 * KV_WIDTH), cst(pe_t), cst(wk_t), cst(wv_t),
                      cst(t_cmp), cst(pool)] + _page_specs(layer, n_pages, 0),
            out_specs=[seq(rows, KV_WIDTH), seq(N_KV * SAMPLE_ROWS, LANE)]),
        out_shape=[jax.ShapeDtypeStruct((nseq, rows, KV_WIDTH), F32),
                   jax.ShapeDtypeStruct((nseq, N_KV * SAMPLE_ROWS, LANE), F32)],
        compiler_params=_params(("arbitrary",)),
        name="cmp_sample",
    )(pt_flat, q, kv, pe_t, wk_t, wv_t, t_cmp, pool, *([cache_t] * n_pages))


def _select_sample_body(ps_ref, o_ref, *, past_len):
    cols = ps_ref.shape[1]
    t = lax.broadcasted_iota(jnp.int32, (1, cols), 1) % SAMPLE_ROWS
    o_ref[...] = _select_blocks(ps_ref[...], (past_len + t) // CMP_BLOCK)


def _select_sample(ps_t, past_len):
    cols = ps_t.shape[1]
    tile = min(cols, 2 * LANE)
    spec = pl.BlockSpec((LANE, tile), lambda i: (0, i))
    return pl.pallas_call(
        functools.partial(_select_sample_body, past_len=past_len),
        grid=(cols // tile,), in_specs=[spec], out_specs=spec,
        out_shape=jax.ShapeDtypeStruct(ps_t.shape, F32),
        compiler_params=_params(("arbitrary",)),
        name="select_sample",
    )(ps_t)


def _attn_sample_body(pt_ref, q_ref, kv_ref, un_ref, oc_ref, gl_ref, bg_ref, e_ref, ts_ref, tn_ref, tw_ref,
                      win_ref, *rest):
    pages, o_ref = rest[:-1], rest[-1]
    qs = _stack_heads(q_ref)
    lane = lax.broadcasted_iota(jnp.int32, (SAMPLE_ROWS, LANE), 1)
    tn = tn_ref[...]

    def new_rows(col):
        return _pad_rows(kv_ref[0, :, col * KV_WIDTH:(col + 1) * KV_WIDTH], LANE).astype(BF16)

    def branch(k_past, v_past, bias, k_col, v_col):
        s = _mm(qs, k_past) + bias
        s_new = _nt(qs, new_rows(k_col)) + tn
        m = jnp.maximum(jnp.max(s, axis=-1, keepdims=True), jnp.max(s_new, axis=-1, keepdims=True))
        p, p_new = jnp.exp(s - m), jnp.exp(s_new - m)
        den = jnp.sum(p, axis=-1, keepdims=True) + jnp.sum(p_new, axis=-1, keepdims=True)
        return (_nt(p.astype(BF16), v_past) + _mm(p_new.astype(BF16), new_rows(v_col))) / den

    cat = lambda kind: jnp.concatenate([pg[kind].reshape(KV_WIDTH, PAGE) for pg in pages], axis=1).astype(BF16)
    madd = _mm(un_ref[0].astype(BF16), e_ref[...])
    madd = jnp.concatenate([madd[g * SAMPLE_ROWS:(g + 1) * SAMPLE_ROWS] for g in range(N_KV) for _ in range(HPG)],
                           axis=0)
    o_sel = branch(cat(0), cat(1), ts_ref[...] + madd, 2, 3)
    win = lambda kind: win_ref[kind].reshape(KV_WIDTH, win_ref.shape[-1]).astype(BF16)
    o_win = branch(win(0), win(1), tw_ref[...], 4, 5)

    gate = jax.nn.sigmoid(gl_ref[0] + bg_ref[...])
    oc = oc_ref[0]

    def merged(h):
        rows = slice(h * SAMPLE_ROWS, (h + 1) * SAMPLE_ROWS)
        col = lambda k: gate[:, k * N_HEADS + h:k * N_HEADS + h + 1]
        return col(0) * oc[rows] + col(1) * o_sel[rows] + col(2) * o_win[rows]

    for c in range(HPG):
        o_ref[0, :, c * LANE:(c + 1) * LANE] = jnp.where(lane < HEAD_DIM, merged(c), merged(HPG + c))


def _attn_sample(layer, pt_flat, n_pages, cache_t, win_t, q, kv, unsel, oc, gl, bg, emask, t_sel, t_new, t_win):
    nseq = q.shape[0]
    seq = lambda a: pl.BlockSpec((1,) + a.shape[1:], lambda b, pt: (b,) + (0,) * (a.ndim - 1))
    cst = lambda a: pl.BlockSpec(a.shape, lambda b, pt: (0,) * a.ndim)
    win_spec = pl.BlockSpec((None, None, 2, N_KV, HEAD_DIM, win_t.shape[-1]),
                            lambda b, pt: (layer, b, 0, 0, 0, 0))
    return pl.pallas_call(
        _attn_sample_body,
        grid_spec=pltpu.PrefetchScalarGridSpec(
            num_scalar_prefetch=1, grid=(nseq,),
            in_specs=[seq(q), seq(kv), seq(unsel), seq(oc), seq(gl), cst(bg), cst(emask), cst(t_sel), cst(t_new),
                      cst(t_win), win_spec] + _page_specs(layer, n_pages, 1),
            out_specs=pl.BlockSpec((1, SAMPLE_ROWS, NSA_WIDTH), lambda b, pt: (b, 0, 0))),
        out_shape=jax.ShapeDtypeStruct((nseq, SAMPLE_ROWS, NSA_WIDTH), F32),
        compiler_params=_params(("arbitrary",)),
        name="attn_sample",
    )(pt_flat, q, kv, unsel, oc, gl, bg, emask, t_sel, t_new, t_win, win_t, *([cache_t] * n_pages))


def _proj_c_body(x_ref, g_ref, w_ref, xb_ref, z_ref):
    y = _mm(_rms(x_ref[...], g_ref[...]).astype(BF16), w_ref[...])
    xb_ref[...] = y[:, :LRU_WIDTH]
    z_ref[...] = y[:, LRU_WIDTH:]


def _proj_c(x, norm_g, w, tm):
    n = x.shape[0]
    row = pl.BlockSpec((tm, LRU_WIDTH), lambda i: (i, 0))
    return pl.pallas_call(
        _proj_c_body,
        grid=(n // tm,),
        in_specs=[row, _const_spec((1, D_MODEL)), _const_spec((D_MODEL, 2 * LRU_WIDTH))],
        out_specs=[row, row],
        out_shape=[jax.ShapeDtypeStruct((n, LRU_WIDTH), F32)] * 2,
        compiler_params=_params(("arbitrary",)),
        name="proj_c",
    )(x, norm_g, w)


def _lru_coeffs(xc, wra_ref, bra_ref, wrx_ref, brx_ref, lam_ref):
    xb = xc.astype(BF16)
    blocks = lambda w_ref: jnp.concatenate(
        [_mm(xb[:, k * LRU_BLOCK:(k + 1) * LRU_BLOCK], w_ref[k]) for k in range(LRU_HEADS)], axis=-1)
    r = jax.nn.sigmoid(blocks(wra_ref) + bra_ref[...])
    ig = jax.nn.sigmoid(blocks(wrx_ref) + brx_ref[...])
    log_a = -LRU_C * r * jax.nn.softplus(-lam_ref[...])
    return jnp.exp(log_a), jnp.sqrt(-_expm1(2.0 * log_a)) * ig * xc


def _expm1(x):
    u = jnp.exp(x)
    small = x > -1.0
    us = jnp.where(small & (u != 1.0), u, 0.5)
    return jnp.where(small, jnp.where(u == 1.0, x, (us - 1.0) * x / jnp.log(us)), u - 1.0)


def _mix_c_prompt_body(xb_ref, z_ref, x_ref, cw_ref, cb_ref, wra_ref, bra_ref, wrx_ref, brx_ref, lam_ref, wo_ref,
                       fg_ref, y_ref, hl_ref, tail_ref, xin_scr, a_scr, b_scr, h_scr, *, final_norm):
    tt = xb_ref.shape[1]
    pad = 8
    hist = CONV_WIDTH - 1

    @pl.when(pl.program_id(1) == 0)
    def _():
        xin_scr[0:pad] = jnp.zeros((pad, LRU_WIDTH), F32)
        h_scr[...] = jnp.zeros(h_scr.shape, F32)

    xin_scr[pad:pad + tt] = xb_ref[0]
    xc = cb_ref[...] + xin_scr[pad - hist:pad - hist + tt] * cw_ref[0:1]
    for k in range(1, CONV_WIDTH):
        xc = xc + xin_scr[pad - hist + k:pad - hist + k + tt] * cw_ref[k:k + 1]
    a, b = _lru_coeffs(xc, wra_ref, bra_ref, wrx_ref, brx_ref, lam_ref)
    a_scr[...] = a
    b_scr[...] = b

    row = lax.broadcasted_iota(jnp.int32, (8, LRU_WIDTH), 0)

    def step(k, h_prev):
        r0 = pl.multiple_of(k * 8, 8)
        ca, cbv = a_scr[pl.ds(r0, 8)], b_scr[pl.ds(r0, 8)]
        for s in (1, 2, 4):
            keep = row >= s
            cbv = cbv + ca * jnp.where(keep, pltpu.roll(cbv, s, axis=0), 0.0)
            ca = ca * jnp.where(keep, pltpu.roll(ca, s, axis=0), 1.0)
        hs = ca * h_prev + cbv
        b_scr[pl.ds(r0, 8)] = hs
        return jnp.broadcast_to(hs[7:8], (8, LRU_WIDTH))

    h_scr[...] = lax.fori_loop(0, tt // 8, step, h_scr[...])
    y = (b_scr[...] * _silu(z_ref[0])).astype(BF16)
    out = x_ref[0] + _mm(y, wo_ref[...])
    y_ref[0] = _rms(out, fg_ref[...]) if final_norm else out
    hl_ref[0] = h_scr[0:1]
    tail_ref[0] = xin_scr[pad + tt - hist:pad + tt]
    xin_scr[0:pad] = xin_scr[tt:tt + pad]


def _mix_c_prompt(xb, z, x, cw, cb, wra, bra, wrx, brx, lam, wo, fg, final_norm, tt=512):
    nb, nt, _ = x.shape
    row = pl.BlockSpec((1, tt, LRU_WIDTH), lambda b, i: (b, i, 0))
    cst = lambda a: _const_spec(a.shape)
    hist = CONV_WIDTH - 1
    return pl.pallas_call(
        functools.partial(_mix_c_prompt_body, final_norm=final_norm),
        grid=(nb, nt // tt),
        in_specs=[row, row, row, cst(cw), cst(cb), cst(wra), cst(bra), cst(wrx), cst(brx), cst(lam), cst(wo),
                  cst(fg)],
        out_specs=[row, pl.BlockSpec((1, 1, LRU_WIDTH), lambda b, i: (b, 0, 0)),
                   pl.BlockSpec((1, hist, LRU_WIDTH), lambda b, i: (b, 0, 0))],
        out_shape=[jax.ShapeDtypeStruct(x.shape, F32), jax.ShapeDtypeStruct((nb, 1, LRU_WIDTH), F32),
                   jax.ShapeDtypeStruct((nb, hist, LRU_WIDTH), F32)],
        scratch_shapes=[pltpu.VMEM((tt + 8, LRU_WIDTH), F32), pltpu.VMEM((tt, LRU_WIDTH), F32),
                        pltpu.VMEM((tt, LRU_WIDTH), F32), pltpu.VMEM((8, LRU_WIDTH), F32)],
        compiler_params=_params(("arbitrary", "arbitrary")),
        name="mix_c_prompt",
    )(xb, z, x, cw, cb, wra, bra, wrx, brx, lam, wo, fg)


def _mix_c_sample_body(xb_ref, z_ref, x_ref, h0_ref, c0_ref, cw_ref, cb_ref, wra_ref, bra_ref, wrx_ref, brx_ref,
                       lam_ref, wo_ref, fg_ref, y_ref, hl_ref, tail_ref, *, final_norm):
    nt = xb_ref.shape[0]
    hist = CONV_WIDTH - 1
    xin = [c0_ref[k] for k in range(hist)] + [xb_ref[t] for t in range(nt)]
    h = h0_ref[...]
    for t in range(nt):
        xc = cb_ref[...] + xin[t] * cw_ref[0:1]
        for k in range(1, CONV_WIDTH):
            xc = xc + xin[t + k] * cw_ref[k:k + 1]
        a, b = _lru_coeffs(xc, wra_ref, bra_ref, wrx_ref, brx_ref, lam_ref)
        h = a * h + b
        out = x_ref[t] + _mm((h * _silu(z_ref[t])).astype(BF16), wo_ref[...])
        y_ref[t] = _rms(out, fg_ref[...]) if final_norm else out
    hl_ref[...] = h
    for k in range(hist):
        tail_ref[k] = xin[nt + k]


def _mix_c_sample(xb, z, x, h0, c0, cw, cb, wra, bra, wrx, brx, lam, wo, fg, final_norm):
    args = (xb, z, x, h0, c0, cw, cb, wra, bra, wrx, brx, lam, wo, fg)
    return pl.pallas_call(
        functools.partial(_mix_c_sample_body, final_norm=final_norm),
        grid=(1,),
        in_specs=[_const_spec(a.shape) for a in args],
        out_specs=[_const_spec(x.shape), _const_spec(h0.shape), _const_spec(c0.shape)],
        out_shape=[jax.ShapeDtypeStruct(x.shape, F32), jax.ShapeDtypeStruct(h0.shape, F32),
                   jax.ShapeDtypeStruct(c0.shape, F32)],
        compiler_params=_params(("arbitrary",)),
        name="mix_c_sample",
    )(*args)


def _pair_perm():
    cols = []
    for c in range(HPG):
        cols += list(range(c * HEAD_DIM, (c + 1) * HEAD_DIM))
        cols += list(range((HPG + c) * HEAD_DIM, (HPG + c + 1) * HEAD_DIM))
    return np.asarray(cols)


def _layout_w_in_a(w):
    o_kv = NSA_WIDTH
    o_g = o_kv + 6 * KV_WIDTH
    o_z = o_g + 3 * N_HEADS
    o_rest = o_z + NSA_WIDTH
    wq = w[:, :NSA_WIDTH].reshape(D_MODEL, N_HEADS, HEAD_DIM)
    zeros = jnp.zeros_like(wq)
    first = (np.arange(N_HEADS) < HPG)[None, :, None]
    wq = jnp.concatenate([jnp.where(first, wq, zeros), jnp.where(first, zeros, wq)], axis=-1)
    wg = jnp.pad(w[:, o_g:o_z], ((0, 0), (0, LANE - 3 * N_HEADS)))
    wz = w[:, o_z:o_rest][:, _pair_perm()]
    return jnp.concatenate([wq.reshape(D_MODEL, QPAD), w[:, o_kv:o_g], wz, w[:, o_rest:], wg], axis=1).astype(BF16)


def _block_diag(w, n):
    return jnp.kron(jnp.eye(n, dtype=w.dtype), w)


def _even_layer(a, yp, ys, cache_t, win_t, pt_flat, n_pages, tmpl, p):
    nb, nt, _ = yp.shape
    nseq, dec_t, _ = ys.shape
    past_len = n_pages * PAGE
    w_in = _layout_w_in_a(p['w_in_a'][a])
    norm_g = p['norm_a'][a][None]
    ln_g, ln_b = p['gmlp_ln_g'][a][None], p['gmlp_ln_b'][a][None]
    bg = jnp.pad(p['b_gate_a'][a], (0, LANE - 3 * N_HEADS))[None]
    w_out = jnp.concatenate([p['w_out_a'][a][:NSA_WIDTH][_pair_perm()], p['w_out_a'][a][NSA_WIDTH:]], axis=0).astype(BF16)
    pe, wc = p['pe_cmp'][a], p['w_cmp'][a]

    q, kvt, kvtb, gl, z, u, vn, zg = _proj_a(yp, norm_g, w_in, ln_g, ln_b, tm=256, feature_major=True)
    pe_rows = jnp.concatenate([pe[0], pe[0], pe[1], pe[1]], axis=1)
    w_bd = jnp.zeros((2 * KV_WIDTH, 2 * KV_WIDTH), F32)
    for k in range(4):
        w_bd = w_bd.at[k * HEAD_DIM:(k + 1) * HEAD_DIM, k * HEAD_DIM:(k + 1) * HEAD_DIM].set(wc[k // 2])
    cmp = _cmp_prompt(kvt, pe_rows, w_bd.astype(BF16))
    kaug, vaug, kwin, vwaug = _prompt_kv_operands(kvtb, tmpl['e_prompt'])
    o = _attn_prompt(q, tmpl['far_lanes'], cmp, kaug, vaug, kwin, vwaug, gl, bg, tmpl['cmp'], tmpl['near'], tmpl['win'])
    yp_new = _post_a(o, z, u, vn, zg, yp, p['w_spatial'][a], p['b_spatial'][a].T, w_out,
                     chunks_per_step=min(4, nt // GMLP_CHUNK))
    rows_p = jnp.transpose(kvt[:, :4 * KV_WIDTH].reshape(nb, 4, N_KV, HEAD_DIM, nt), (0, 4, 1, 2, 3))
    wlen = min(WINDOW, nt)
    win_p = jnp.transpose(kvt[:, 4 * KV_WIDTH:, nt - wlen:].reshape(nb, 2, N_KV, HEAD_DIM, wlen), (0, 4, 1, 2, 3))

    xs = ys.reshape(1, nseq * dec_t, D_MODEL)
    q, kv, gl, z, u, vn, zg = _proj_a(xs, norm_g, w_in, ln_g, ln_b, tm=nseq * dec_t, feature_major=False)
    pad_t = lambda x: jnp.pad(x.reshape(nseq, dec_t, -1), ((0, 0), (0, SAMPLE_ROWS - dec_t), (0, 0)))
    q8, kv8, gl8 = pad_t(q), pad_t(kv), pad_t(gl)
    pe_t = jnp.concatenate([jnp.swapaxes(pe, 1, 2)] * N_KV, axis=1)
    wk_t = _block_diag(wc[0].T, N_KV).astype(BF16)
    wv_t = _block_diag(wc[1].T, N_KV).astype(BF16)
    oc, ps = _cmp_sample(a, pt_flat, n_pages, cache_t, q8, kv8, pe_t, wk_t, wv_t, tmpl['s_cmp'])
    unsel_t = _select_sample(ps.reshape(nseq * N_KV * SAMPLE_ROWS, LANE).T, past_len)
    unsel = unsel_t.T.reshape(nseq, N_KV * SAMPLE_ROWS, LANE)
    o8 = _attn_sample(a, pt_flat, n_pages, cache_t, win_t, q8, kv8, unsel, oc, gl8, bg, tmpl['e_sample'],
                      tmpl['s_sel'], tmpl['s_new'], tmpl['s_win'])
    o = o8[:, :dec_t].reshape(1, nseq * dec_t, NSA_WIDTH)
    w_small = p['w_spatial'][a][:, :dec_t, :dec_t]
    above = lambda k: jnp.pad(jnp.diagonal(w_small, offset=-k, axis1=1, axis2=2), ((0, 0), (k, 0))).T
    w_sp = jnp.stack([jnp.tile(above(k), (nseq, 1)) for k in range(dec_t)])
    b_sp = jnp.tile(p['b_spatial'][a][:, :dec_t].T, (nseq, 1))
    ys_new = _post_a(o, z, u, vn, zg, xs, w_sp, b_sp, w_out, short_chunks=True).reshape(nseq, dec_t, D_MODEL)
    rows_s = kv[0, :, :4 * KV_WIDTH].reshape(nseq, dec_t, 4, N_KV, HEAD_DIM)
    win_s = kv[0, :, 4 * KV_WIDTH:].reshape(nseq, dec_t, 2, N_KV, HEAD_DIM)
    return yp_new, ys_new, rows_p, rows_s, win_p, win_s, vn.reshape(nseq, dec_t, GMLP_WIDTH)


def _odd_layer(c, yp, ys, h0, conv0, p, final):
    nb, nt, _ = yp.shape
    nseq, dec_t, _ = ys.shape
    norm_g = p['norm_c'][c][None]
    w_in = p['w_in_c'][c].astype(BF16)
    consts = (p['conv_w'][c], p['conv_b'][c][None], p['w_rg_a'][c].astype(BF16), p['b_rg_a'][c][None],
              p['w_rg_x'][c].astype(BF16), p['b_rg_x'][c][None], p['lru_lambda'][c][None],
              p['w_out_c'][c].astype(BF16), p['final_norm'][None])
    xb, z = _proj_c(yp.reshape(nb * nt, D_MODEL), norm_g, w_in, tm=512)
    yp_new, h_p, tail_p = _mix_c_prompt(xb.reshape(nb, nt, -1), z.reshape(nb, nt, -1), yp, *consts, final_norm=final)
    xs = jnp.swapaxes(ys, 0, 1)
    xb, z = _proj_c(xs.reshape(dec_t * nseq, D_MODEL), norm_g, w_in, tm=dec_t * nseq)
    tm3 = lambda x: x.reshape(dec_t, nseq, -1)
    ys_new, h_s, tail_s = _mix_c_sample(tm3(xb), tm3(z), xs, h0, jnp.swapaxes(conv0, 0, 1), *consts,
                                        final_norm=final)
    return yp_new, jnp.swapaxes(ys_new, 0, 1), h_p[:, 0], h_s, tail_p, jnp.swapaxes(tail_s, 0, 1)


def kernel(x_prompt, x_sample, cache_nsa, cache_win, state_lru_h, state_lru_conv, page_table, norm_a, w_in_a, b_gate_a, pe_cmp, w_cmp, gmlp_ln_g, gmlp_ln_b, w_spatial, b_spatial, w_out_a, norm_c, w_in_c, conv_w, conv_b, w_rg_a, b_rg_a, w_rg_x, b_rg_x, lru_lambda, w_out_c, rel_bias, final_norm):
    p = dict(norm_a=norm_a, w_in_a=w_in_a, b_gate_a=b_gate_a, pe_cmp=pe_cmp, w_cmp=w_cmp, gmlp_ln_g=gmlp_ln_g,
             gmlp_ln_b=gmlp_ln_b, w_spatial=w_spatial, b_spatial=b_spatial, w_out_a=w_out_a, norm_c=norm_c,
             w_in_c=w_in_c, conv_w=conv_w, conv_b=conv_b, w_rg_a=w_rg_a, b_rg_a=b_rg_a, w_rg_x=w_rg_x,
             b_rg_x=b_rg_x, lru_lambda=lru_lambda, w_out_c=w_out_c, rel_bias=rel_bias, final_norm=final_norm)
    nb, nt, _ = x_prompt.shape
    nseq, n_pages = page_table.shape
    past_len = n_pages * PAGE
    wlen = cache_win.shape[2]
    depth = norm_a.shape[0] + norm_c.shape[0]
    assert nt % KV_TILE == 0 and nt >= WIN_KEYS and nt // CMP_BLOCK <= LANE
    assert past_len // CMP_BLOCK < LANE and x_sample.shape[1] <= SAMPLE_ROWS and n_pages <= N_PAGES_MAX

    cache_t = jnp.transpose(cache_nsa, (0, 1, 3, 4, 5, 2))
    win_t = jnp.transpose(cache_win, (0, 1, 3, 4, 5, 2))
    pt_flat = page_table.reshape(-1).astype(jnp.int32)

    blk_of_key = lambda n: (np.arange(n) // CMP_BLOCK)[None, :] == np.arange(LANE)[:, None]
    rows_s = SAMPLE_ROWS
    stack = lambda t: t.reshape(N_HEADS * rows_s, t.shape[-1])
    by_group = lambda t: t.reshape(N_KV, GROUP_ROWS, t.shape[-1])
    assert NEAR_SPAN + LANE - (KV_TILE - 1) >= _UPPER[-1] and nt % KV_TILE == 0
    tmpl = dict(
        far_lanes=_far_bias_lanes(rel_bias),
        e_prompt=jnp.asarray(np.where(blk_of_key(nt), NEG, 0.0), BF16),
        e_sample=jnp.asarray(np.where(blk_of_key(past_len), NEG, 0.0), BF16),
        cmp=_bias_template(rel_bias, Q_BLOCK, LANE, -CMP_BLOCK, CMP_BLOCK * (LANE - 2) - (CMP_BLOCK - 1)),
        near=by_group(_bias_template(rel_bias, Q_BLOCK, NEAR_SPAN + KV_TILE, -1, NEAR_SPAN)),
        win=by_group(_bias_template(rel_bias, Q_BLOCK, WINDOW + WIN_KEYS, -1, WINDOW, hi=WINDOW)),
        s_cmp=stack(_bias_template(rel_bias, rows_s, LANE, -CMP_BLOCK, past_len - (CMP_BLOCK - 1))),
        s_sel=stack(_bias_template(rel_bias, rows_s, past_len, -1, past_len)),
        s_new=stack(_bias_template(rel_bias, rows_s, LANE, -1, 0, cmax=x_sample.shape[1])),
        s_win=stack(_bias_template(rel_bias, rows_s, wlen, -1, wlen, hi=WINDOW)),
    )

    yp, ys = x_prompt, x_sample
    outs = [[] for _ in range(9)]
    for layer in range(depth):
        if layer % 2 == 0:
            yp, ys, *leaves = _even_layer(layer // 2, yp, ys, cache_t, win_t, pt_flat, n_pages, tmpl, p)
            for dst, leaf in zip(outs[:5], leaves):
                dst.append(leaf)
        else:
            c = layer // 2
            yp, ys, *leaves = _odd_layer(c, yp, ys, state_lru_h[c], state_lru_conv[c], p, final=layer == depth - 1)
            for dst, leaf in zip(outs[5:], leaves):
                dst.append(leaf)
    return (yp, ys) + tuple(jnp.stack(o) for o in outs)
```

```python
import functools
import math

import numpy as np
import jax
import jax.numpy as jnp
from jax import lax
from jax.experimental import pallas as pl
from jax.experimental.pallas import tpu as pltpu

F32 = jnp.float32
BF16 = jnp.bfloat16

D_MODEL = 1024
N_HEADS = 8
N_KV = 2
HPG = N_HEADS // N_KV
HEAD_DIM = 64
NSA_WIDTH = N_HEADS * HEAD_DIM
KV_WIDTH = N_KV * HEAD_DIM
CMP_BLOCK = 64
N_SEL = 16
WINDOW = 512
GMLP_GROUPS = 4
GMLP_WIDTH = 512
GMLP_CHUNK = 128
LRU_WIDTH = 1024
LRU_HEADS = 8
LRU_BLOCK = 128
CONV_WIDTH = 4
LRU_C = 8.0
REL_BUCKETS = 32
REL_MAX_DIST = 1024
EPS = 1e-6
NEG = -1e30
PAGE = 128

LANE = 128
Q_BLOCK = 128
KV_TILE = 512
NEAR_SPAN = 1280
WIN_KEYS = WINDOW + Q_BLOCK
VMEM_LIMIT = 56 * 1024 * 1024

QPAD = N_HEADS * LANE
C_KV = QPAD
C_Z = C_KV + 6 * KV_WIDTH
C_U = C_Z + NSA_WIDTH
C_V = C_U + GMLP_WIDTH
C_ZG = C_V + GMLP_WIDTH
C_G = C_ZG + GMLP_WIDTH
PROJ_COLS = C_G + LANE


def _nt(a, b):
    return lax.dot_general(a, b, (((1,), (1,)), ((), ())), preferred_element_type=F32)


def _mm(a, b):
    return jnp.dot(a, b, preferred_element_type=F32)


def _params(sem):
    return pltpu.CompilerParams(dimension_semantics=sem, vmem_limit_bytes=VMEM_LIMIT)


def _const_spec(shape):
    n = len(shape)
    return pl.BlockSpec(shape, lambda *_: (0,) * n)


def _bucket_upper_bounds():
    max_exact = REL_BUCKETS // 2
    d = np.arange(0, 4 * REL_MAX_DIST, dtype=np.int64)

    def buckets(ft):
        df = np.maximum(d, 1).astype(ft)
        large = max_exact + (np.log(df / ft(max_exact)) / ft(math.log(REL_MAX_DIST / max_exact))
                             * ft(REL_BUCKETS - max_exact)).astype(np.int32)
        return np.where(d < max_exact, d, np.minimum(large, REL_BUCKETS - 1))

    b32, b64 = buckets(np.float32), buckets(np.float64)
    assert (b32 == b64).all() and (np.diff(b32) >= 0).all() and b32[-1] == REL_BUCKETS - 1
    return [int(np.argmax(b32 > k)) for k in range(REL_BUCKETS - 1)]


_UPPER = _bucket_upper_bounds()


def _bias_tmpl_body(tab_ref, o_ref, *, rows, cs, off, hi, cmax):
    h = pl.program_id(0)
    t = lax.broadcasted_iota(jnp.int32, (rows, LANE), 0)
    lane = lax.broadcasted_iota(jnp.int32, (rows, LANE), 1)

    def chunk(k, carry):
        c0 = pl.multiple_of(k * LANE, LANE)
        c = lane + c0
        d = t + cs * c + off
        dd = jnp.maximum(d, 0)
        val = jnp.full((rows, LANE), tab_ref[REL_BUCKETS - 1, h], F32)
        for b in range(REL_BUCKETS - 2, -1, -1):
            val = jnp.where(dd < _UPPER[b], tab_ref[b, h], val)
        ok = (d >= 0) & (d <= hi) & (c < cmax)
        o_ref[0, :, pl.ds(c0, LANE)] = jnp.where(ok, val, NEG)
        return carry

    lax.fori_loop(0, o_ref.shape[2] // LANE, chunk, 0)


def _bias_template(rel_bias, rows, width, cs, off, hi=1 << 30, cmax=1 << 30):
    return pl.pallas_call(
        functools.partial(_bias_tmpl_body, rows=rows, cs=cs, off=off, hi=hi, cmax=cmax),
        grid=(N_HEADS,),
        in_specs=[pl.BlockSpec(memory_space=pltpu.SMEM)],
        out_specs=pl.BlockSpec((1, rows, width), lambda h: (h, 0, 0)),
        out_shape=jax.ShapeDtypeStruct((N_HEADS, rows, width), F32),
        compiler_params=_params(("arbitrary",)),
        name="bias_template",
    )(rel_bias)


def _rms(x, g):
    return x * lax.rsqrt(jnp.mean(x * x, axis=-1, keepdims=True) + EPS) * g


def _proj_a_body(x_ref, g_ref, w_ref, lng_ref, lnb_ref, q_ref, *rest, feature_major):
    xn = _rms(x_ref[0], g_ref[...])
    y = _mm(xn.astype(BF16), w_ref[...])
    q_ref[0] = (y[:, :QPAD] * (HEAD_DIM ** -0.5)).astype(q_ref.dtype)
    kv = y[:, C_KV:C_Z]
    if feature_major:
        nsa_ref, win_ref, kvtb_ref, gl_ref, z_ref, u_ref, vn_ref, zg_ref = rest
        kvt = kv.T
        nsa_ref[0] = kvt[:4 * KV_WIDTH]
        win_ref[0] = kvt[4 * KV_WIDTH:]
        kvtb_ref[0] = kvt.astype(BF16)
    else:
        kv_ref, gl_ref, z_ref, u_ref, vn_ref, zg_ref = rest
        kv_ref[0] = kv
    z_ref[0] = y[:, C_Z:C_U]
    u_ref[0] = y[:, C_U:C_V]
    v = y[:, C_V:C_ZG]
    mu = jnp.mean(v, axis=-1, keepdims=True)
    var = jnp.mean(jnp.square(v - mu), axis=-1, keepdims=True)
    vn_ref[0] = (v - mu) * lax.rsqrt(var + EPS) * lng_ref[...] + lnb_ref[...]
    zg_ref[0] = y[:, C_ZG:C_G]
    gl_ref[0] = y[:, C_G:]


def _proj_a(x, norm_g, w, ln_g, ln_b, tm, feature_major):
    nb, nt, _ = x.shape
    row = lambda width: pl.BlockSpec((1, tm, width), lambda b, i: (b, i, 0))
    colT = pl.BlockSpec((1, 6 * KV_WIDTH, tm), lambda b, i: (b, 0, i))
    sds = lambda shape, dt=F32: jax.ShapeDtypeStruct(shape, dt)
    if feature_major:
        col = lambda r: pl.BlockSpec((1, r, tm), lambda b, i: (b, 0, i))
        kv_specs = [col(4 * KV_WIDTH), col(2 * KV_WIDTH), colT]
        kv_shapes = [sds((nb, 4 * KV_WIDTH, nt)), sds((nb, 2 * KV_WIDTH, nt)), sds((nb, 6 * KV_WIDTH, nt), BF16)]
    else:
        kv_specs, kv_shapes = [row(6 * KV_WIDTH)], [sds((nb, nt, 6 * KV_WIDTH))]
    return pl.pallas_call(
        functools.partial(_proj_a_body, feature_major=feature_major),
        grid=(nb, nt // tm),
        in_specs=[row(D_MODEL), _const_spec((1, D_MODEL)), _const_spec((D_MODEL, PROJ_COLS)),
                  _const_spec((1, GMLP_WIDTH)), _const_spec((1, GMLP_WIDTH))],
        out_specs=[row(QPAD)] + kv_specs + [row(LANE), row(NSA_WIDTH), row(GMLP_WIDTH), row(GMLP_WIDTH),
                                            row(GMLP_WIDTH)],
        out_shape=[sds((nb, nt, QPAD), BF16 if feature_major else F32)] + kv_shapes
                  + [sds((nb, nt, LANE)), sds((nb, nt, NSA_WIDTH)), sds((nb, nt, GMLP_WIDTH)),
                     sds((nb, nt, GMLP_WIDTH)), sds((nb, nt, GMLP_WIDTH))],
        compiler_params=_params(("arbitrary", "arbitrary")),
        name="proj_a",
    )(x, norm_g, w, ln_g, ln_b)


def _split_bf16(x):
    hi = x.astype(BF16)
    return hi, (x - hi.astype(F32)).astype(BF16)


def _cmp_prompt_body(kvt_ref, pe_ref, w_ref, o_ref):
    nt = kvt_ref.shape[2]
    blk = lax.broadcasted_iota(jnp.int32, (LANE, nt), 0)
    pos = lax.broadcasted_iota(jnp.int32, (LANE, nt), 1)
    pool = jnp.where(pos // CMP_BLOCK == blk, 1.0 / CMP_BLOCK, 0.0).astype(BF16)
    hi, lo = _split_bf16(kvt_ref[0])
    mean = _nt(pool, hi) + _nt(pool, lo)
    mean = mean + jnp.mean(pe_ref[...], axis=0, keepdims=True)
    o_ref[0] = _mm(mean.astype(BF16), w_ref[...])


def _cmp_prompt(kvt, pe_rows, w_bd):
    nb, _, nt = kvt.shape
    return pl.pallas_call(
        _cmp_prompt_body,
        grid=(nb,),
        in_specs=[pl.BlockSpec((1, 2 * KV_WIDTH, nt), lambda b: (b, 0, 0)),
                  _const_spec((CMP_BLOCK, 2 * KV_WIDTH)), _const_spec((2 * KV_WIDTH, 2 * KV_WIDTH))],
        out_specs=pl.BlockSpec((1, LANE, 2 * KV_WIDTH), lambda b: (b, 0, 0)),
        out_shape=jax.ShapeDtypeStruct((nb, LANE, 2 * KV_WIDTH), F32),
        compiler_params=_params(("arbitrary",)),
        name="cmp_prompt",
    )(kvt, pe_rows, w_bd)


def _select_blocks(score_t, cur):
    shape = score_t.shape
    blk = lax.broadcasted_iota(jnp.int32, shape, 0)
    forced = (blk == 0) | (blk == cur) | (blk == cur - 1)
    score = jnp.where(forced | (blk >= cur), -jnp.inf, score_t)

    def pick(_, carry):
        score, unsel = carry
        mx = jnp.max(score, axis=0, keepdims=True)
        idx = jnp.min(jnp.where(score == mx, blk, shape[0]), axis=0, keepdims=True)
        hit = blk == idx
        unsel = jnp.where(hit & (mx > -jnp.inf), 0.0, unsel)
        return jnp.where(hit, -jnp.inf, score), unsel

    _, unsel = lax.fori_loop(0, N_SEL - 3, pick, (score, jnp.where(forced, 0.0, 1.0)))
    return unsel


def _softmax_rows(s):
    m = jnp.max(s, axis=-1, keepdims=True)
    e = jnp.exp(s - m)
    return e / jnp.sum(e, axis=-1, keepdims=True)


GROUP_ROWS = HPG * Q_BLOCK


def _rowmax(s):
    m = s[:, :LANE]
    for k in range(1, s.shape[1] // LANE):
        m = jnp.maximum(m, s[:, k * LANE:(k + 1) * LANE])
    return jnp.broadcast_to(jnp.max(m, axis=-1, keepdims=True), (s.shape[0], LANE))


def _tile_lanes(x, n):
    return jnp.concatenate([x] * n, axis=1)


def _attn_prompt_body(q_ref, bl_ref, cmp_ref, ka_ref, va_ref, kw_ref, vw_ref, gl_ref, bg_ref,
                      tc_ref, tn_ref, tw_ref, o_ref, lhs_scr, m_scr, acc_scr, part_scr, gsel_scr):
    i = pl.program_id(1)
    qf = q_ref[0]
    lane = lax.broadcasted_iota(jnp.int32, (Q_BLOCK, LANE), 1)
    stack = lambda x, g: jnp.concatenate([x[:, h * LANE:(h + 1) * LANE] for h in range(g * HPG, (g + 1) * HPG)], axis=0)
    qs = [stack(qf, g) for g in range(N_KV)]
    qs_far = [stack(qf + bl_ref[...], g) for g in range(N_KV)]

    kc = cmp_ref[0, :, :KV_WIDTH].astype(BF16)
    vc = cmp_ref[0, :, KV_WIDTH:].astype(BF16)
    seen = lane <= 2 * i + 1
    shift = (2 * i + 2) % LANE
    o_cmp, p_sum = [], []
    for g in range(N_KV):
        tmpl = jnp.concatenate([jnp.where(seen, pltpu.roll(tc_ref[h], shift, axis=1), NEG)
                                for h in range(g * HPG, (g + 1) * HPG)], axis=0)
        ok = tmpl > 0.5 * NEG
        p = jnp.where(ok, _softmax_rows(_nt(qs[g], kc) + tmpl), 0.0)
        o_cmp.append(_mm(p.astype(BF16), vc))
        p_sum.append(p.reshape(HPG, Q_BLOCK, LANE).sum(axis=0))

    w0 = pl.multiple_of(jnp.maximum(i * Q_BLOCK - WINDOW, 0), LANE)
    woff = pl.multiple_of(WINDOW - (i * Q_BLOCK - w0), LANE)
    kwt = kw_ref[0, :, pl.ds(w0, WIN_KEYS)]
    gate = jax.nn.sigmoid(gl_ref[0] + bg_ref[...])
    for g in range(N_KV):
        col = lambda k: jnp.concatenate([gate[:, k * N_HEADS + h:k * N_HEADS + h + 1]
                                         for h in range(g * HPG, (g + 1) * HPG)], axis=0)
        s = _mm(qs[g], kwt) + tw_ref[g, :, pl.ds(woff, WIN_KEYS)]
        e = jnp.exp(s - _tile_lanes(_rowmax(s), WIN_KEYS // LANE)).astype(BF16)
        pv = _nt(e, vw_ref[0, g, :, pl.ds(w0, WIN_KEYS)])
        part_scr[g] = col(0) * o_cmp[g] + col(2) * (pv / pltpu.roll(pv, HEAD_DIM, axis=1))
        gsel_scr[g] = jnp.broadcast_to(col(1), (GROUP_ROWS, LANE))

    tpos = i * Q_BLOCK + lax.broadcasted_iota(jnp.int32, (1, N_KV * Q_BLOCK), 1) % Q_BLOCK
    unsel_t = _select_blocks(jnp.concatenate([ps.T for ps in p_sum], axis=1), tpos // CMP_BLOCK)
    for g in range(N_KV):
        un = _tile_rows(unsel_t[:, g * Q_BLOCK:(g + 1) * Q_BLOCK].T.astype(BF16), HPG)
        lhs_scr[0, g] = jnp.concatenate([qs_far[g], un], axis=1)
        lhs_scr[1, g] = jnp.concatenate([qs[g], un], axis=1)

    m_scr[...] = jnp.full(m_scr.shape, NEG, F32)
    acc_scr[...] = jnp.zeros(acc_scr.shape, F32)

    def tile(j, near):
        k0 = pl.multiple_of(j * KV_TILE, KV_TILE)
        s = [_mm(lhs_scr[1 if near else 0, g], ka_ref[0, g, :, pl.ds(k0, KV_TILE)]) for g in range(N_KV)]
        if near:
            off = pl.multiple_of(NEAR_SPAN - (i * Q_BLOCK - j * KV_TILE), LANE)
            s = [s[g] + tn_ref[g, :, pl.ds(off, KV_TILE)] for g in range(N_KV)]
        m_old = [m_scr[g] for g in range(N_KV)]
        m_new = [jnp.maximum(m_old[g], _rowmax(s[g])) for g in range(N_KV)]
        p = [jnp.exp(s[g] - _tile_lanes(m_new[g], KV_TILE // LANE)).astype(BF16) for g in range(N_KV)]
        for g in range(N_KV):
            acc_scr[g] = jnp.exp(m_old[g] - m_new[g]) * acc_scr[g] + _nt(p[g], va_ref[0, g, :, pl.ds(k0, KV_TILE)])
            m_scr[g] = m_new[g]

    n_far = jnp.maximum(i * Q_BLOCK - (NEAR_SPAN + LANE) + KV_TILE, 0) // KV_TILE
    n_tiles = i // (KV_TILE // Q_BLOCK) + 1
    def far_pair(jj, c):
        tile(2 * jj, False)
        tile(2 * jj + 1, False)
        return c

    lax.fori_loop(0, n_far // 2, far_pair, 0)

    @pl.when(n_far % 2 == 1)
    def _():
        tile(n_far - 1, False)

    lax.fori_loop(n_far, n_tiles, lambda j, c: (tile(j, True), c)[1], 0)

    merged = []
    for g in range(N_KV):
        acc = acc_scr[g]
        merged.append(part_scr[g] + gsel_scr[g] * (acc / pltpu.roll(acc, HEAD_DIM, axis=1)))
    for c in range(HPG):
        rows = slice(c * Q_BLOCK, (c + 1) * Q_BLOCK)
        o_ref[0, :, c * LANE:(c + 1) * LANE] = jnp.where(lane < HEAD_DIM, merged[0][rows], merged[1][rows])


def _tile_rows(x, n):
    return jnp.concatenate([x] * n, axis=0)


def _attn_prompt(q, bias_lanes, cmp, kaug, vaug, kwin, vwaug, gl, bg, t_cmp, t_near, t_win):
    nb, nt, _ = q.shape
    once = pl.Buffered(1)
    per_b = lambda a: pl.BlockSpec((1,) + a.shape[1:], lambda b, i: (b,) + (0,) * (a.ndim - 1), pipeline_mode=once)
    whole = lambda a: pl.BlockSpec(a.shape, lambda b, i: (0,) * a.ndim, pipeline_mode=once)
    return pl.pallas_call(
        _attn_prompt_body,
        grid=(nb, nt // Q_BLOCK),
        in_specs=[pl.BlockSpec((1, Q_BLOCK, QPAD), lambda b, i: (b, i, 0)),
                  whole(bias_lanes), per_b(cmp), per_b(kaug), per_b(vaug), per_b(kwin), per_b(vwaug),
                  pl.BlockSpec((1, Q_BLOCK, LANE), lambda b, i: (b, i, 0)),
                  _const_spec((1, LANE)), whole(t_cmp), whole(t_near), whole(t_win)],
        out_specs=pl.BlockSpec((1, Q_BLOCK, NSA_WIDTH), lambda b, i: (b, i, 0)),
        out_shape=jax.ShapeDtypeStruct((nb, nt, NSA_WIDTH), F32),
        scratch_shapes=[pltpu.VMEM((2, N_KV, GROUP_ROWS, 2 * LANE), BF16)]
                       + [pltpu.VMEM((N_KV, GROUP_ROWS, LANE), F32)] * 4,
        compiler_params=_params(("arbitrary", "arbitrary")),
        name="attn_prompt",
    )(q, bias_lanes, cmp, kaug, vaug, kwin, vwaug, gl, bg, t_cmp, t_near, t_win)


def _prompt_kv_operands(kvtb, emask):
    nb, _, nt = kvtb.shape
    half = lambda kind, g: kvtb[:, kind * KV_WIDTH + g * HEAD_DIM:kind * KV_WIDTH + (g + 1) * HEAD_DIM]
    ones = jnp.ones((nb, HEAD_DIM, nt), BF16)
    extra = ones.at[:, 2:].set(0)
    e = jnp.broadcast_to(emask[None], (nb, LANE, nt))
    kaug = jnp.stack([jnp.concatenate([half(2, 0), extra, e], axis=1),
                      jnp.concatenate([extra, half(2, 1), e], axis=1)], axis=1)
    aug = lambda kind: jnp.stack([jnp.concatenate([half(kind, 0), ones], axis=1),
                                  jnp.concatenate([ones, half(kind, 1)], axis=1)], axis=1)
    return kaug, aug(3), kvtb[:, 4 * KV_WIDTH:5 * KV_WIDTH], aug(5)


def _far_bias_lanes(rel_bias):
    b = rel_bias[REL_BUCKETS - 1]
    hi = b.astype(BF16).astype(F32)
    pair = jnp.stack([hi, b - hi], axis=1)
    tile = jnp.zeros((N_HEADS, LANE), F32)
    first = tile.at[:, HEAD_DIM:HEAD_DIM + 2].set(pair)
    second = tile.at[:, 0:2].set(pair)
    return jnp.where((np.arange(N_HEADS) < HPG)[:, None], first, second).reshape(1, QPAD).astype(BF16)


def _silu(x):
    return x * jax.nn.sigmoid(x)


def _post_a_body(o_ref, z_ref, u_ref, vn_ref, zg_ref, x_ref, wsp_ref, bsp_ref, wo_ref, y_ref, *, short_chunks):
    mix = []
    if short_chunks:
        vn = vn_ref[0]
        for g in range(GMLP_GROUPS):
            vg = vn[:, g * LANE:(g + 1) * LANE]
            acc = bsp_ref[:, g:g + 1] + wsp_ref[0][:, g:g + 1] * vg
            for k in range(1, wsp_ref.shape[0]):
                acc = acc + wsp_ref[k][:, g:g + 1] * pltpu.roll(vg, k, axis=0)
            mix.append(acc)
    else:
        chunk = wsp_ref.shape[1]
        r = lax.broadcasted_iota(jnp.int32, (chunk, chunk), 0)
        c = lax.broadcasted_iota(jnp.int32, (chunk, chunk), 1)
        vn = vn_ref[0].astype(BF16)
        for g in range(GMLP_GROUPS):
            wm = jnp.where(c <= r, wsp_ref[g], 0.0).astype(BF16)
            mix.append(jnp.concatenate(
                [_mm(wm, vn[k * chunk:(k + 1) * chunk, g * LANE:(g + 1) * LANE]) + bsp_ref[:, g:g + 1]
                 for k in range(vn.shape[0] // chunk)], axis=0))
    y_g = u_ref[0] * jnp.concatenate(mix, axis=-1) * _silu(zg_ref[0])
    y_nsa = o_ref[0] * _silu(z_ref[0])
    cat = jnp.concatenate([y_nsa, y_g], axis=-1).astype(BF16)
    y_ref[0] = x_ref[0] + _mm(cat, wo_ref[...])


def _post_a(o, z, u, vn, zg, x, wsp, bsp, wo, short_chunks=False, chunks_per_step=1):
    nb, nt, _ = x.shape
    rows = wsp.shape[1] * chunks_per_step
    row = lambda width: pl.BlockSpec((1, rows, width), lambda b, i: (b, i, 0))
    return pl.pallas_call(
        functools.partial(_post_a_body, short_chunks=short_chunks),
        grid=(nb, nt // rows),
        in_specs=[row(NSA_WIDTH), row(NSA_WIDTH), row(GMLP_WIDTH), row(GMLP_WIDTH), row(GMLP_WIDTH),
                  row(D_MODEL), _const_spec(wsp.shape), _const_spec(bsp.shape), _const_spec(wo.shape)],
        out_specs=row(D_MODEL),
        out_shape=jax.ShapeDtypeStruct(x.shape, F32),
        compiler_params=_params(("arbitrary", "arbitrary")),
        name="post_a",
    )(o, z, u, vn, zg, x, wsp, bsp, wo)


SAMPLE_ROWS = 8
N_PAGES_MAX = 16


SEQS_PER_STEP = 2


def _stack_heads(q_ref, s):
    return jnp.concatenate([q_ref[s, :, h * LANE:(h + 1) * LANE] for h in range(N_HEADS)], axis=0).astype(BF16)


def _pad_rows(x, rows):
    return jnp.concatenate([x, jnp.zeros((rows - x.shape[0], x.shape[1]), x.dtype)], axis=0)


def _cmp_sample_body(pt_ref, q_ref, kv_ref, pe_ref, wk_ref, wv_ref, tc_ref, pool_ref, *rest):
    n_pages = (len(rest) - 2) // SEQS_PER_STEP
    oc_ref, ps_ref = rest[-2:]
    for s in range(SEQS_PER_STEP):
        _cmp_sample_one(s, rest[s * n_pages:(s + 1) * n_pages], q_ref, kv_ref, pe_ref, wk_ref, wv_ref, tc_ref,
                        pool_ref, oc_ref, ps_ref)


def _cmp_sample_one(s, pages, q_ref, kv_ref, pe_ref, wk_ref, wv_ref, tc_ref, pool_ref, oc_ref, ps_ref):
    n_pages = len(pages)
    lane = lax.broadcasted_iota(jnp.int32, (KV_WIDTH, LANE), 1)
    eye = lax.broadcasted_iota(jnp.int32, (KV_WIDTH, LANE), 0) == lane
    qs = _stack_heads(q_ref, s)
    summ = []
    for kind in range(2):
        sums = jnp.zeros((KV_WIDTH, LANE), F32)
        for p in range(0, n_pages, 2):
            x = jnp.concatenate([pages[p][kind].reshape(KV_WIDTH, PAGE), pages[p + 1][kind].reshape(KV_WIDTH, PAGE)],
                                axis=1)
            hi, lo = _split_bf16(x)
            pool = pool_ref[p * PAGE:(p + 2) * PAGE]
            sums = sums + _mm(hi, pool) + _mm(lo, pool)
        new = jnp.sum(kv_ref[s, :, kind * KV_WIDTH:(kind + 1) * KV_WIDTH], axis=0, keepdims=True)
        new_col = jnp.sum(jnp.where(eye, new, 0.0), axis=1, keepdims=True)
        sums = jnp.where(lane == 2 * n_pages, new_col, sums)
        mean_t = sums * (1.0 / CMP_BLOCK) + jnp.mean(pe_ref[kind], axis=1, keepdims=True)
        w_ref = wk_ref if kind == 0 else wv_ref
        summ.append(_mm(w_ref[...], mean_t.astype(BF16)).astype(BF16))
    tmpl = tc_ref[...]
    ok = tmpl > 0.5 * NEG
    p = jnp.where(ok, _softmax_rows(_mm(qs, summ[0]) + tmpl), 0.0)
    oc_ref[s] = _nt(p.astype(BF16), summ[1])
    ps_ref[s] = p.reshape(N_KV, HPG, SAMPLE_ROWS, LANE).sum(axis=1).reshape(N_KV * SAMPLE_ROWS, LANE)


def _page_specs(layer, n_pages, kind_block):
    def spec(s, p):
        return pl.BlockSpec((None, None, 2, N_KV, HEAD_DIM, PAGE),
                            lambda b, pt: (layer, pt[(b * SEQS_PER_STEP + s) * n_pages + p], kind_block, 0, 0, 0))
    return [spec(s, p) for s in range(SEQS_PER_STEP) for p in range(n_pages)]


def _cmp_sample(layer, pt_flat, n_pages, cache_t, q, kv, pe_t, wk_t, wv_t, t_cmp):
    nseq = q.shape[0]
    rows = N_HEADS * SAMPLE_ROWS
    seq = lambda r, w: pl.BlockSpec((SEQS_PER_STEP, r, w), lambda b, pt: (b, 0, 0))
    cst = lambda a: pl.BlockSpec(a.shape, lambda b, pt: (0,) * a.ndim)
    assert n_pages % 2 == 0 and nseq % SEQS_PER_STEP == 0
    pool = jnp.asarray((np.arange(n_pages * PAGE) // CMP_BLOCK)[:, None] == np.arange(LANE)[None, :], BF16)
    return pl.pallas_call(
        _cmp_sample_body,
        grid_spec=pltpu.PrefetchScalarGridSpec(
            num_scalar_prefetch=1, grid=(nseq // SEQS_PER_STEP,),
            in_specs=[seq(SAMPLE_ROWS, QPAD), seq(SAMPLE_ROWS, 6 * KV_WIDTH), cst(pe_t), cst(wk_t), cst(wv_t),
                      cst(t_cmp), cst(pool)] + _page_specs(layer, n_pages, 0),
            out_specs=[seq(rows, KV_WIDTH), seq(N_KV * SAMPLE_ROWS, LANE)]),
        out_shape=[jax.ShapeDtypeStruct((nseq, rows, KV_WIDTH), F32),
                   jax.ShapeDtypeStruct((nseq, N_KV * SAMPLE_ROWS, LANE), F32)],
        compiler_params=_params(("arbitrary",)),
        name="cmp_sample",
    )(pt_flat, q, kv, pe_t, wk_t, wv_t, t_cmp, pool, *([cache_t] * (SEQS_PER_STEP * n_pages)))


def _select_sample_body(ps_ref, o_ref, *, past_len):
    cols = ps_ref.shape[1]
    t = lax.broadcasted_iota(jnp.int32, (1, cols), 1) % SAMPLE_ROWS
    o_ref[...] = _select_blocks(ps_ref[...], (past_len + t) // CMP_BLOCK)


def _select_sample(ps_t, past_len):
    cols = ps_t.shape[1]
    tile = min(cols, 2 * LANE)
    spec = pl.BlockSpec((LANE, tile), lambda i: (0, i))
    return pl.pallas_call(
        functools.partial(_select_sample_body, past_len=past_len),
        grid=(cols // tile,), in_specs=[spec], out_specs=spec,
        out_shape=jax.ShapeDtypeStruct(ps_t.shape, F32),
        compiler_params=_params(("arbitrary",)),
        name="select_sample",
    )(ps_t)


def _attn_sample_body(pt_ref, q_ref, kv_ref, un_ref, oc_ref, gl_ref, bg_ref, e_ref, ts_ref, tn_ref, tw_ref,
                      win_ref, *rest):
    n_pages = (len(rest) - 1) // SEQS_PER_STEP
    for s in range(SEQS_PER_STEP):
        _attn_sample_one(s, rest[s * n_pages:(s + 1) * n_pages], q_ref, kv_ref, un_ref, oc_ref, gl_ref, bg_ref, e_ref,
                         ts_ref, tn_ref, tw_ref, win_ref, rest[-1])


def _attn_sample_one(s, pages, q_ref, kv_ref, un_ref, oc_ref, gl_ref, bg_ref, e_ref, ts_ref, tn_ref, tw_ref,
                     win_ref, o_ref):
    qs = _stack_heads(q_ref, s)
    lane = lax.broadcasted_iota(jnp.int32, (SAMPLE_ROWS, LANE), 1)
    tn = tn_ref[...]

    def new_rows(col):
        return _pad_rows(kv_ref[s, :, col * KV_WIDTH:(col + 1) * KV_WIDTH], LANE).astype(BF16)

    def branch(k_past, v_past, bias, k_col, v_col):
        s = _mm(qs, k_past) + bias
        s_new = _nt(qs, new_rows(k_col)) + tn
        m = jnp.maximum(jnp.max(s, axis=-1, keepdims=True), jnp.max(s_new, axis=-1, keepdims=True))
        p, p_new = jnp.exp(s - m), jnp.exp(s_new - m)
        den = jnp.sum(p, axis=-1, keepdims=True) + jnp.sum(p_new, axis=-1, keepdims=True)
        return (_nt(p.astype(BF16), v_past) + _mm(p_new.astype(BF16), new_rows(v_col))) / den

    cat = lambda kind: jnp.concatenate([pg[kind].reshape(KV_WIDTH, PAGE) for pg in pages], axis=1).astype(BF16)
    madd = _mm(un_ref[s].astype(BF16), e_ref[...])
    madd = jnp.concatenate([madd[g * SAMPLE_ROWS:(g + 1) * SAMPLE_ROWS] for g in range(N_KV) for _ in range(HPG)],
                           axis=0)
    o_sel = branch(cat(0), cat(1), ts_ref[...] + madd, 2, 3)
    win = lambda kind: win_ref[s, kind].reshape(KV_WIDTH, win_ref.shape[-1]).astype(BF16)
    o_win = branch(win(0), win(1), tw_ref[...], 4, 5)

    gate = jax.nn.sigmoid(gl_ref[s] + bg_ref[...])
    oc = oc_ref[s]

    def merged(h):
        rows = slice(h * SAMPLE_ROWS, (h + 1) * SAMPLE_ROWS)
        col = lambda k: gate[:, k * N_HEADS + h:k * N_HEADS + h + 1]
        return col(0) * oc[rows] + col(1) * o_sel[rows] + col(2) * o_win[rows]

    for c in range(HPG):
        o_ref[s, :, c * LANE:(c + 1) * LANE] = jnp.where(lane < HEAD_DIM, merged(c), merged(HPG + c))


def _attn_sample(layer, pt_flat, n_pages, cache_t, win_t, q, kv, unsel, oc, gl, bg, emask, t_sel, t_new, t_win):
    nseq = q.shape[0]
    seq = lambda a: pl.BlockSpec((SEQS_PER_STEP,) + a.shape[1:], lambda b, pt: (b,) + (0,) * (a.ndim - 1))
    cst = lambda a: pl.BlockSpec(a.shape, lambda b, pt: (0,) * a.ndim)
    win_spec = pl.BlockSpec((None, SEQS_PER_STEP, 2, N_KV, HEAD_DIM, win_t.shape[-1]),
                            lambda b, pt: (layer, b, 0, 0, 0, 0))
    return pl.pallas_call(
        _attn_sample_body,
        grid_spec=pltpu.PrefetchScalarGridSpec(
            num_scalar_prefetch=1, grid=(nseq // SEQS_PER_STEP,),
            in_specs=[seq(q), seq(kv), seq(unsel), seq(oc), seq(gl), cst(bg), cst(emask), cst(t_sel), cst(t_new),
                      cst(t_win), win_spec] + _page_specs(layer, n_pages, 1),
            out_specs=pl.BlockSpec((SEQS_PER_STEP, SAMPLE_ROWS, NSA_WIDTH), lambda b, pt: (b, 0, 0))),
        out_shape=jax.ShapeDtypeStruct((nseq, SAMPLE_ROWS, NSA_WIDTH), F32),
        compiler_params=_params(("arbitrary",)),
        name="attn_sample",
    )(pt_flat, q, kv, unsel, oc, gl, bg, emask, t_sel, t_new, t_win, win_t,
      *([cache_t] * (SEQS_PER_STEP * n_pages)))


def _proj_c_body(x_ref, g_ref, w_ref, xb_ref, z_ref):
    y = _mm(_rms(x_ref[...], g_ref[...]).astype(BF16), w_ref[...])
    xb_ref[...] = y[:, :LRU_WIDTH]
    z_ref[...] = y[:, LRU_WIDTH:]


def _proj_c(x, norm_g, w, tm):
    n = x.shape[0]
    row = pl.BlockSpec((tm, LRU_WIDTH), lambda i: (i, 0))
    return pl.pallas_call(
        _proj_c_body,
        grid=(n // tm,),
        in_specs=[row, _const_spec((1, D_MODEL)), _const_spec((D_MODEL, 2 * LRU_WIDTH))],
        out_specs=[row, row],
        out_shape=[jax.ShapeDtypeStruct((n, LRU_WIDTH), F32)] * 2,
        compiler_params=_params(("arbitrary",)),
        name="proj_c",
    )(x, norm_g, w)


def _lru_coeffs(xc, wra_ref, bra_ref, wrx_ref, brx_ref, lam_ref):
    xb = xc.astype(BF16)
    blocks = lambda w_ref: jnp.concatenate(
        [_mm(xb[:, k * LRU_BLOCK:(k + 1) * LRU_BLOCK], w_ref[k]) for k in range(LRU_HEADS)], axis=-1)
    r = jax.nn.sigmoid(blocks(wra_ref) + bra_ref[...])
    ig = jax.nn.sigmoid(blocks(wrx_ref) + brx_ref[...])
    log_a = -LRU_C * r * jax.nn.softplus(-lam_ref[...])
    return jnp.exp(log_a), jnp.sqrt(-_expm1(2.0 * log_a)) * ig * xc


def _expm1(x):
    u = jnp.exp(x)
    small = x > -1.0
    us = jnp.where(small & (u != 1.0), u, 0.5)
    return jnp.where(small, jnp.where(u == 1.0, x, (us - 1.0) * x / jnp.log(us)), u - 1.0)


def _mix_c_prompt_body(x_ref, ng_ref, wi_ref, cw_ref, cb_ref, wra_ref, bra_ref, wrx_ref, brx_ref, lam_ref, wo_ref,
                       fg_ref, y_ref, hl_ref, tail_ref, xin_scr, a_scr, b_scr, z_scr, h_scr, *, final_norm):
    tt = x_ref.shape[1]
    pad = 8
    hist = CONV_WIDTH - 1

    @pl.when(pl.program_id(1) == 0)
    def _():
        xin_scr[0:pad] = jnp.zeros((pad, LRU_WIDTH), F32)
        h_scr[...] = jnp.zeros(h_scr.shape, F32)

    proj = _mm(_rms(x_ref[0], ng_ref[...]).astype(BF16), wi_ref[...])
    xin_scr[pad:pad + tt] = proj[:, :LRU_WIDTH]
    z_scr[...] = proj[:, LRU_WIDTH:]
    xc = cb_ref[...] + xin_scr[pad - hist:pad - hist + tt] * cw_ref[0:1]
    for k in range(1, CONV_WIDTH):
        xc = xc + xin_scr[pad - hist + k:pad - hist + k + tt] * cw_ref[k:k + 1]
    a, b = _lru_coeffs(xc, wra_ref, bra_ref, wrx_ref, brx_ref, lam_ref)
    a_scr[...] = a
    b_scr[...] = b

    row = lax.broadcasted_iota(jnp.int32, (8, LRU_WIDTH), 0)

    def step(k, h_prev):
        r0 = pl.multiple_of(k * 8, 8)
        ca, cbv = a_scr[pl.ds(r0, 8)], b_scr[pl.ds(r0, 8)]
        for s in (1, 2, 4):
            keep = row >= s
            cbv = cbv + ca * jnp.where(keep, pltpu.roll(cbv, s, axis=0), 0.0)
            ca = ca * jnp.where(keep, pltpu.roll(ca, s, axis=0), 1.0)
        hs = ca * h_prev + cbv
        b_scr[pl.ds(r0, 8)] = hs
        return jnp.broadcast_to(hs[7:8], (8, LRU_WIDTH))

    h_scr[...] = lax.fori_loop(0, tt // 8, step, h_scr[...])
    y = (b_scr[...] * _silu(z_scr[...])).astype(BF16)
    out = x_ref[0] + _mm(y, wo_ref[...])
    y_ref[0] = _rms(out, fg_ref[...]) if final_norm else out
    hl_ref[0] = h_scr[0:1]
    tail_ref[0] = xin_scr[pad + tt - hist:pad + tt]
    xin_scr[0:pad] = xin_scr[tt:tt + pad]


def _mix_c_prompt(x, ng, wi, cw, cb, wra, bra, wrx, brx, lam, wo, fg, final_norm, tt=512):
    nb, nt, _ = x.shape
    row = pl.BlockSpec((1, tt, LRU_WIDTH), lambda b, i: (b, i, 0))
    cst = lambda a: pl.BlockSpec(a.shape, lambda b, i: (0,) * a.ndim, pipeline_mode=pl.Buffered(1))
    hist = CONV_WIDTH - 1
    return pl.pallas_call(
        functools.partial(_mix_c_prompt_body, final_norm=final_norm),
        grid=(nb, nt // tt),
        in_specs=[row, cst(ng), cst(wi), cst(cw), cst(cb), cst(wra), cst(bra), cst(wrx), cst(brx), cst(lam), cst(wo),
                  cst(fg)],
        out_specs=[row, pl.BlockSpec((1, 1, LRU_WIDTH), lambda b, i: (b, 0, 0)),
                   pl.BlockSpec((1, hist, LRU_WIDTH), lambda b, i: (b, 0, 0))],
        out_shape=[jax.ShapeDtypeStruct(x.shape, F32), jax.ShapeDtypeStruct((nb, 1, LRU_WIDTH), F32),
                   jax.ShapeDtypeStruct((nb, hist, LRU_WIDTH), F32)],
        scratch_shapes=[pltpu.VMEM((tt + 8, LRU_WIDTH), F32)] + [pltpu.VMEM((tt, LRU_WIDTH), F32)] * 3
                       + [pltpu.VMEM((8, LRU_WIDTH), F32)],
        compiler_params=_params(("arbitrary", "arbitrary")),
        name="mix_c_prompt",
    )(x, ng, wi, cw, cb, wra, bra, wrx, brx, lam, wo, fg)


def _mix_c_sample_body(xb_ref, z_ref, x_ref, h0_ref, c0_ref, cw_ref, cb_ref, wra_ref, bra_ref, wrx_ref, brx_ref,
                       lam_ref, wo_ref, fg_ref, y_ref, hl_ref, tail_ref, *, final_norm):
    nt = xb_ref.shape[0]
    hist = CONV_WIDTH - 1
    xin = [c0_ref[k] for k in range(hist)] + [xb_ref[t] for t in range(nt)]
    h = h0_ref[...]
    for t in range(nt):
        xc = cb_ref[...] + xin[t] * cw_ref[0:1]
        for k in range(1, CONV_WIDTH):
            xc = xc + xin[t + k] * cw_ref[k:k + 1]
        a, b = _lru_coeffs(xc, wra_ref, bra_ref, wrx_ref, brx_ref, lam_ref)
        h = a * h + b
        out = x_ref[t] + _mm((h * _silu(z_ref[t])).astype(BF16), wo_ref[...])
        y_ref[t] = _rms(out, fg_ref[...]) if final_norm else out
    hl_ref[...] = h
    for k in range(hist):
        tail_ref[k] = xin[nt + k]


def _mix_c_sample(xb, z, x, h0, c0, cw, cb, wra, bra, wrx, brx, lam, wo, fg, final_norm):
    args = (xb, z, x, h0, c0, cw, cb, wra, bra, wrx, brx, lam, wo, fg)
    return pl.pallas_call(
        functools.partial(_mix_c_sample_body, final_norm=final_norm),
        grid=(1,),
        in_specs=[_const_spec(a.shape) for a in args],
        out_specs=[_const_spec(x.shape), _const_spec(h0.shape), _const_spec(c0.shape)],
        out_shape=[jax.ShapeDtypeStruct(x.shape, F32), jax.ShapeDtypeStruct(h0.shape, F32),
                   jax.ShapeDtypeStruct(c0.shape, F32)],
        compiler_params=_params(("arbitrary",)),
        name="mix_c_sample",
    )(*args)


def _pair_perm():
    cols = []
    for c in range(HPG):
        cols += list(range(c * HEAD_DIM, (c + 1) * HEAD_DIM))
        cols += list(range((HPG + c) * HEAD_DIM, (HPG + c + 1) * HEAD_DIM))
    return np.asarray(cols)


def _layout_w_in_a(w):
    o_kv = NSA_WIDTH
    o_g = o_kv + 6 * KV_WIDTH
    o_z = o_g + 3 * N_HEADS
    o_rest = o_z + NSA_WIDTH
    wq = w[:, :NSA_WIDTH].reshape(D_MODEL, N_HEADS, HEAD_DIM)
    zeros = jnp.zeros_like(wq)
    first = (np.arange(N_HEADS) < HPG)[None, :, None]
    wq = jnp.concatenate([jnp.where(first, wq, zeros), jnp.where(first, zeros, wq)], axis=-1)
    wg = jnp.pad(w[:, o_g:o_z], ((0, 0), (0, LANE - 3 * N_HEADS)))
    wz = w[:, o_z:o_rest][:, _pair_perm()]
    return jnp.concatenate([wq.reshape(D_MODEL, QPAD), w[:, o_kv:o_g], wz, w[:, o_rest:], wg], axis=1).astype(BF16)


def _block_diag(w, n):
    return jnp.kron(jnp.eye(n, dtype=w.dtype), w)


def _even_layer(a, yp, ys, cache_t, win_t, pt_flat, n_pages, tmpl, p):
    nb, nt, _ = yp.shape
    nseq, dec_t, _ = ys.shape
    past_len = n_pages * PAGE
    w_in = _layout_w_in_a(p['w_in_a'][a])
    norm_g = p['norm_a'][a][None]
    ln_g, ln_b = p['gmlp_ln_g'][a][None], p['gmlp_ln_b'][a][None]
    bg = jnp.pad(p['b_gate_a'][a], (0, LANE - 3 * N_HEADS))[None]
    w_out = jnp.concatenate([p['w_out_a'][a][:NSA_WIDTH][_pair_perm()], p['w_out_a'][a][NSA_WIDTH:]], axis=0).astype(BF16)
    pe, wc = p['pe_cmp'][a], p['w_cmp'][a]

    q, nsa_t, wrows_t, kvtb, gl, z, u, vn, zg = _proj_a(yp, norm_g, w_in, ln_g, ln_b, tm=256, feature_major=True)
    pe_rows = jnp.concatenate([pe[0], pe[0], pe[1], pe[1]], axis=1)
    w_bd = jnp.zeros((2 * KV_WIDTH, 2 * KV_WIDTH), F32)
    for k in range(4):
        w_bd = w_bd.at[k * HEAD_DIM:(k + 1) * HEAD_DIM, k * HEAD_DIM:(k + 1) * HEAD_DIM].set(wc[k // 2])
    cmp = _cmp_prompt(nsa_t, pe_rows, w_bd.astype(BF16))
    kaug, vaug, kwin, vwaug = _prompt_kv_operands(kvtb, tmpl['e_prompt'])
    o = _attn_prompt(q, tmpl['far_lanes'], cmp, kaug, vaug, kwin, vwaug, gl, bg, tmpl['cmp'], tmpl['near'], tmpl['win'])
    yp_new = _post_a(o, z, u, vn, zg, yp, p['w_spatial'][a], p['b_spatial'][a].T, w_out,
                     chunks_per_step=min(4, nt // GMLP_CHUNK))
    rows_p = jnp.transpose(nsa_t.reshape(nb, 4, N_KV, HEAD_DIM, nt), (0, 4, 1, 2, 3))
    wlen = min(WINDOW, nt)
    win_p = jnp.transpose(wrows_t[:, :, nt - wlen:].reshape(nb, 2, N_KV, HEAD_DIM, wlen), (0, 4, 1, 2, 3))

    xs = ys.reshape(1, nseq * dec_t, D_MODEL)
    q, kv, gl, z, u, vn, zg = _proj_a(xs, norm_g, w_in, ln_g, ln_b, tm=nseq * dec_t, feature_major=False)
    pad_t = lambda x: jnp.pad(x.reshape(nseq, dec_t, -1), ((0, 0), (0, SAMPLE_ROWS - dec_t), (0, 0)))
    q8, kv8, gl8 = pad_t(q), pad_t(kv), pad_t(gl)
    pe_t = jnp.concatenate([jnp.swapaxes(pe, 1, 2)] * N_KV, axis=1)
    wk_t = _block_diag(wc[0].T, N_KV).astype(BF16)
    wv_t = _block_diag(wc[1].T, N_KV).astype(BF16)
    oc, ps = _cmp_sample(a, pt_flat, n_pages, cache_t, q8, kv8, pe_t, wk_t, wv_t, tmpl['s_cmp'])
    unsel_t = _select_sample(ps.reshape(nseq * N_KV * SAMPLE_ROWS, LANE).T, past_len)
    unsel = unsel_t.T.reshape(nseq, N_KV * SAMPLE_ROWS, LANE)
    o8 = _attn_sample(a, pt_flat, n_pages, cache_t, win_t, q8, kv8, unsel, oc, gl8, bg, tmpl['e_sample'],
                      tmpl['s_sel'], tmpl['s_new'], tmpl['s_win'])
    o = o8[:, :dec_t].reshape(1, nseq * dec_t, NSA_WIDTH)
    w_small = p['w_spatial'][a][:, :dec_t, :dec_t]
    above = lambda k: jnp.pad(jnp.diagonal(w_small, offset=-k, axis1=1, axis2=2), ((0, 0), (k, 0))).T
    w_sp = jnp.stack([jnp.tile(above(k), (nseq, 1)) for k in range(dec_t)])
    b_sp = jnp.tile(p['b_spatial'][a][:, :dec_t].T, (nseq, 1))
    ys_new = _post_a(o, z, u, vn, zg, xs, w_sp, b_sp, w_out, short_chunks=True).reshape(nseq, dec_t, D_MODEL)
    rows_s = kv[0, :, :4 * KV_WIDTH].reshape(nseq, dec_t, 4, N_KV, HEAD_DIM)
    win_s = kv[0, :, 4 * KV_WIDTH:].reshape(nseq, dec_t, 2, N_KV, HEAD_DIM)
    return yp_new, ys_new, rows_p, rows_s, win_p, win_s, vn.reshape(nseq, dec_t, GMLP_WIDTH)


def _odd_layer(c, yp, ys, h0, conv0, p, final):
    nb, nt, _ = yp.shape
    nseq, dec_t, _ = ys.shape
    norm_g = p['norm_c'][c][None]
    w_in = p['w_in_c'][c].astype(BF16)
    consts = (p['conv_w'][c], p['conv_b'][c][None], p['w_rg_a'][c].astype(BF16), p['b_rg_a'][c][None],
              p['w_rg_x'][c].astype(BF16), p['b_rg_x'][c][None], p['lru_lambda'][c][None],
              p['w_out_c'][c].astype(BF16), p['final_norm'][None])
    yp_new, h_p, tail_p = _mix_c_prompt(yp, norm_g, w_in, *consts, final_norm=final)
    xs = jnp.swapaxes(ys, 0, 1)
    xb, z = _proj_c(xs.reshape(dec_t * nseq, D_MODEL), norm_g, w_in, tm=dec_t * nseq)
    tm3 = lambda x: x.reshape(dec_t, nseq, -1)
    ys_new, h_s, tail_s = _mix_c_sample(tm3(xb), tm3(z), xs, h0, jnp.swapaxes(conv0, 0, 1), *consts,
                                        final_norm=final)
    return yp_new, jnp.swapaxes(ys_new, 0, 1), h_p[:, 0], h_s, tail_p, jnp.swapaxes(tail_s, 0, 1)


def kernel(x_prompt, x_sample, cache_nsa, cache_win, state_lru_h, state_lru_conv, page_table, norm_a, w_in_a, b_gate_a, pe_cmp, w_cmp, gmlp_ln_g, gmlp_ln_b, w_spatial, b_spatial, w_out_a, norm_c, w_in_c, conv_w, conv_b, w_rg_a, b_rg_a, w_rg_x, b_rg_x, lru_lambda, w_out_c, rel_bias, final_norm):
    p = dict(norm_a=norm_a, w_in_a=w_in_a, b_gate_a=b_gate_a, pe_cmp=pe_cmp, w_cmp=w_cmp, gmlp_ln_g=gmlp_ln_g,
             gmlp_ln_b=gmlp_ln_b, w_spatial=w_spatial, b_spatial=b_spatial, w_out_a=w_out_a, norm_c=norm_c,
             w_in_c=w_in_c, conv_w=conv_w, conv_b=conv_b, w_rg_a=w_rg_a, b_rg_a=b_rg_a, w_rg_x=w_rg_x,
             b_rg_x=b_rg_x, lru_lambda=lru_lambda, w_out_c=w_out_c, rel_bias=rel_bias, final_norm=final_norm)
    nb, nt, _ = x_prompt.shape
    nseq, n_pages = page_table.shape
    past_len = n_pages * PAGE
    wlen = cache_win.shape[2]
    depth = norm_a.shape[0] + norm_c.shape[0]
    assert nt % KV_TILE == 0 and nt >= WIN_KEYS and nt // CMP_BLOCK <= LANE
    assert past_len // CMP_BLOCK < LANE and x_sample.shape[1] <= SAMPLE_ROWS and n_pages <= N_PAGES_MAX

    cache_t = jnp.transpose(cache_nsa, (0, 1, 3, 4, 5, 2))
    win_t = jnp.transpose(cache_win, (0, 1, 3, 4, 5, 2))
    pt_flat = page_table.reshape(-1).astype(jnp.int32)

    blk_of_key = lambda n: (np.arange(n) // CMP_BLOCK)[None, :] == np.arange(LANE)[:, None]
    rows_s = SAMPLE_ROWS
    stack = lambda t: t.reshape(N_HEADS * rows_s, t.shape[-1])
    by_group = lambda t: t.reshape(N_KV, GROUP_ROWS, t.shape[-1])
    assert NEAR_SPAN + LANE - (KV_TILE - 1) >= _UPPER[-1] and nt % KV_TILE == 0
    tmpl = dict(
        far_lanes=_far_bias_lanes(rel_bias),
        e_prompt=jnp.asarray(np.where(blk_of_key(nt), NEG, 0.0), BF16),
        e_sample=jnp.asarray(np.where(blk_of_key(past_len), NEG, 0.0), BF16),
        cmp=_bias_template(rel_bias, Q_BLOCK, LANE, -CMP_BLOCK, CMP_BLOCK * (LANE - 2) - (CMP_BLOCK - 1)),
        near=by_group(_bias_template(rel_bias, Q_BLOCK, NEAR_SPAN + KV_TILE, -1, NEAR_SPAN)),
        win=by_group(_bias_template(rel_bias, Q_BLOCK, WINDOW + WIN_KEYS, -1, WINDOW, hi=WINDOW)),
        s_cmp=stack(_bias_template(rel_bias, rows_s, LANE, -CMP_BLOCK, past_len - (CMP_BLOCK - 1))),
        s_sel=stack(_bias_template(rel_bias, rows_s, past_len, -1, past_len)),
        s_new=stack(_bias_template(rel_bias, rows_s, LANE, -1, 0, cmax=x_sample.shape[1])),
        s_win=stack(_bias_template(rel_bias, rows_s, wlen, -1, wlen, hi=WINDOW)),
    )

    yp, ys = x_prompt, x_sample
    outs = [[] for _ in range(9)]
    for layer in range(depth):
        if layer % 2 == 0:
            yp, ys, *leaves = _even_layer(layer // 2, yp, ys, cache_t, win_t, pt_flat, n_pages, tmpl, p)
            for dst, leaf in zip(outs[:5], leaves):
                dst.append(leaf)
        else:
            c = layer // 2
            yp, ys, *leaves = _odd_layer(c, yp, ys, state_lru_h[c], state_lru_conv[c], p, final=layer == depth - 1)
            for dst, leaf in zip(outs[5:], leaves):
                dst.append(leaf)
    return (yp, ys) + tuple(jnp.stack(o) for o in outs)
```

```python
import functools
import math

import numpy as np
import jax
import jax.numpy as jnp
from jax import lax
from jax.experimental import pallas as pl
from jax.experimental.pallas import tpu as pltpu

F32 = jnp.float32
BF16 = jnp.bfloat16

D_MODEL = 1024
N_HEADS = 8
N_KV = 2
HPG = N_HEADS // N_KV
HEAD_DIM = 64
NSA_WIDTH = N_HEADS * HEAD_DIM
KV_WIDTH = N_KV * HEAD_DIM
CMP_BLOCK = 64
N_SEL = 16
WINDOW = 512
GMLP_GROUPS = 4
GMLP_WIDTH = 512
GMLP_CHUNK = 128
LRU_WIDTH = 1024
LRU_HEADS = 8
LRU_BLOCK = 128
CONV_WIDTH = 4
LRU_C = 8.0
REL_BUCKETS = 32
REL_MAX_DIST = 1024
EPS = 1e-6
NEG = -1e30
PAGE = 128

LANE = 128
Q_BLOCK = 128
KV_TILE = 512
NEAR_SPAN = 1280
WIN_KEYS = WINDOW + Q_BLOCK
VMEM_LIMIT = 56 * 1024 * 1024

QPAD = N_HEADS * LANE
C_KV = QPAD
C_Z = C_KV + 6 * KV_WIDTH
C_U = C_Z + NSA_WIDTH
C_V = C_U + GMLP_WIDTH
C_ZG = C_V + GMLP_WIDTH
C_G = C_ZG + GMLP_WIDTH
PROJ_COLS = C_G + LANE


def _nt(a, b):
    return lax.dot_general(a, b, (((1,), (1,)), ((), ())), preferred_element_type=F32)


def _mm(a, b):
    return jnp.dot(a, b, preferred_element_type=F32)


def _params(sem):
    return pltpu.CompilerParams(dimension_semantics=sem, vmem_limit_bytes=VMEM_LIMIT)


def _const_spec(shape):
    n = len(shape)
    return pl.BlockSpec(shape, lambda *_: (0,) * n)


def _bucket_upper_bounds():
    max_exact = REL_BUCKETS // 2
    d = np.arange(0, 4 * REL_MAX_DIST, dtype=np.int64)

    def buckets(ft):
        df = np.maximum(d, 1).astype(ft)
        large = max_exact + (np.log(df / ft(max_exact)) / ft(math.log(REL_MAX_DIST / max_exact))
                             * ft(REL_BUCKETS - max_exact)).astype(np.int32)
        return np.where(d < max_exact, d, np.minimum(large, REL_BUCKETS - 1))

    b32, b64 = buckets(np.float32), buckets(np.float64)
    assert (b32 == b64).all() and (np.diff(b32) >= 0).all() and b32[-1] == REL_BUCKETS - 1
    return [int(np.argmax(b32 > k)) for k in range(REL_BUCKETS - 1)]


_UPPER = _bucket_upper_bounds()


def _bias_tmpl_body(tab_ref, o_ref, *, rows, cs, off, hi, cmax):
    h = pl.program_id(0)
    t = lax.broadcasted_iota(jnp.int32, (rows, LANE), 0)
    lane = lax.broadcasted_iota(jnp.int32, (rows, LANE), 1)

    def chunk(k, carry):
        c0 = pl.multiple_of(k * LANE, LANE)
        c = lane + c0
        d = t + cs * c + off
        dd = jnp.maximum(d, 0)
        val = jnp.full((rows, LANE), tab_ref[REL_BUCKETS - 1, h], F32)
        for b in range(REL_BUCKETS - 2, -1, -1):
            val = jnp.where(dd < _UPPER[b], tab_ref[b, h], val)
        ok = (d >= 0) & (d <= hi) & (c < cmax)
        o_ref[0, :, pl.ds(c0, LANE)] = jnp.where(ok, val, NEG)
        return carry

    lax.fori_loop(0, o_ref.shape[2] // LANE, chunk, 0)


def _bias_template(rel_bias, rows, width, cs, off, hi=1 << 30, cmax=1 << 30):
    return pl.pallas_call(
        functools.partial(_bias_tmpl_body, rows=rows, cs=cs, off=off, hi=hi, cmax=cmax),
        grid=(N_HEADS,),
        in_specs=[pl.BlockSpec(memory_space=pltpu.SMEM)],
        out_specs=pl.BlockSpec((1, rows, width), lambda h: (h, 0, 0)),
        out_shape=jax.ShapeDtypeStruct((N_HEADS, rows, width), F32),
        compiler_params=_params(("arbitrary",)),
        name="bias_template",
    )(rel_bias)


def _rms(x, g):
    return x * lax.rsqrt(jnp.mean(x * x, axis=-1, keepdims=True) + EPS) * g


def _proj_a_body(x_ref, g_ref, w_ref, lng_ref, lnb_ref, q_ref, *rest, feature_major):
    xn = _rms(x_ref[0], g_ref[...])
    y = _mm(xn.astype(BF16), w_ref[...])
    q_ref[0] = (y[:, :QPAD] * (HEAD_DIM ** -0.5)).astype(q_ref.dtype)
    kv = y[:, C_KV:C_Z]
    if feature_major:
        nsa_ref, win_ref, kvtb_ref, gl_ref, z_ref, u_ref, vn_ref, zg_ref = rest
        kvt = kv.T
        nsa_ref[0] = kvt[:4 * KV_WIDTH]
        win_ref[0] = kvt[4 * KV_WIDTH:]
        kvtb_ref[0] = kvt.astype(BF16)
    else:
        kv_ref, gl_ref, z_ref, u_ref, vn_ref, zg_ref = rest
        kv_ref[0] = kv
    z_ref[0] = y[:, C_Z:C_U]
    u_ref[0] = y[:, C_U:C_V]
    v = y[:, C_V:C_ZG]
    mu = jnp.mean(v, axis=-1, keepdims=True)
    var = jnp.mean(jnp.square(v - mu), axis=-1, keepdims=True)
    vn_ref[0] = (v - mu) * lax.rsqrt(var + EPS) * lng_ref[...] + lnb_ref[...]
    zg_ref[0] = y[:, C_ZG:C_G]
    gl_ref[0] = y[:, C_G:]


def _proj_a(x, norm_g, w, ln_g, ln_b, tm, feature_major):
    nb, nt, _ = x.shape
    row = lambda width: pl.BlockSpec((1, tm, width), lambda b, i: (b, i, 0))
    colT = pl.BlockSpec((1, 6 * KV_WIDTH, tm), lambda b, i: (b, 0, i))
    sds = lambda shape, dt=F32: jax.ShapeDtypeStruct(shape, dt)
    if feature_major:
        col = lambda r: pl.BlockSpec((1, r, tm), lambda b, i: (b, 0, i))
        kv_specs = [col(4 * KV_WIDTH), col(2 * KV_WIDTH), colT]
        kv_shapes = [sds((nb, 4 * KV_WIDTH, nt)), sds((nb, 2 * KV_WIDTH, nt)), sds((nb, 6 * KV_WIDTH, nt), BF16)]
    else:
        kv_specs, kv_shapes = [row(6 * KV_WIDTH)], [sds((nb, nt, 6 * KV_WIDTH))]
    return pl.pallas_call(
        functools.partial(_proj_a_body, feature_major=feature_major),
        grid=(nb, nt // tm),
        in_specs=[row(D_MODEL), _const_spec((1, D_MODEL)), _const_spec((D_MODEL, PROJ_COLS)),
                  _const_spec((1, GMLP_WIDTH)), _const_spec((1, GMLP_WIDTH))],
        out_specs=[row(QPAD)] + kv_specs + [row(LANE), row(NSA_WIDTH), row(GMLP_WIDTH), row(GMLP_WIDTH),
                                            row(GMLP_WIDTH)],
        out_shape=[sds((nb, nt, QPAD), BF16 if feature_major else F32)] + kv_shapes
                  + [sds((nb, nt, LANE)), sds((nb, nt, NSA_WIDTH)), sds((nb, nt, GMLP_WIDTH)),
                     sds((nb, nt, GMLP_WIDTH)), sds((nb, nt, GMLP_WIDTH))],
        compiler_params=_params(("arbitrary", "arbitrary")),
        name="proj_a",
    )(x, norm_g, w, ln_g, ln_b)


def _split_bf16(x):
    hi = x.astype(BF16)
    return hi, (x - hi.astype(F32)).astype(BF16)


def _cmp_prompt_body(kvt_ref, pe_ref, w_ref, o_ref):
    nt = kvt_ref.shape[2]
    blk = lax.broadcasted_iota(jnp.int32, (LANE, nt), 0)
    pos = lax.broadcasted_iota(jnp.int32, (LANE, nt), 1)
    pool = jnp.where(pos // CMP_BLOCK == blk, 1.0 / CMP_BLOCK, 0.0).astype(BF16)
    hi, lo = _split_bf16(kvt_ref[0])
    mean = _nt(pool, hi) + _nt(pool, lo)
    mean = mean + jnp.mean(pe_ref[...], axis=0, keepdims=True)
    o_ref[0] = _mm(mean.astype(BF16), w_ref[...])


def _cmp_prompt(kvt, pe_rows, w_bd):
    nb, _, nt = kvt.shape
    return pl.pallas_call(
        _cmp_prompt_body,
        grid=(nb,),
        in_specs=[pl.BlockSpec((1, 2 * KV_WIDTH, nt), lambda b: (b, 0, 0)),
                  _const_spec((CMP_BLOCK, 2 * KV_WIDTH)), _const_spec((2 * KV_WIDTH, 2 * KV_WIDTH))],
        out_specs=pl.BlockSpec((1, LANE, 2 * KV_WIDTH), lambda b: (b, 0, 0)),
        out_shape=jax.ShapeDtypeStruct((nb, LANE, 2 * KV_WIDTH), F32),
        compiler_params=_params(("arbitrary",)),
        name="cmp_prompt",
    )(kvt, pe_rows, w_bd)


def _select_blocks(score_t, cur):
    shape = score_t.shape
    blk = lax.broadcasted_iota(jnp.int32, shape, 0)
    forced = (blk == 0) | (blk == cur) | (blk == cur - 1)
    score = jnp.where(forced | (blk >= cur), -jnp.inf, score_t)

    def pick(_, carry):
        score, unsel = carry
        mx = jnp.max(score, axis=0, keepdims=True)
        idx = jnp.min(jnp.where(score == mx, blk, shape[0]), axis=0, keepdims=True)
        hit = blk == idx
        unsel = jnp.where(hit & (mx > -jnp.inf), 0.0, unsel)
        return jnp.where(hit, -jnp.inf, score), unsel

    _, unsel = lax.fori_loop(0, N_SEL - 3, pick, (score, jnp.where(forced, 0.0, 1.0)))
    return unsel


def _softmax_rows(s):
    m = jnp.max(s, axis=-1, keepdims=True)
    e = jnp.exp(s - m)
    return e / jnp.sum(e, axis=-1, keepdims=True)


GROUP_ROWS = HPG * Q_BLOCK


def _rowmax(s):
    m = s[:, :LANE]
    for k in range(1, s.shape[1] // LANE):
        m = jnp.maximum(m, s[:, k * LANE:(k + 1) * LANE])
    return jnp.broadcast_to(jnp.max(m, axis=-1, keepdims=True), (s.shape[0], LANE))


def _tile_lanes(x, n):
    return jnp.concatenate([x] * n, axis=1)


def _attn_prompt_body(q_ref, bl_ref, cmp_ref, ka_ref, va_ref, kw_ref, vw_ref, gl_ref, bg_ref,
                      tc_ref, tn_ref, tw_ref, z_ref, u_ref, vn_ref, zg_ref, x_ref, wsp_ref, bsp_ref, wo_ref,
                      y_ref, lhs_scr, m_scr, acc_scr, part_scr, gsel_scr, ypart_scr):
    i = pl.program_id(1)
    qf = q_ref[0]

    y_g = u_ref[0] * _gmlp_mix(vn_ref[0].astype(BF16), wsp_ref, bsp_ref) * _silu(zg_ref[0])
    ypart_scr[...] = x_ref[0] + _mm(y_g.astype(BF16), wo_ref[NSA_WIDTH:])
    lane = lax.broadcasted_iota(jnp.int32, (Q_BLOCK, LANE), 1)
    stack = lambda x, g: jnp.concatenate([x[:, h * LANE:(h + 1) * LANE] for h in range(g * HPG, (g + 1) * HPG)], axis=0)
    qs = [stack(qf, g) for g in range(N_KV)]
    qs_far = [stack(qf + bl_ref[...], g) for g in range(N_KV)]

    kc = cmp_ref[0, :, :KV_WIDTH].astype(BF16)
    vc = cmp_ref[0, :, KV_WIDTH:].astype(BF16)
    seen = lane <= 2 * i + 1
    shift = (2 * i + 2) % LANE
    o_cmp, p_sum = [], []
    for g in range(N_KV):
        tmpl = jnp.concatenate([jnp.where(seen, pltpu.roll(tc_ref[h], shift, axis=1), NEG)
                                for h in range(g * HPG, (g + 1) * HPG)], axis=0)
        ok = tmpl > 0.5 * NEG
        p = jnp.where(ok, _softmax_rows(_nt(qs[g], kc) + tmpl), 0.0)
        o_cmp.append(_mm(p.astype(BF16), vc))
        p_sum.append(p.reshape(HPG, Q_BLOCK, LANE).sum(axis=0))

    w0 = pl.multiple_of(jnp.maximum(i * Q_BLOCK - WINDOW, 0), LANE)
    woff = pl.multiple_of(WINDOW - (i * Q_BLOCK - w0), LANE)
    kwt = kw_ref[0, :, pl.ds(w0, WIN_KEYS)]
    gate = jax.nn.sigmoid(gl_ref[0] + bg_ref[...])
    for g in range(N_KV):
        col = lambda k: jnp.concatenate([gate[:, k * N_HEADS + h:k * N_HEADS + h + 1]
                                         for h in range(g * HPG, (g + 1) * HPG)], axis=0)
        s = _mm(qs[g], kwt) + tw_ref[g, :, pl.ds(woff, WIN_KEYS)]
        e = jnp.exp(s - _tile_lanes(_rowmax(s), WIN_KEYS // LANE)).astype(BF16)
        pv = _nt(e, vw_ref[0, g, :, pl.ds(w0, WIN_KEYS)])
        part_scr[g] = col(0) * o_cmp[g] + col(2) * (pv / pltpu.roll(pv, HEAD_DIM, axis=1))
        gsel_scr[g] = jnp.broadcast_to(col(1), (GROUP_ROWS, LANE))

    tpos = i * Q_BLOCK + lax.broadcasted_iota(jnp.int32, (1, N_KV * Q_BLOCK), 1) % Q_BLOCK
    unsel_t = _select_blocks(jnp.concatenate([ps.T for ps in p_sum], axis=1), tpos // CMP_BLOCK)
    for g in range(N_KV):
        un = _tile_rows(unsel_t[:, g * Q_BLOCK:(g + 1) * Q_BLOCK].T.astype(BF16), HPG)
        lhs_scr[0, g] = jnp.concatenate([qs_far[g], un], axis=1)
        lhs_scr[1, g] = jnp.concatenate([qs[g], un], axis=1)

    m_scr[...] = jnp.full(m_scr.shape, NEG, F32)
    acc_scr[...] = jnp.zeros(acc_scr.shape, F32)

    def tile(j, near):
        k0 = pl.multiple_of(j * KV_TILE, KV_TILE)
        s = [_mm(lhs_scr[1 if near else 0, g], ka_ref[0, g, :, pl.ds(k0, KV_TILE)]) for g in range(N_KV)]
        if near:
            off = pl.multiple_of(NEAR_SPAN - (i * Q_BLOCK - j * KV_TILE), LANE)
            s = [s[g] + tn_ref[g, :, pl.ds(off, KV_TILE)] for g in range(N_KV)]
        m_old = [m_scr[g] for g in range(N_KV)]
        m_new = [jnp.maximum(m_old[g], _rowmax(s[g])) for g in range(N_KV)]
        p = [jnp.exp(s[g] - _tile_lanes(m_new[g], KV_TILE // LANE)).astype(BF16) for g in range(N_KV)]
        for g in range(N_KV):
            acc_scr[g] = jnp.exp(m_old[g] - m_new[g]) * acc_scr[g] + _nt(p[g], va_ref[0, g, :, pl.ds(k0, KV_TILE)])
            m_scr[g] = m_new[g]

    n_far = jnp.maximum(i * Q_BLOCK - (NEAR_SPAN + LANE) + KV_TILE, 0) // KV_TILE
    n_tiles = i // (KV_TILE // Q_BLOCK) + 1
    def far_pair(jj, c):
        tile(2 * jj, False)
        tile(2 * jj + 1, False)
        return c

    lax.fori_loop(0, n_far // 2, far_pair, 0)

    @pl.when(n_far % 2 == 1)
    def _():
        tile(n_far - 1, False)

    def near_pair(jj, c):
        tile(n_far + 2 * jj, True)
        tile(n_far + 2 * jj + 1, True)
        return c

    n_near = n_tiles - n_far
    lax.fori_loop(0, n_near // 2, near_pair, 0)

    @pl.when(n_near % 2 == 1)
    def _():
        tile(n_tiles - 1, True)

    merged = []
    for g in range(N_KV):
        acc = acc_scr[g]
        merged.append(part_scr[g] + gsel_scr[g] * (acc / pltpu.roll(acc, HEAD_DIM, axis=1)))
    y_nsa = []
    for c in range(HPG):
        rows = slice(c * Q_BLOCK, (c + 1) * Q_BLOCK)
        o = jnp.where(lane < HEAD_DIM, merged[0][rows], merged[1][rows])
        y_nsa.append((o * _silu(z_ref[0, :, c * LANE:(c + 1) * LANE])).astype(BF16))
    y_ref[0] = ypart_scr[...] + _mm(jnp.concatenate(y_nsa, axis=-1), wo_ref[:NSA_WIDTH])


def _tile_rows(x, n):
    return jnp.concatenate([x] * n, axis=0)


def _attn_prompt(q, bias_lanes, cmp, kaug, vaug, kwin, vwaug, gl, bg, t_cmp, t_near, t_win,
                 z, u, vn, zg, x, wsp, bsp, wo):
    nb, nt, _ = q.shape
    assert wsp.shape[1] == Q_BLOCK
    once = pl.Buffered(1)
    per_b = lambda a: pl.BlockSpec((1,) + a.shape[1:], lambda b, i: (b,) + (0,) * (a.ndim - 1), pipeline_mode=once)
    whole = lambda a: pl.BlockSpec(a.shape, lambda b, i: (0,) * a.ndim, pipeline_mode=once)
    row = lambda width: pl.BlockSpec((1, Q_BLOCK, width), lambda b, i: (b, i, 0))
    return pl.pallas_call(
        _attn_prompt_body,
        grid=(nb, nt // Q_BLOCK),
        in_specs=[row(QPAD), whole(bias_lanes), per_b(cmp), per_b(kaug), per_b(vaug), per_b(kwin), per_b(vwaug),
                  row(LANE), _const_spec((1, LANE)), whole(t_cmp), whole(t_near), whole(t_win),
                  row(NSA_WIDTH), row(GMLP_WIDTH), row(GMLP_WIDTH), row(GMLP_WIDTH), row(D_MODEL),
                  whole(wsp), whole(bsp), whole(wo)],
        out_specs=row(D_MODEL),
        out_shape=jax.ShapeDtypeStruct((nb, nt, D_MODEL), F32),
        scratch_shapes=[pltpu.VMEM((2, N_KV, GROUP_ROWS, 2 * LANE), BF16)]
                       + [pltpu.VMEM((N_KV, GROUP_ROWS, LANE), F32)] * 4 + [pltpu.VMEM((Q_BLOCK, D_MODEL), F32)],
        compiler_params=_params(("arbitrary", "arbitrary")),
        name="attn_prompt",
    )(q, bias_lanes, cmp, kaug, vaug, kwin, vwaug, gl, bg, t_cmp, t_near, t_win, z, u, vn, zg, x, wsp, bsp, wo)


def _prompt_kv_operands(kvtb, emask):
    nb, _, nt = kvtb.shape
    half = lambda kind, g: kvtb[:, kind * KV_WIDTH + g * HEAD_DIM:kind * KV_WIDTH + (g + 1) * HEAD_DIM]
    ones = jnp.ones((nb, HEAD_DIM, nt), BF16)
    extra = ones.at[:, 2:].set(0)
    e = jnp.broadcast_to(emask[None], (nb, LANE, nt))
    kaug = jnp.stack([jnp.concatenate([half(2, 0), extra, e], axis=1),
                      jnp.concatenate([extra, half(2, 1), e], axis=1)], axis=1)
    aug = lambda kind: jnp.stack([jnp.concatenate([half(kind, 0), ones], axis=1),
                                  jnp.concatenate([ones, half(kind, 1)], axis=1)], axis=1)
    return kaug, aug(3), kvtb[:, 4 * KV_WIDTH:5 * KV_WIDTH], aug(5)


def _far_bias_lanes(rel_bias):
    b = rel_bias[REL_BUCKETS - 1]
    hi = b.astype(BF16).astype(F32)
    pair = jnp.stack([hi, b - hi], axis=1)
    tile = jnp.zeros((N_HEADS, LANE), F32)
    first = tile.at[:, HEAD_DIM:HEAD_DIM + 2].set(pair)
    second = tile.at[:, 0:2].set(pair)
    return jnp.where((np.arange(N_HEADS) < HPG)[:, None], first, second).reshape(1, QPAD).astype(BF16)


def _silu(x):
    return x * jax.nn.sigmoid(x)


def _gmlp_mix(vn, wsp_ref, bsp_ref):
    chunk = wsp_ref.shape[1]
    r = lax.broadcasted_iota(jnp.int32, (chunk, chunk), 0)
    c = lax.broadcasted_iota(jnp.int32, (chunk, chunk), 1)
    mix = []
    for g in range(GMLP_GROUPS):
        wm = jnp.where(c <= r, wsp_ref[g], 0.0).astype(BF16)
        mix.append(jnp.concatenate(
            [_mm(wm, vn[k * chunk:(k + 1) * chunk, g * LANE:(g + 1) * LANE]) + bsp_ref[:, g:g + 1]
             for k in range(vn.shape[0] // chunk)], axis=0))
    return jnp.concatenate(mix, axis=-1)


def _post_a_body(o_ref, z_ref, u_ref, vn_ref, zg_ref, x_ref, wsp_ref, bsp_ref, wo_ref, y_ref, *, short_chunks):
    if short_chunks:
        vn = vn_ref[0]
        mix = []
        for g in range(GMLP_GROUPS):
            vg = vn[:, g * LANE:(g + 1) * LANE]
            acc = bsp_ref[:, g:g + 1] + wsp_ref[0][:, g:g + 1] * vg
            for k in range(1, wsp_ref.shape[0]):
                acc = acc + wsp_ref[k][:, g:g + 1] * pltpu.roll(vg, k, axis=0)
            mix.append(acc)
        mix = jnp.concatenate(mix, axis=-1)
    else:
        mix = _gmlp_mix(vn_ref[0].astype(BF16), wsp_ref, bsp_ref)
    y_g = u_ref[0] * mix * _silu(zg_ref[0])
    y_nsa = o_ref[0] * _silu(z_ref[0])
    cat = jnp.concatenate([y_nsa, y_g], axis=-1).astype(BF16)
    y_ref[0] = x_ref[0] + _mm(cat, wo_ref[...])


def _post_a(o, z, u, vn, zg, x, wsp, bsp, wo, short_chunks=False, chunks_per_step=1):
    nb, nt, _ = x.shape
    rows = wsp.shape[1] * chunks_per_step
    row = lambda width: pl.BlockSpec((1, rows, width), lambda b, i: (b, i, 0))
    return pl.pallas_call(
        functools.partial(_post_a_body, short_chunks=short_chunks),
        grid=(nb, nt // rows),
        in_specs=[row(NSA_WIDTH), row(NSA_WIDTH), row(GMLP_WIDTH), row(GMLP_WIDTH), row(GMLP_WIDTH),
                  row(D_MODEL), _const_spec(wsp.shape), _const_spec(bsp.shape), _const_spec(wo.shape)],
        out_specs=row(D_MODEL),
        out_shape=jax.ShapeDtypeStruct(x.shape, F32),
        compiler_params=_params(("arbitrary", "arbitrary")),
        name="post_a",
    )(o, z, u, vn, zg, x, wsp, bsp, wo)


SAMPLE_ROWS = 8
N_PAGES_MAX = 16


SEQS_PER_STEP = 2


def _stack_heads(q_ref, s):
    return jnp.concatenate([q_ref[s, :, h * LANE:(h + 1) * LANE] for h in range(N_HEADS)], axis=0).astype(BF16)


def _pad_rows(x, rows):
    return jnp.concatenate([x, jnp.zeros((rows - x.shape[0], x.shape[1]), x.dtype)], axis=0)


def _cmp_sample_body(pt_ref, q_ref, kv_ref, pe_ref, wk_ref, wv_ref, tc_ref, pool_ref, *rest):
    n_pages = (len(rest) - 2) // SEQS_PER_STEP
    oc_ref, ps_ref = rest[-2:]
    for s in range(SEQS_PER_STEP):
        _cmp_sample_one(s, rest[s * n_pages:(s + 1) * n_pages], q_ref, kv_ref, pe_ref, wk_ref, wv_ref, tc_ref,
                        pool_ref, oc_ref, ps_ref)


def _cmp_sample_one(s, pages, q_ref, kv_ref, pe_ref, wk_ref, wv_ref, tc_ref, pool_ref, oc_ref, ps_ref):
    n_pages = len(pages)
    lane = lax.broadcasted_iota(jnp.int32, (KV_WIDTH, LANE), 1)
    eye = lax.broadcasted_iota(jnp.int32, (KV_WIDTH, LANE), 0) == lane
    qs = _stack_heads(q_ref, s)
    summ = []
    for kind in range(2):
        sums = jnp.zeros((KV_WIDTH, LANE), F32)
        for p in range(0, n_pages, 2):
            x = jnp.concatenate([pages[p][kind].reshape(KV_WIDTH, PAGE), pages[p + 1][kind].reshape(KV_WIDTH, PAGE)],
                                axis=1)
            hi, lo = _split_bf16(x)
            pool = pool_ref[p * PAGE:(p + 2) * PAGE]
            sums = sums + _mm(hi, pool) + _mm(lo, pool)
        new = jnp.sum(kv_ref[s, :, kind * KV_WIDTH:(kind + 1) * KV_WIDTH], axis=0, keepdims=True)
        new_col = jnp.sum(jnp.where(eye, new, 0.0), axis=1, keepdims=True)
        sums = jnp.where(lane == 2 * n_pages, new_col, sums)
        mean_t = sums * (1.0 / CMP_BLOCK) + jnp.mean(pe_ref[kind], axis=1, keepdims=True)
        w_ref = wk_ref if kind == 0 else wv_ref
        summ.append(_mm(w_ref[...], mean_t.astype(BF16)).astype(BF16))
    tmpl = tc_ref[...]
    ok = tmpl > 0.5 * NEG
    p = jnp.where(ok, _softmax_rows(_mm(qs, summ[0]) + tmpl), 0.0)
    oc_ref[s] = _nt(p.astype(BF16), summ[1])
    ps_ref[s] = p.reshape(N_KV, HPG, SAMPLE_ROWS, LANE).sum(axis=1).reshape(N_KV * SAMPLE_ROWS, LANE)


def _page_specs(layer, n_pages, kind_block):
    def spec(s, p):
        return pl.BlockSpec((None, None, 2, N_KV, HEAD_DIM, PAGE),
                            lambda b, pt: (layer, pt[(b * SEQS_PER_STEP + s) * n_pages + p], kind_block, 0, 0, 0))
    return [spec(s, p) for s in range(SEQS_PER_STEP) for p in range(n_pages)]


def _cmp_sample(layer, pt_flat, n_pages, cache_t, q, kv, pe_t, wk_t, wv_t, t_cmp):
    nseq = q.shape[0]
    rows = N_HEADS * SAMPLE_ROWS
    seq = lambda r, w: pl.BlockSpec((SEQS_PER_STEP, r, w), lambda b, pt: (b, 0, 0))
    cst = lambda a: pl.BlockSpec(a.shape, lambda b, pt: (0,) * a.ndim)
    assert n_pages % 2 == 0 and nseq % SEQS_PER_STEP == 0
    pool = jnp.asarray((np.arange(n_pages * PAGE) // CMP_BLOCK)[:, None] == np.arange(LANE)[None, :], BF16)
    return pl.pallas_call(
        _cmp_sample_body,
        grid_spec=pltpu.PrefetchScalarGridSpec(
            num_scalar_prefetch=1, grid=(nseq // SEQS_PER_STEP,),
            in_specs=[seq(SAMPLE_ROWS, QPAD), seq(SAMPLE_ROWS, 6 * KV_WIDTH), cst(pe_t), cst(wk_t), cst(wv_t),
                      cst(t_cmp), cst(pool)] + _page_specs(layer, n_pages, 0),
            out_specs=[seq(rows, KV_WIDTH), seq(N_KV * SAMPLE_ROWS, LANE)]),
        out_shape=[jax.ShapeDtypeStruct((nseq, rows, KV_WIDTH), F32),
                   jax.ShapeDtypeStruct((nseq, N_KV * SAMPLE_ROWS, LANE), F32)],
        compiler_params=_params(("arbitrary",)),
        name="cmp_sample",
    )(pt_flat, q, kv, pe_t, wk_t, wv_t, t_cmp, pool, *([cache_t] * (SEQS_PER_STEP * n_pages)))


def _select_sample_body(ps_ref, o_ref, *, past_len):
    cols = ps_ref.shape[1]
    t = lax.broadcasted_iota(jnp.int32, (1, cols), 1) % SAMPLE_ROWS
    o_ref[...] = _select_blocks(ps_ref[...], (past_len + t) // CMP_BLOCK)


def _select_sample(ps_t, past_len):
    cols = ps_t.shape[1]
    tile = min(cols, 2 * LANE)
    spec = pl.BlockSpec((LANE, tile), lambda i: (0, i))
    return pl.pallas_call(
        functools.partial(_select_sample_body, past_len=past_len),
        grid=(cols // tile,), in_specs=[spec], out_specs=spec,
        out_shape=jax.ShapeDtypeStruct(ps_t.shape, F32),
        compiler_params=_params(("arbitrary",)),
        name="select_sample",
    )(ps_t)


def _attn_sample_body(pt_ref, q_ref, kv_ref, un_ref, oc_ref, gl_ref, bg_ref, e_ref, ts_ref, tn_ref, tw_ref,
                      win_ref, *rest):
    n_pages = (len(rest) - 1) // SEQS_PER_STEP
    for s in range(SEQS_PER_STEP):
        _attn_sample_one(s, rest[s * n_pages:(s + 1) * n_pages], q_ref, kv_ref, un_ref, oc_ref, gl_ref, bg_ref, e_ref,
                         ts_ref, tn_ref, tw_ref, win_ref, rest[-1])


def _attn_sample_one(s, pages, q_ref, kv_ref, un_ref, oc_ref, gl_ref, bg_ref, e_ref, ts_ref, tn_ref, tw_ref,
                     win_ref, o_ref):
    qs = _stack_heads(q_ref, s)
    lane = lax.broadcasted_iota(jnp.int32, (SAMPLE_ROWS, LANE), 1)
    tn = tn_ref[...]

    def new_rows(col):
        return _pad_rows(kv_ref[s, :, col * KV_WIDTH:(col + 1) * KV_WIDTH], LANE).astype(BF16)

    def branch(k_past, v_past, bias, k_col, v_col):
        s = _mm(qs, k_past) + bias
        s_new = _nt(qs, new_rows(k_col)) + tn
        m = jnp.maximum(jnp.max(s, axis=-1, keepdims=True), jnp.max(s_new, axis=-1, keepdims=True))
        p, p_new = jnp.exp(s - m), jnp.exp(s_new - m)
        den = jnp.sum(p, axis=-1, keepdims=True) + jnp.sum(p_new, axis=-1, keepdims=True)
        return (_nt(p.astype(BF16), v_past) + _mm(p_new.astype(BF16), new_rows(v_col))) / den

    cat = lambda kind: jnp.concatenate([pg[kind].reshape(KV_WIDTH, PAGE) for pg in pages], axis=1).astype(BF16)
    madd = _mm(un_ref[s].astype(BF16), e_ref[...])
    madd = jnp.concatenate([madd[g * SAMPLE_ROWS:(g + 1) * SAMPLE_ROWS] for g in range(N_KV) for _ in range(HPG)],
                           axis=0)
    o_sel = branch(cat(0), cat(1), ts_ref[...] + madd, 2, 3)
    win = lambda kind: win_ref[s, kind].reshape(KV_WIDTH, win_ref.shape[-1]).astype(BF16)
    o_win = branch(win(0), win(1), tw_ref[...], 4, 5)

    gate = jax.nn.sigmoid(gl_ref[s] + bg_ref[...])
    oc = oc_ref[s]

    def merged(h):
        rows = slice(h * SAMPLE_ROWS, (h + 1) * SAMPLE_ROWS)
        col = lambda k: gate[:, k * N_HEADS + h:k * N_HEADS + h + 1]
        return col(0) * oc[rows] + col(1) * o_sel[rows] + col(2) * o_win[rows]

    for c in range(HPG):
        o_ref[s, :, c * LANE:(c + 1) * LANE] = jnp.where(lane < HEAD_DIM, merged(c), merged(HPG + c))


def _attn_sample(layer, pt_flat, n_pages, cache_t, win_t, q, kv, unsel, oc, gl, bg, emask, t_sel, t_new, t_win):
    nseq = q.shape[0]
    seq = lambda a: pl.BlockSpec((SEQS_PER_STEP,) + a.shape[1:], lambda b, pt: (b,) + (0,) * (a.ndim - 1))
    cst = lambda a: pl.BlockSpec(a.shape, lambda b, pt: (0,) * a.ndim)
    win_spec = pl.BlockSpec((None, SEQS_PER_STEP, 2, N_KV, HEAD_DIM, win_t.shape[-1]),
                            lambda b, pt: (layer, b, 0, 0, 0, 0))
    return pl.pallas_call(
        _attn_sample_body,
        grid_spec=pltpu.PrefetchScalarGridSpec(
            num_scalar_prefetch=1, grid=(nseq // SEQS_PER_STEP,),
            in_specs=[seq(q), seq(kv), seq(unsel), seq(oc), seq(gl), cst(bg), cst(emask), cst(t_sel), cst(t_new),
                      cst(t_win), win_spec] + _page_specs(layer, n_pages, 1),
            out_specs=pl.BlockSpec((SEQS_PER_STEP, SAMPLE_ROWS, NSA_WIDTH), lambda b, pt: (b, 0, 0))),
        out_shape=jax.ShapeDtypeStruct((nseq, SAMPLE_ROWS, NSA_WIDTH), F32),
        compiler_params=_params(("arbitrary",)),
        name="attn_sample",
    )(pt_flat, q, kv, unsel, oc, gl, bg, emask, t_sel, t_new, t_win, win_t,
      *([cache_t] * (SEQS_PER_STEP * n_pages)))


def _proj_c_body(x_ref, g_ref, w_ref, xb_ref, z_ref):
    y = _mm(_rms(x_ref[...], g_ref[...]).astype(BF16), w_ref[...])
    xb_ref[...] = y[:, :LRU_WIDTH]
    z_ref[...] = y[:, LRU_WIDTH:]


def _proj_c(x, norm_g, w, tm):
    n = x.shape[0]
    row = pl.BlockSpec((tm, LRU_WIDTH), lambda i: (i, 0))
    return pl.pallas_call(
        _proj_c_body,
        grid=(n // tm,),
        in_specs=[row, _const_spec((1, D_MODEL)), _const_spec((D_MODEL, 2 * LRU_WIDTH))],
        out_specs=[row, row],
        out_shape=[jax.ShapeDtypeStruct((n, LRU_WIDTH), F32)] * 2,
        compiler_params=_params(("arbitrary",)),
        name="proj_c",
    )(x, norm_g, w)


def _lru_coeffs(xc, wra_ref, bra_ref, wrx_ref, brx_ref, lam_ref):
    xb = xc.astype(BF16)
    blocks = lambda w_ref: jnp.concatenate(
        [_mm(xb[:, k * LRU_BLOCK:(k + 1) * LRU_BLOCK], w_ref[k]) for k in range(LRU_HEADS)], axis=-1)
    r = jax.nn.sigmoid(blocks(wra_ref) + bra_ref[...])
    ig = jax.nn.sigmoid(blocks(wrx_ref) + brx_ref[...])
    log_a = -LRU_C * r * jax.nn.softplus(-lam_ref[...])
    a = jnp.exp(log_a)
    return a, jnp.sqrt(jnp.tanh(-log_a) * (1.0 + a * a)) * ig * xc


def _mix_c_prompt_body(x_ref, ng_ref, wi_ref, cw_ref, cb_ref, wra_ref, bra_ref, wrx_ref, brx_ref, lam_ref, wo_ref,
                       fg_ref, y_ref, hl_ref, tail_ref, xin_scr, a_scr, b_scr, z_scr, h_scr, *, final_norm):
    tt = x_ref.shape[1]
    pad = 8
    hist = CONV_WIDTH - 1

    @pl.when(pl.program_id(1) == 0)
    def _():
        xin_scr[0:pad] = jnp.zeros((pad, LRU_WIDTH), F32)
        h_scr[...] = jnp.zeros(h_scr.shape, F32)

    proj = _mm(_rms(x_ref[0], ng_ref[...]).astype(BF16), wi_ref[...])
    xin_scr[pad:pad + tt] = proj[:, :LRU_WIDTH]
    z_scr[...] = proj[:, LRU_WIDTH:]
    xc = cb_ref[...] + xin_scr[pad - hist:pad - hist + tt] * cw_ref[0:1]
    for k in range(1, CONV_WIDTH):
        xc = xc + xin_scr[pad - hist + k:pad - hist + k + tt] * cw_ref[k:k + 1]
    a, b = _lru_coeffs(xc, wra_ref, bra_ref, wrx_ref, brx_ref, lam_ref)
    a_scr[...] = a
    b_scr[...] = b

    row = lax.broadcasted_iota(jnp.int32, (8, LRU_WIDTH), 0)

    def step(k, h_prev):
        r0 = pl.multiple_of(k * 8, 8)
        ca, cbv = a_scr[pl.ds(r0, 8)], b_scr[pl.ds(r0, 8)]
        for s in (1, 2, 4):
            keep = row >= s
            cbv = cbv + ca * jnp.where(keep, pltpu.roll(cbv, s, axis=0), 0.0)
            ca = ca * jnp.where(keep, pltpu.roll(ca, s, axis=0), 1.0)
        hs = ca * h_prev + cbv
        b_scr[pl.ds(r0, 8)] = hs
        return jnp.broadcast_to(hs[7:8], (8, LRU_WIDTH))

    h_scr[...] = lax.fori_loop(0, tt // 8, step, h_scr[...])
    y = (b_scr[...] * _silu(z_scr[...])).astype(BF16)
    out = x_ref[0] + _mm(y, wo_ref[...])
    y_ref[0] = _rms(out, fg_ref[...]) if final_norm else out
    hl_ref[0] = h_scr[0:1]
    tail_ref[0] = xin_scr[pad + tt - hist:pad + tt]
    xin_scr[0:pad] = xin_scr[tt:tt + pad]


def _mix_c_prompt(x, ng, wi, cw, cb, wra, bra, wrx, brx, lam, wo, fg, final_norm, tt=512):
    nb, nt, _ = x.shape
    row = pl.BlockSpec((1, tt, LRU_WIDTH), lambda b, i: (b, i, 0))
    cst = lambda a: pl.BlockSpec(a.shape, lambda b, i: (0,) * a.ndim, pipeline_mode=pl.Buffered(1))
    hist = CONV_WIDTH - 1
    return pl.pallas_call(
        functools.partial(_mix_c_prompt_body, final_norm=final_norm),
        grid=(nb, nt // tt),
        in_specs=[row, cst(ng), cst(wi), cst(cw), cst(cb), cst(wra), cst(bra), cst(wrx), cst(brx), cst(lam), cst(wo),
                  cst(fg)],
        out_specs=[row, pl.BlockSpec((1, 1, LRU_WIDTH), lambda b, i: (b, 0, 0)),
                   pl.BlockSpec((1, hist, LRU_WIDTH), lambda b, i: (b, 0, 0))],
        out_shape=[jax.ShapeDtypeStruct(x.shape, F32), jax.ShapeDtypeStruct((nb, 1, LRU_WIDTH), F32),
                   jax.ShapeDtypeStruct((nb, hist, LRU_WIDTH), F32)],
        scratch_shapes=[pltpu.VMEM((tt + 8, LRU_WIDTH), F32)] + [pltpu.VMEM((tt, LRU_WIDTH), F32)] * 3
                       + [pltpu.VMEM((8, LRU_WIDTH), F32)],
        compiler_params=_params(("arbitrary", "arbitrary")),
        name="mix_c_prompt",
    )(x, ng, wi, cw, cb, wra, bra, wrx, brx, lam, wo, fg)


def _mix_c_sample_body(xb_ref, z_ref, x_ref, h0_ref, c0_ref, cw_ref, cb_ref, wra_ref, bra_ref, wrx_ref, brx_ref,
                       lam_ref, wo_ref, fg_ref, y_ref, hl_ref, tail_ref, *, final_norm):
    nt = xb_ref.shape[0]
    hist = CONV_WIDTH - 1
    xin = [c0_ref[k] for k in range(hist)] + [xb_ref[t] for t in range(nt)]
    h = h0_ref[...]
    for t in range(nt):
        xc = cb_ref[...] + xin[t] * cw_ref[0:1]
        for k in range(1, CONV_WIDTH):
            xc = xc + xin[t + k] * cw_ref[k:k + 1]
        a, b = _lru_coeffs(xc, wra_ref, bra_ref, wrx_ref, brx_ref, lam_ref)
        h = a * h + b
        out = x_ref[t] + _mm((h * _silu(z_ref[t])).astype(BF16), wo_ref[...])
        y_ref[t] = _rms(out, fg_ref[...]) if final_norm else out
    hl_ref[...] = h
    for k in range(hist):
        tail_ref[k] = xin[nt + k]


def _mix_c_sample(xb, z, x, h0, c0, cw, cb, wra, bra, wrx, brx, lam, wo, fg, final_norm):
    args = (xb, z, x, h0, c0, cw, cb, wra, bra, wrx, brx, lam, wo, fg)
    return pl.pallas_call(
        functools.partial(_mix_c_sample_body, final_norm=final_norm),
        grid=(1,),
        in_specs=[_const_spec(a.shape) for a in args],
        out_specs=[_const_spec(x.shape), _const_spec(h0.shape), _const_spec(c0.shape)],
        out_shape=[jax.ShapeDtypeStruct(x.shape, F32), jax.ShapeDtypeStruct(h0.shape, F32),
                   jax.ShapeDtypeStruct(c0.shape, F32)],
        compiler_params=_params(("arbitrary",)),
        name="mix_c_sample",
    )(*args)


def _pair_perm():
    cols = []
    for c in range(HPG):
        cols += list(range(c * HEAD_DIM, (c + 1) * HEAD_DIM))
        cols += list(range((HPG + c) * HEAD_DIM, (HPG + c + 1) * HEAD_DIM))
    return np.asarray(cols)


def _layout_w_in_a(w):
    o_kv = NSA_WIDTH
    o_g = o_kv + 6 * KV_WIDTH
    o_z = o_g + 3 * N_HEADS
    o_rest = o_z + NSA_WIDTH
    wq = w[:, :NSA_WIDTH].reshape(D_MODEL, N_HEADS, HEAD_DIM)
    zeros = jnp.zeros_like(wq)
    first = (np.arange(N_HEADS) < HPG)[None, :, None]
    wq = jnp.concatenate([jnp.where(first, wq, zeros), jnp.where(first, zeros, wq)], axis=-1)
    wg = jnp.pad(w[:, o_g:o_z], ((0, 0), (0, LANE - 3 * N_HEADS)))
    wz = w[:, o_z:o_rest][:, _pair_perm()]
    return jnp.concatenate([wq.reshape(D_MODEL, QPAD), w[:, o_kv:o_g], wz, w[:, o_rest:], wg], axis=1).astype(BF16)


def _block_diag(w, n):
    return jnp.kron(jnp.eye(n, dtype=w.dtype), w)


def _even_layer(a, yp, ys, cache_t, win_t, pt_flat, n_pages, tmpl, p):
    nb, nt, _ = yp.shape
    nseq, dec_t, _ = ys.shape
    past_len = n_pages * PAGE
    w_in = _layout_w_in_a(p['w_in_a'][a])
    norm_g = p['norm_a'][a][None]
    ln_g, ln_b = p['gmlp_ln_g'][a][None], p['gmlp_ln_b'][a][None]
    bg = jnp.pad(p['b_gate_a'][a], (0, LANE - 3 * N_HEADS))[None]
    w_out = jnp.concatenate([p['w_out_a'][a][:NSA_WIDTH][_pair_perm()], p['w_out_a'][a][NSA_WIDTH:]], axis=0).astype(BF16)
    pe, wc = p['pe_cmp'][a], p['w_cmp'][a]

    q, nsa_t, wrows_t, kvtb, gl, z, u, vn, zg = _proj_a(yp, norm_g, w_in, ln_g, ln_b, tm=256, feature_major=True)
    pe_rows = jnp.concatenate([pe[0], pe[0], pe[1], pe[1]], axis=1)
    w_bd = jnp.zeros((2 * KV_WIDTH, 2 * KV_WIDTH), F32)
    for k in range(4):
        w_bd = w_bd.at[k * HEAD_DIM:(k + 1) * HEAD_DIM, k * HEAD_DIM:(k + 1) * HEAD_DIM].set(wc[k // 2])
    cmp = _cmp_prompt(nsa_t, pe_rows, w_bd.astype(BF16))
    kaug, vaug, kwin, vwaug = _prompt_kv_operands(kvtb, tmpl['e_prompt'])
    yp_new = _attn_prompt(q, tmpl['far_lanes'], cmp, kaug, vaug, kwin, vwaug, gl, bg, tmpl['cmp'], tmpl['near'],
                          tmpl['win'], z, u, vn, zg, yp, p['w_spatial'][a], p['b_spatial'][a].T, w_out)
    rows_p = jnp.transpose(nsa_t.reshape(nb, 4, N_KV, HEAD_DIM, nt), (0, 4, 1, 2, 3))
    wlen = min(WINDOW, nt)
    win_p = jnp.transpose(wrows_t[:, :, nt - wlen:].reshape(nb, 2, N_KV, HEAD_DIM, wlen), (0, 4, 1, 2, 3))

    xs = ys.reshape(1, nseq * dec_t, D_MODEL)
    q, kv, gl, z, u, vn, zg = _proj_a(xs, norm_g, w_in, ln_g, ln_b, tm=nseq * dec_t, feature_major=False)
    pad_t = lambda x: jnp.pad(x.reshape(nseq, dec_t, -1), ((0, 0), (0, SAMPLE_ROWS - dec_t), (0, 0)))
    q8, kv8, gl8 = pad_t(q), pad_t(kv), pad_t(gl)
    pe_t = jnp.concatenate([jnp.swapaxes(pe, 1, 2)] * N_KV, axis=1)
    wk_t = _block_diag(wc[0].T, N_KV).astype(BF16)
    wv_t = _block_diag(wc[1].T, N_KV).astype(BF16)
    oc, ps = _cmp_sample(a, pt_flat, n_pages, cache_t, q8, kv8, pe_t, wk_t, wv_t, tmpl['s_cmp'])
    unsel_t = _select_sample(ps.reshape(nseq * N_KV * SAMPLE_ROWS, LANE).T, past_len)
    unsel = unsel_t.T.reshape(nseq, N_KV * SAMPLE_ROWS, LANE)
    o8 = _attn_sample(a, pt_flat, n_pages, cache_t, win_t, q8, kv8, unsel, oc, gl8, bg, tmpl['e_sample'],
                      tmpl['s_sel'], tmpl['s_new'], tmpl['s_win'])
    o = o8[:, :dec_t].reshape(1, nseq * dec_t, NSA_WIDTH)
    w_small = p['w_spatial'][a][:, :dec_t, :dec_t]
    above = lambda k: jnp.pad(jnp.diagonal(w_small, offset=-k, axis1=1, axis2=2), ((0, 0), (k, 0))).T
    w_sp = jnp.stack([jnp.tile(above(k), (nseq, 1)) for k in range(dec_t)])
    b_sp = jnp.tile(p['b_spatial'][a][:, :dec_t].T, (nseq, 1))
    ys_new = _post_a(o, z, u, vn, zg, xs, w_sp, b_sp, w_out, short_chunks=True).reshape(nseq, dec_t, D_MODEL)
    rows_s = kv[0, :, :4 * KV_WIDTH].reshape(nseq, dec_t, 4, N_KV, HEAD_DIM)
    win_s = kv[0, :, 4 * KV_WIDTH:].reshape(nseq, dec_t, 2, N_KV, HEAD_DIM)
    return yp_new, ys_new, rows_p, rows_s, win_p, win_s, vn.reshape(nseq, dec_t, GMLP_WIDTH)


def _odd_layer(c, yp, ys, h0, conv0, p, final):
    nb, nt, _ = yp.shape
    nseq, dec_t, _ = ys.shape
    norm_g = p['norm_c'][c][None]
    w_in = p['w_in_c'][c].astype(BF16)
    consts = (p['conv_w'][c], p['conv_b'][c][None], p['w_rg_a'][c].astype(BF16), p['b_rg_a'][c][None],
              p['w_rg_x'][c].astype(BF16), p['b_rg_x'][c][None], p['lru_lambda'][c][None],
              p['w_out_c'][c].astype(BF16), p['final_norm'][None])
    yp_new, h_p, tail_p = _mix_c_prompt(yp, norm_g, w_in, *consts, final_norm=final)
    xs = jnp.swapaxes(ys, 0, 1)
    xb, z = _proj_c(xs.reshape(dec_t * nseq, D_MODEL), norm_g, w_in, tm=dec_t * nseq)
    tm3 = lambda x: x.reshape(dec_t, nseq, -1)
    ys_new, h_s, tail_s = _mix_c_sample(tm3(xb), tm3(z), xs, h0, jnp.swapaxes(conv0, 0, 1), *consts,
                                        final_norm=final)
    return yp_new, jnp.swapaxes(ys_new, 0, 1), h_p[:, 0], h_s, tail_p, jnp.swapaxes(tail_s, 0, 1)


def kernel(x_prompt, x_sample, cache_nsa, cache_win, state_lru_h, state_lru_conv, page_table, norm_a, w_in_a, b_gate_a, pe_cmp, w_cmp, gmlp_ln_g, gmlp_ln_b, w_spatial, b_spatial, w_out_a, norm_c, w_in_c, conv_w, conv_b, w_rg_a, b_rg_a, w_rg_x, b_rg_x, lru_lambda, w_out_c, rel_bias, final_norm):
    p = dict(norm_a=norm_a, w_in_a=w_in_a, b_gate_a=b_gate_a, pe_cmp=pe_cmp, w_cmp=w_cmp, gmlp_ln_g=gmlp_ln_g,
             gmlp_ln_b=gmlp_ln_b, w_spatial=w_spatial, b_spatial=b_spatial, w_out_a=w_out_a, norm_c=norm_c,
             w_in_c=w_in_c, conv_w=conv_w, conv_b=conv_b, w_rg_a=w_rg_a, b_rg_a=b_rg_a, w_rg_x=w_rg_x,
             b_rg_x=b_rg_x, lru_lambda=lru_lambda, w_out_c=w_out_c, rel_bias=rel_bias, final_norm=final_norm)
    nb, nt, _ = x_prompt.shape
    nseq, n_pages = page_table.shape
    past_len = n_pages * PAGE
    wlen = cache_win.shape[2]
    depth = norm_a.shape[0] + norm_c.shape[0]
    assert nt % KV_TILE == 0 and nt >= WIN_KEYS and nt // CMP_BLOCK <= LANE
    assert past_len // CMP_BLOCK < LANE and x_sample.shape[1] <= SAMPLE_ROWS and n_pages <= N_PAGES_MAX

    cache_t = jnp.transpose(cache_nsa, (0, 1, 3, 4, 5, 2))
    win_t = jnp.transpose(cache_win, (0, 1, 3, 4, 5, 2))
    pt_flat = page_table.reshape(-1).astype(jnp.int32)

    blk_of_key = lambda n: (np.arange(n) // CMP_BLOCK)[None, :] == np.arange(LANE)[:, None]
    rows_s = SAMPLE_ROWS
    stack = lambda t: t.reshape(N_HEADS * rows_s, t.shape[-1])
    by_group = lambda t: t.reshape(N_KV, GROUP_ROWS, t.shape[-1])
    assert NEAR_SPAN + LANE - (KV_TILE - 1) >= _UPPER[-1] and nt % KV_TILE == 0
    tmpl = dict(
        far_lanes=_far_bias_lanes(rel_bias),
        e_prompt=jnp.asarray(np.where(blk_of_key(nt), NEG, 0.0), BF16),
        e_sample=jnp.asarray(np.where(blk_of_key(past_len), NEG, 0.0), BF16),
        cmp=_bias_template(rel_bias, Q_BLOCK, LANE, -CMP_BLOCK, CMP_BLOCK * (LANE - 2) - (CMP_BLOCK - 1)),
        near=by_group(_bias_template(rel_bias, Q_BLOCK, NEAR_SPAN + KV_TILE, -1, NEAR_SPAN)),
        win=by_group(_bias_template(rel_bias, Q_BLOCK, WINDOW + WIN_KEYS, -1, WINDOW, hi=WINDOW)),
        s_cmp=stack(_bias_template(rel_bias, rows_s, LANE, -CMP_BLOCK, past_len - (CMP_BLOCK - 1))),
        s_sel=stack(_bias_template(rel_bias, rows_s, past_len, -1, past_len)),
        s_new=stack(_bias_template(rel_bias, rows_s, LANE, -1, 0, cmax=x_sample.shape[1])),
        s_win=stack(_bias_template(rel_bias, rows_s, wlen, -1, wlen, hi=WINDOW)),
    )

    yp, ys = x_prompt, x_sample
    outs = [[] for _ in range(9)]
    for layer in range(depth):
        if layer % 2 == 0:
            yp, ys, *leaves = _even_layer(layer // 2, yp, ys, cache_t, win_t, pt_flat, n_pages, tmpl, p)
            for dst, leaf in zip(outs[:5], leaves):
                dst.append(leaf)
        else:
            c = layer // 2
            yp, ys, *leaves = _odd_layer(c, yp, ys, state_lru_h[c], state_lru_conv[c], p, final=layer == depth - 1)
            for dst, leaf in zip(outs[5:], leaves):
                dst.append(leaf)
    return (yp, ys) + tuple(jnp.stack(o) for o in outs)
```

```python
import functools
import math

import numpy as np
import jax
import jax.numpy as jnp
from jax import lax
from jax.experimental import pallas as pl
from jax.experimental.pallas import tpu as pltpu

F32 = jnp.float32
BF16 = jnp.bfloat16

D_MODEL = 1024
N_HEADS = 8
N_KV = 2
HPG = N_HEADS // N_KV
HEAD_DIM = 64
NSA_WIDTH = N_HEADS * HEAD_DIM
KV_WIDTH = N_KV * HEAD_DIM
CMP_BLOCK = 64
N_SEL = 16
WINDOW = 512
GMLP_GROUPS = 4
GMLP_WIDTH = 512
GMLP_CHUNK = 128
LRU_WIDTH = 1024
LRU_HEADS = 8
LRU_BLOCK = 128
CONV_WIDTH = 4
LRU_C = 8.0
REL_BUCKETS = 32
REL_MAX_DIST = 1024
EPS = 1e-6
NEG = -1e30
PAGE = 128

LANE = 128
Q_BLOCK = 128
KV_TILE = 512
NEAR_SPAN = 1280
WIN_KEYS = WINDOW + Q_BLOCK
VMEM_LIMIT = 56 * 1024 * 1024

QPAD = N_HEADS * LANE
C_KV = QPAD
C_Z = C_KV + 6 * KV_WIDTH
C_U = C_Z + NSA_WIDTH
C_V = C_U + GMLP_WIDTH
C_ZG = C_V + GMLP_WIDTH
C_G = C_ZG + GMLP_WIDTH
PROJ_COLS = C_G + LANE


def _nt(a, b):
    return lax.dot_general(a, b, (((1,), (1,)), ((), ())), preferred_element_type=F32)


def _mm(a, b):
    return jnp.dot(a, b, preferred_element_type=F32)


def _params(sem):
    return pltpu.CompilerParams(dimension_semantics=sem, vmem_limit_bytes=VMEM_LIMIT)


def _const_spec(shape):
    n = len(shape)
    return pl.BlockSpec(shape, lambda *_: (0,) * n)


def _bucket_upper_bounds():
    max_exact = REL_BUCKETS // 2
    d = np.arange(0, 4 * REL_MAX_DIST, dtype=np.int64)

    def buckets(ft):
        df = np.maximum(d, 1).astype(ft)
        large = max_exact + (np.log(df / ft(max_exact)) / ft(math.log(REL_MAX_DIST / max_exact))
                             * ft(REL_BUCKETS - max_exact)).astype(np.int32)
        return np.where(d < max_exact, d, np.minimum(large, REL_BUCKETS - 1))

    b32, b64 = buckets(np.float32), buckets(np.float64)
    assert (b32 == b64).all() and (np.diff(b32) >= 0).all() and b32[-1] == REL_BUCKETS - 1
    return [int(np.argmax(b32 > k)) for k in range(REL_BUCKETS - 1)]


_UPPER = _bucket_upper_bounds()


def _bias_tmpl_body(tab_ref, o_ref, *, rows, cs, off, hi, cmax):
    h = pl.program_id(0)
    t = lax.broadcasted_iota(jnp.int32, (rows, LANE), 0)
    lane = lax.broadcasted_iota(jnp.int32, (rows, LANE), 1)

    def chunk(k, carry):
        c0 = pl.multiple_of(k * LANE, LANE)
        c = lane + c0
        d = t + cs * c + off
        dd = jnp.maximum(d, 0)
        val = jnp.full((rows, LANE), tab_ref[REL_BUCKETS - 1, h], F32)
        for b in range(REL_BUCKETS - 2, -1, -1):
            val = jnp.where(dd < _UPPER[b], tab_ref[b, h], val)
        ok = (d >= 0) & (d <= hi) & (c < cmax)
        o_ref[0, :, pl.ds(c0, LANE)] = jnp.where(ok, val, NEG)
        return carry

    lax.fori_loop(0, o_ref.shape[2] // LANE, chunk, 0)


def _bias_template(rel_bias, rows, width, cs, off, hi=1 << 30, cmax=1 << 30):
    return pl.pallas_call(
        functools.partial(_bias_tmpl_body, rows=rows, cs=cs, off=off, hi=hi, cmax=cmax),
        grid=(N_HEADS,),
        in_specs=[pl.BlockSpec(memory_space=pltpu.SMEM)],
        out_specs=pl.BlockSpec((1, rows, width), lambda h: (h, 0, 0)),
        out_shape=jax.ShapeDtypeStruct((N_HEADS, rows, width), F32),
        compiler_params=_params(("arbitrary",)),
        name="bias_template",
    )(rel_bias)


def _rms(x, g):
    return x * lax.rsqrt(jnp.mean(x * x, axis=-1, keepdims=True) + EPS) * g


def _proj_a_body(x_ref, g_ref, w_ref, lng_ref, lnb_ref, *rest, feature_major, n_prev):
    if n_prev:
        prev_ref, rest = rest[0], rest[1:]
    q_ref, rest = rest[0], rest[1:]
    xn = _rms(x_ref[0], g_ref[...])
    y = _mm(xn.astype(BF16), w_ref[...])
    q_ref[0] = (y[:, :QPAD] * (HEAD_DIM ** -0.5)).astype(q_ref.dtype)
    kv = y[:, C_KV:C_Z]
    if feature_major:
        nsa_ref, win_ref, kvtb_ref, gl_ref, z_ref, u_ref, vn_ref, zg_ref = rest
        kvt = kv.T
        if n_prev:
            nsa_ref[0:n_prev, 0] = prev_ref[:, 0]
        nsa_ref[n_prev, 0] = kvt[:4 * KV_WIDTH]
        win_ref[0] = kvt[4 * KV_WIDTH:]
        kvtb_ref[0] = kvt.astype(BF16)
    else:
        kv_ref, gl_ref, z_ref, u_ref, vn_ref, zg_ref = rest
        kv_ref[0] = kv
    z_ref[0] = y[:, C_Z:C_U]
    u_ref[0] = y[:, C_U:C_V]
    v = y[:, C_V:C_ZG]
    mu = jnp.mean(v, axis=-1, keepdims=True)
    var = jnp.mean(jnp.square(v - mu), axis=-1, keepdims=True)
    vn_ref[0] = (v - mu) * lax.rsqrt(var + EPS) * lng_ref[...] + lnb_ref[...]
    zg_ref[0] = y[:, C_ZG:C_G]
    gl_ref[0] = y[:, C_G:]


def _proj_a(x, norm_g, w, ln_g, ln_b, tm, feature_major, prev_nsa=None):
    nb, nt, _ = x.shape
    n_prev = 0 if prev_nsa is None else prev_nsa.shape[0]
    row = lambda width: pl.BlockSpec((1, tm, width), lambda b, i: (b, i, 0))
    colT = pl.BlockSpec((1, 6 * KV_WIDTH, tm), lambda b, i: (b, 0, i))
    sds = lambda shape, dt=F32: jax.ShapeDtypeStruct(shape, dt)
    once = lambda shape: pl.BlockSpec(shape, lambda b, i: (0,) * len(shape), pipeline_mode=pl.Buffered(1))
    layers = lambda n: pl.BlockSpec((n, 1, 4 * KV_WIDTH, tm), lambda b, i: (0, b, 0, i))
    if feature_major:
        col = lambda r: pl.BlockSpec((1, r, tm), lambda b, i: (b, 0, i))
        kv_specs = [layers(n_prev + 1), col(2 * KV_WIDTH), colT]
        kv_shapes = [sds((n_prev + 1, nb, 4 * KV_WIDTH, nt)), sds((nb, 2 * KV_WIDTH, nt)),
                     sds((nb, 6 * KV_WIDTH, nt), BF16)]
    else:
        kv_specs, kv_shapes = [row(6 * KV_WIDTH)], [sds((nb, nt, 6 * KV_WIDTH))]
    prev_args, prev_specs = ([prev_nsa], [layers(n_prev)]) if n_prev else ([], [])
    return pl.pallas_call(
        functools.partial(_proj_a_body, feature_major=feature_major, n_prev=n_prev),
        grid=(nb, nt // tm),
        in_specs=[row(D_MODEL), once((1, D_MODEL)), once((D_MODEL, PROJ_COLS)),
                  once((1, GMLP_WIDTH)), once((1, GMLP_WIDTH))] + prev_specs,
        out_specs=[row(QPAD)] + kv_specs + [row(LANE), row(NSA_WIDTH), row(GMLP_WIDTH), row(GMLP_WIDTH),
                                            row(GMLP_WIDTH)],
        out_shape=[sds((nb, nt, QPAD), BF16 if feature_major else F32)] + kv_shapes
                  + [sds((nb, nt, LANE)), sds((nb, nt, NSA_WIDTH)), sds((nb, nt, GMLP_WIDTH)),
                     sds((nb, nt, GMLP_WIDTH)), sds((nb, nt, GMLP_WIDTH))],
        compiler_params=_params(("arbitrary", "arbitrary")),
        name="proj_a",
    )(x, norm_g, w, ln_g, ln_b, *prev_args)


def _split_bf16(x):
    hi = x.astype(BF16)
    return hi, (x - hi.astype(F32)).astype(BF16)


def _cmp_prompt_body(kvt_ref, pe_ref, w_ref, o_ref):
    nt = kvt_ref.shape[2]
    blk = lax.broadcasted_iota(jnp.int32, (LANE, nt), 0)
    pos = lax.broadcasted_iota(jnp.int32, (LANE, nt), 1)
    pool = jnp.where(pos // CMP_BLOCK == blk, 1.0 / CMP_BLOCK, 0.0).astype(BF16)
    hi, lo = _split_bf16(kvt_ref[0])
    mean = _nt(pool, hi) + _nt(pool, lo)
    mean = mean + jnp.mean(pe_ref[...], axis=0, keepdims=True)
    o_ref[0] = _mm(mean.astype(BF16), w_ref[...])


def _cmp_prompt(kvt, pe_rows, w_bd):
    n_layers, nb, _, nt = kvt.shape
    return pl.pallas_call(
        _cmp_prompt_body,
        grid=(nb,),
        in_specs=[pl.BlockSpec((None, 1, 2 * KV_WIDTH, nt), lambda b: (n_layers - 1, b, 0, 0)),
                  _const_spec((CMP_BLOCK, 2 * KV_WIDTH)), _const_spec((2 * KV_WIDTH, 2 * KV_WIDTH))],
        out_specs=pl.BlockSpec((1, LANE, 2 * KV_WIDTH), lambda b: (b, 0, 0)),
        out_shape=jax.ShapeDtypeStruct((nb, LANE, 2 * KV_WIDTH), F32),
        compiler_params=_params(("arbitrary",)),
        name="cmp_prompt",
    )(kvt, pe_rows, w_bd)


def _select_blocks(score_t, cur):
    shape = score_t.shape
    blk = lax.broadcasted_iota(jnp.int32, shape, 0)
    forced = (blk == 0) | (blk == cur) | (blk == cur - 1)
    score = jnp.where(forced | (blk >= cur), -jnp.inf, score_t)

    def pick(_, carry):
        score, unsel = carry
        mx = jnp.max(score, axis=0, keepdims=True)
        idx = jnp.min(jnp.where(score == mx, blk, shape[0]), axis=0, keepdims=True)
        hit = blk == idx
        unsel = jnp.where(hit & (mx > -jnp.inf), 0.0, unsel)
        return jnp.where(hit, -jnp.inf, score), unsel

    _, unsel = lax.fori_loop(0, N_SEL - 3, pick, (score, jnp.where(forced, 0.0, 1.0)))
    return unsel


def _softmax_rows(s):
    m = jnp.max(s, axis=-1, keepdims=True)
    e = jnp.exp(s - m)
    return e / jnp.sum(e, axis=-1, keepdims=True)


GROUP_ROWS = HPG * Q_BLOCK


def _rowmax(s):
    m = s[:, :LANE]
    for k in range(1, s.shape[1] // LANE):
        m = jnp.maximum(m, s[:, k * LANE:(k + 1) * LANE])
    return jnp.broadcast_to(jnp.max(m, axis=-1, keepdims=True), (s.shape[0], LANE))


def _tile_lanes(x, n):
    return jnp.concatenate([x] * n, axis=1)


def _attn_prompt_body(q_ref, bl_ref, cmp_ref, ka_ref, va_ref, kw_ref, vw_ref, gl_ref, bg_ref,
                      tc_ref, tn_ref, tw_ref, z_ref, u_ref, vn_ref, zg_ref, x_ref, wsp_ref, bsp_ref, wo_ref,
                      y_ref, lhs_scr, m_scr, acc_scr, part_scr, gsel_scr, ypart_scr):
    i = pl.program_id(1)
    qf = q_ref[0]

    y_g = u_ref[0] * _gmlp_mix(vn_ref[0].astype(BF16), wsp_ref, bsp_ref) * _silu(zg_ref[0])
    ypart_scr[...] = x_ref[0] + _mm(y_g.astype(BF16), wo_ref[NSA_WIDTH:])
    lane = lax.broadcasted_iota(jnp.int32, (Q_BLOCK, LANE), 1)
    stack = lambda x, g: jnp.concatenate([x[:, h * LANE:(h + 1) * LANE] for h in range(g * HPG, (g + 1) * HPG)], axis=0)
    qs = [stack(qf, g) for g in range(N_KV)]
    qs_far = [stack(qf + bl_ref[...], g) for g in range(N_KV)]

    kc = cmp_ref[0, :, :KV_WIDTH].astype(BF16)
    vc = cmp_ref[0, :, KV_WIDTH:].astype(BF16)
    seen = lane <= 2 * i + 1
    shift = (2 * i + 2) % LANE
    o_cmp, p_sum = [], []
    for g in range(N_KV):
        tmpl = jnp.concatenate([jnp.where(seen, pltpu.roll(tc_ref[h], shift, axis=1), NEG)
                                for h in range(g * HPG, (g + 1) * HPG)], axis=0)
        ok = tmpl > 0.5 * NEG
        p = jnp.where(ok, _softmax_rows(_nt(qs[g], kc) + tmpl), 0.0)
        o_cmp.append(_mm(p.astype(BF16), vc))
        p_sum.append(p.reshape(HPG, Q_BLOCK, LANE).sum(axis=0))

    w0 = pl.multiple_of(jnp.maximum(i * Q_BLOCK - WINDOW, 0), LANE)
    woff = pl.multiple_of(WINDOW - (i * Q_BLOCK - w0), LANE)
    kwt = kw_ref[0, :, pl.ds(w0, WIN_KEYS)]
    gate = jax.nn.sigmoid(gl_ref[0] + bg_ref[...])
    for g in range(N_KV):
        col = lambda k: jnp.concatenate([gate[:, k * N_HEADS + h:k * N_HEADS + h + 1]
                                         for h in range(g * HPG, (g + 1) * HPG)], axis=0)
        s = _mm(qs[g], kwt) + tw_ref[g, :, pl.ds(woff, WIN_KEYS)]
        e = jnp.exp(s - _tile_lanes(_rowmax(s), WIN_KEYS // LANE)).astype(BF16)
        pv = _nt(e, vw_ref[0, g, :, pl.ds(w0, WIN_KEYS)])
        part_scr[g] = col(0) * o_cmp[g] + col(2) * (pv / pltpu.roll(pv, HEAD_DIM, axis=1))
        gsel_scr[g] = jnp.broadcast_to(col(1), (GROUP_ROWS, LANE))

    tpos = i * Q_BLOCK + lax.broadcasted_iota(jnp.int32, (1, N_KV * Q_BLOCK), 1) % Q_BLOCK
    unsel_t = _select_blocks(jnp.concatenate([ps.T for ps in p_sum], axis=1), tpos // CMP_BLOCK)
    for g in range(N_KV):
        un = _tile_rows(unsel_t[:, g * Q_BLOCK:(g + 1) * Q_BLOCK].T.astype(BF16), HPG)
        lhs_scr[0, g] = jnp.concatenate([qs_far[g], un], axis=1)
        lhs_scr[1, g] = jnp.concatenate([qs[g], un], axis=1)

    m_scr[...] = jnp.full(m_scr.shape, NEG, F32)
    acc_scr[...] = jnp.zeros(acc_scr.shape, F32)

    def tile(j, near):
        k0 = pl.multiple_of(j * KV_TILE, KV_TILE)
        s = [_mm(lhs_scr[1 if near else 0, g], ka_ref[0, g, :, pl.ds(k0, KV_TILE)]) for g in range(N_KV)]
        if near:
            off = pl.multiple_of(NEAR_SPAN - (i * Q_BLOCK - j * KV_TILE), LANE)
            s = [s[g] + tn_ref[g, :, pl.ds(off, KV_TILE)] for g in range(N_KV)]
        m_old = [m_scr[g] for g in range(N_KV)]
        m_new = [jnp.maximum(m_old[g], _rowmax(s[g])) for g in range(N_KV)]
        p = [jnp.exp(s[g] - _tile_lanes(m_new[g], KV_TILE // LANE)).astype(BF16) for g in range(N_KV)]
        for g in range(N_KV):
            acc_scr[g] = jnp.exp(m_old[g] - m_new[g]) * acc_scr[g] + _nt(p[g], va_ref[0, g, :, pl.ds(k0, KV_TILE)])
            m_scr[g] = m_new[g]

    n_far = jnp.maximum(i * Q_BLOCK - (NEAR_SPAN + LANE) + KV_TILE, 0) // KV_TILE
    n_tiles = i // (KV_TILE // Q_BLOCK) + 1
    def far_quad(jj, c):
        for k in range(4):
            tile(4 * jj + k, False)
        return c

    lax.fori_loop(0, n_far // 4, far_quad, 0)
    done = n_far // 4 * 4

    @pl.when(n_far - done >= 2)
    def _():
        tile(done, False)
        tile(done + 1, False)

    @pl.when(n_far % 2 == 1)
    def _():
        tile(n_far - 1, False)

    def near_pair(jj, c):
        tile(n_far + 2 * jj, True)
        tile(n_far + 2 * jj + 1, True)
        return c

    n_near = n_tiles - n_far
    lax.fori_loop(0, n_near // 2, near_pair, 0)

    @pl.when(n_near % 2 == 1)
    def _():
        tile(n_tiles - 1, True)

    merged = []
    for g in range(N_KV):
        acc = acc_scr[g]
        merged.append(part_scr[g] + gsel_scr[g] * (acc / pltpu.roll(acc, HEAD_DIM, axis=1)))
    y_nsa = []
    for c in range(HPG):
        rows = slice(c * Q_BLOCK, (c + 1) * Q_BLOCK)
        o = jnp.where(lane < HEAD_DIM, merged[0][rows], merged[1][rows])
        y_nsa.append((o * _silu(z_ref[0, :, c * LANE:(c + 1) * LANE])).astype(BF16))
    y_ref[0] = ypart_scr[...] + _mm(jnp.concatenate(y_nsa, axis=-1), wo_ref[:NSA_WIDTH])


def _tile_rows(x, n):
    return jnp.concatenate([x] * n, axis=0)


def _attn_prompt(q, bias_lanes, cmp, kaug, vaug, kwin, vwaug, gl, bg, t_cmp, t_near, t_win,
                 z, u, vn, zg, x, wsp, bsp, wo):
    nb, nt, _ = q.shape
    assert wsp.shape[1] == Q_BLOCK
    once = pl.Buffered(1)
    per_b = lambda a: pl.BlockSpec((1,) + a.shape[1:], lambda b, i: (b,) + (0,) * (a.ndim - 1), pipeline_mode=once)
    whole = lambda a: pl.BlockSpec(a.shape, lambda b, i: (0,) * a.ndim, pipeline_mode=once)
    row = lambda width: pl.BlockSpec((1, Q_BLOCK, width), lambda b, i: (b, i, 0))
    return pl.pallas_call(
        _attn_prompt_body,
        grid=(nb, nt // Q_BLOCK),
        in_specs=[row(QPAD), whole(bias_lanes), per_b(cmp), per_b(kaug), per_b(vaug), per_b(kwin), per_b(vwaug),
                  row(LANE), _const_spec((1, LANE)), whole(t_cmp), whole(t_near), whole(t_win),
                  row(NSA_WIDTH), row(GMLP_WIDTH), row(GMLP_WIDTH), row(GMLP_WIDTH), row(D_MODEL),
                  whole(wsp), whole(bsp), whole(wo)],
        out_specs=row(D_MODEL),
        out_shape=jax.ShapeDtypeStruct((nb, nt, D_MODEL), F32),
        scratch_shapes=[pltpu.VMEM((2, N_KV, GROUP_ROWS, 2 * LANE), BF16)]
                       + [pltpu.VMEM((N_KV, GROUP_ROWS, LANE), F32)] * 4 + [pltpu.VMEM((Q_BLOCK, D_MODEL), F32)],
        compiler_params=_params(("arbitrary", "arbitrary")),
        name="attn_prompt",
    )(q, bias_lanes, cmp, kaug, vaug, kwin, vwaug, gl, bg, t_cmp, t_near, t_win, z, u, vn, zg, x, wsp, bsp, wo)


def _prompt_kv_operands(kvtb, emask):
    nb, _, nt = kvtb.shape
    half = lambda kind, g: kvtb[:, kind * KV_WIDTH + g * HEAD_DIM:kind * KV_WIDTH + (g + 1) * HEAD_DIM]
    ones = jnp.ones((nb, HEAD_DIM, nt), BF16)
    extra = ones.at[:, 2:].set(0)
    e = jnp.broadcast_to(emask[None], (nb, LANE, nt))
    kaug = jnp.stack([jnp.concatenate([half(2, 0), extra, e], axis=1),
                      jnp.concatenate([extra, half(2, 1), e], axis=1)], axis=1)
    aug = lambda kind: jnp.stack([jnp.concatenate([half(kind, 0), ones], axis=1),
                                  jnp.concatenate([ones, half(kind, 1)], axis=1)], axis=1)
    return kaug, aug(3), kvtb[:, 4 * KV_WIDTH:5 * KV_WIDTH], aug(5)


def _far_bias_lanes(rel_bias):
    b = rel_bias[REL_BUCKETS - 1]
    hi = b.astype(BF16).astype(F32)
    pair = jnp.stack([hi, b - hi], axis=1)
    tile = jnp.zeros((N_HEADS, LANE), F32)
    first = tile.at[:, HEAD_DIM:HEAD_DIM + 2].set(pair)
    second = tile.at[:, 0:2].set(pair)
    return jnp.where((np.arange(N_HEADS) < HPG)[:, None], first, second).reshape(1, QPAD).astype(BF16)


def _silu(x):
    return x * jax.nn.sigmoid(x)


def _gmlp_mix(vn, wsp_ref, bsp_ref):
    chunk = wsp_ref.shape[1]
    r = lax.broadcasted_iota(jnp.int32, (chunk, chunk), 0)
    c = lax.broadcasted_iota(jnp.int32, (chunk, chunk), 1)
    mix = []
    for g in range(GMLP_GROUPS):
        wm = jnp.where(c <= r, wsp_ref[g], 0.0).astype(BF16)
        mix.append(jnp.concatenate(
            [_mm(wm, vn[k * chunk:(k + 1) * chunk, g * LANE:(g + 1) * LANE]) + bsp_ref[:, g:g + 1]
             for k in range(vn.shape[0] // chunk)], axis=0))
    return jnp.concatenate(mix, axis=-1)


def _post_a_body(o_ref, z_ref, u_ref, vn_ref, zg_ref, x_ref, wsp_ref, bsp_ref, wo_ref, y_ref, *, short_chunks):
    if short_chunks:
        vn = vn_ref[0]
        mix = []
        for g in range(GMLP_GROUPS):
            vg = vn[:, g * LANE:(g + 1) * LANE]
            acc = bsp_ref[:, g:g + 1] + wsp_ref[0][:, g:g + 1] * vg
            for k in range(1, wsp_ref.shape[0]):
                acc = acc + wsp_ref[k][:, g:g + 1] * pltpu.roll(vg, k, axis=0)
            mix.append(acc)
        mix = jnp.concatenate(mix, axis=-1)
    else:
        mix = _gmlp_mix(vn_ref[0].astype(BF16), wsp_ref, bsp_ref)
    y_g = u_ref[0] * mix * _silu(zg_ref[0])
    y_nsa = o_ref[0] * _silu(z_ref[0])
    cat = jnp.concatenate([y_nsa, y_g], axis=-1).astype(BF16)
    y_ref[0] = x_ref[0] + _mm(cat, wo_ref[...])


def _post_a(o, z, u, vn, zg, x, wsp, bsp, wo, short_chunks=False):
    nb, nt, _ = x.shape
    rows = wsp.shape[1]
    row = lambda width: pl.BlockSpec((1, rows, width), lambda b, i: (b, i, 0))
    return pl.pallas_call(
        functools.partial(_post_a_body, short_chunks=short_chunks),
        grid=(nb, nt // rows),
        in_specs=[row(NSA_WIDTH), row(NSA_WIDTH), row(GMLP_WIDTH), row(GMLP_WIDTH), row(GMLP_WIDTH),
                  row(D_MODEL), _const_spec(wsp.shape), _const_spec(bsp.shape), _const_spec(wo.shape)],
        out_specs=row(D_MODEL),
        out_shape=jax.ShapeDtypeStruct(x.shape, F32),
        compiler_params=_params(("arbitrary", "arbitrary")),
        name="post_a",
    )(o, z, u, vn, zg, x, wsp, bsp, wo)


SAMPLE_ROWS = 8
N_PAGES_MAX = 16


SEQS_PER_STEP = 2


def _stack_heads(q_ref, s):
    return jnp.concatenate([q_ref[s, :, h * LANE:(h + 1) * LANE] for h in range(N_HEADS)], axis=0).astype(BF16)


def _pad_rows(x, rows):
    return jnp.concatenate([x, jnp.zeros((rows - x.shape[0], x.shape[1]), x.dtype)], axis=0)


def _cmp_sample_body(pt_ref, q_ref, kv_ref, pe_ref, wk_ref, wv_ref, tc_ref, pool_ref, *rest):
    n_pages = (len(rest) - 2) // SEQS_PER_STEP
    oc_ref, ps_ref = rest[-2:]
    for s in range(SEQS_PER_STEP):
        _cmp_sample_one(s, rest[s * n_pages:(s + 1) * n_pages], q_ref, kv_ref, pe_ref, wk_ref, wv_ref, tc_ref,
                        pool_ref, oc_ref, ps_ref)


def _cmp_sample_one(s, pages, q_ref, kv_ref, pe_ref, wk_ref, wv_ref, tc_ref, pool_ref, oc_ref, ps_ref):
    n_pages = len(pages)
    lane = lax.broadcasted_iota(jnp.int32, (KV_WIDTH, LANE), 1)
    eye = lax.broadcasted_iota(jnp.int32, (KV_WIDTH, LANE), 0) == lane
    qs = _stack_heads(q_ref, s)
    summ = []
    for kind in range(2):
        sums = jnp.zeros((KV_WIDTH, LANE), F32)
        for p in range(0, n_pages, 2):
            x = jnp.concatenate([pages[p][kind].reshape(KV_WIDTH, PAGE), pages[p + 1][kind].reshape(KV_WIDTH, PAGE)],
                                axis=1)
            hi, lo = _split_bf16(x)
            pool = pool_ref[p * PAGE:(p + 2) * PAGE]
            sums = sums + _mm(hi, pool) + _mm(lo, pool)
        new = jnp.sum(kv_ref[s, :, kind * KV_WIDTH:(kind + 1) * KV_WIDTH], axis=0, keepdims=True)
        new_col = jnp.sum(jnp.where(eye, new, 0.0), axis=1, keepdims=True)
        sums = jnp.where(lane == 2 * n_pages, new_col, sums)
        mean_t = sums * (1.0 / CMP_BLOCK) + jnp.mean(pe_ref[kind], axis=1, keepdims=True)
        w_ref = wk_ref if kind == 0 else wv_ref
        summ.append(_mm(w_ref[...], mean_t.astype(BF16)).astype(BF16))
    tmpl = tc_ref[...]
    ok = tmpl > 0.5 * NEG
    p = jnp.where(ok, _softmax_rows(_mm(qs, summ[0]) + tmpl), 0.0)
    oc_ref[s] = _nt(p.astype(BF16), summ[1])
    ps_ref[s] = p.reshape(N_KV, HPG, SAMPLE_ROWS, LANE).sum(axis=1).reshape(N_KV * SAMPLE_ROWS, LANE)


def _page_specs(layer, n_pages, kind_block):
    def spec(s, p):
        return pl.BlockSpec((None, None, 2, N_KV, HEAD_DIM, PAGE),
                            lambda b, pt: (layer, pt[(b * SEQS_PER_STEP + s) * n_pages + p], kind_block, 0, 0, 0))
    return [spec(s, p) for s in range(SEQS_PER_STEP) for p in range(n_pages)]


def _cmp_sample(layer, pt_flat, n_pages, cache_t, q, kv, pe_t, wk_t, wv_t, t_cmp):
    nseq = q.shape[0]
    rows = N_HEADS * SAMPLE_ROWS
    seq = lambda r, w: pl.BlockSpec((SEQS_PER_STEP, r, w), lambda b, pt: (b, 0, 0))
    cst = lambda a: pl.BlockSpec(a.shape, lambda b, pt: (0,) * a.ndim)
    assert n_pages % 2 == 0 and nseq % SEQS_PER_STEP == 0
    pool = jnp.asarray((np.arange(n_pages * PAGE) // CMP_BLOCK)[:, None] == np.arange(LANE)[None, :], BF16)
    return pl.pallas_call(
        _cmp_sample_body,
        grid_spec=pltpu.PrefetchScalarGridSpec(
            num_scalar_prefetch=1, grid=(nseq // SEQS_PER_STEP,),
            in_specs=[seq(SAMPLE_ROWS, QPAD), seq(SAMPLE_ROWS, 6 * KV_WIDTH), cst(pe_t), cst(wk_t), cst(wv_t),
                      cst(t_cmp), cst(pool)] + _page_specs(layer, n_pages, 0),
            out_specs=[seq(rows, KV_WIDTH), seq(N_KV * SAMPLE_ROWS, LANE)]),
        out_shape=[jax.ShapeDtypeStruct((nseq, rows, KV_WIDTH), F32),
                   jax.ShapeDtypeStruct((nseq, N_KV * SAMPLE_ROWS, LANE), F32)],
        compiler_params=_params(("arbitrary",)),
        name="cmp_sample",
    )(pt_flat, q, kv, pe_t, wk_t, wv_t, t_cmp, pool, *([cache_t] * (SEQS_PER_STEP * n_pages)))


def _select_sample_body(ps_ref, o_ref, *, past_len):
    cols = ps_ref.shape[1]
    t = lax.broadcasted_iota(jnp.int32, (1, cols), 1) % SAMPLE_ROWS
    o_ref[...] = _select_blocks(ps_ref[...], (past_len + t) // CMP_BLOCK)


def _select_sample(ps_t, past_len):
    cols = ps_t.shape[1]
    tile = min(cols, 2 * LANE)
    spec = pl.BlockSpec((LANE, tile), lambda i: (0, i))
    return pl.pallas_call(
        functools.partial(_select_sample_body, past_len=past_len),
        grid=(cols // tile,), in_specs=[spec], out_specs=spec,
        out_shape=jax.ShapeDtypeStruct(ps_t.shape, F32),
        compiler_params=_params(("arbitrary",)),
        name="select_sample",
    )(ps_t)


def _attn_sample_body(pt_ref, q_ref, kv_ref, un_ref, oc_ref, gl_ref, bg_ref, e_ref, ts_ref, tn_ref, tw_ref,
                      win_ref, *rest):
    n_pages = (len(rest) - 1) // SEQS_PER_STEP
    for s in range(SEQS_PER_STEP):
        _attn_sample_one(s, rest[s * n_pages:(s + 1) * n_pages], q_ref, kv_ref, un_ref, oc_ref, gl_ref, bg_ref, e_ref,
                         ts_ref, tn_ref, tw_ref, win_ref, rest[-1])


def _attn_sample_one(s, pages, q_ref, kv_ref, un_ref, oc_ref, gl_ref, bg_ref, e_ref, ts_ref, tn_ref, tw_ref,
                     win_ref, o_ref):
    qs = _stack_heads(q_ref, s)
    lane = lax.broadcasted_iota(jnp.int32, (SAMPLE_ROWS, LANE), 1)
    tn = tn_ref[...]

    def new_rows(col):
        return _pad_rows(kv_ref[s, :, col * KV_WIDTH:(col + 1) * KV_WIDTH], LANE).astype(BF16)

    def branch(k_past, v_past, bias, k_col, v_col):
        s = _mm(qs, k_past) + bias
        s_new = _nt(qs, new_rows(k_col)) + tn
        m = jnp.maximum(jnp.max(s, axis=-1, keepdims=True), jnp.max(s_new, axis=-1, keepdims=True))
        p, p_new = jnp.exp(s - m), jnp.exp(s_new - m)
        den = jnp.sum(p, axis=-1, keepdims=True) + jnp.sum(p_new, axis=-1, keepdims=True)
        return (_nt(p.astype(BF16), v_past) + _mm(p_new.astype(BF16), new_rows(v_col))) / den

    cat = lambda kind: jnp.concatenate([pg[kind].reshape(KV_WIDTH, PAGE) for pg in pages], axis=1).astype(BF16)
    madd = _mm(un_ref[s].astype(BF16), e_ref[...])
    madd = jnp.concatenate([madd[g * SAMPLE_ROWS:(g + 1) * SAMPLE_ROWS] for g in range(N_KV) for _ in range(HPG)],
                           axis=0)
    o_sel = branch(cat(0), cat(1), ts_ref[...] + madd, 2, 3)
    win = lambda kind: win_ref[s, kind].reshape(KV_WIDTH, win_ref.shape[-1]).astype(BF16)
    o_win = branch(win(0), win(1), tw_ref[...], 4, 5)

    gate = jax.nn.sigmoid(gl_ref[s] + bg_ref[...])
    oc = oc_ref[s]

    def merged(h):
        rows = slice(h * SAMPLE_ROWS, (h + 1) * SAMPLE_ROWS)
        col = lambda k: gate[:, k * N_HEADS + h:k * N_HEADS + h + 1]
        return col(0) * oc[rows] + col(1) * o_sel[rows] + col(2) * o_win[rows]

    for c in range(HPG):
        o_ref[s, :, c * LANE:(c + 1) * LANE] = jnp.where(lane < HEAD_DIM, merged(c), merged(HPG + c))


def _attn_sample(layer, pt_flat, n_pages, cache_t, win_t, q, kv, unsel, oc, gl, bg, emask, t_sel, t_new, t_win):
    nseq = q.shape[0]
    seq = lambda a: pl.BlockSpec((SEQS_PER_STEP,) + a.shape[1:], lambda b, pt: (b,) + (0,) * (a.ndim - 1))
    cst = lambda a: pl.BlockSpec(a.shape, lambda b, pt: (0,) * a.ndim)
    win_spec = pl.BlockSpec((None, SEQS_PER_STEP, 2, N_KV, HEAD_DIM, win_t.shape[-1]),
                            lambda b, pt: (layer, b, 0, 0, 0, 0))
    return pl.pallas_call(
        _attn_sample_body,
        grid_spec=pltpu.PrefetchScalarGridSpec(
            num_scalar_prefetch=1, grid=(nseq // SEQS_PER_STEP,),
            in_specs=[seq(q), seq(kv), seq(unsel), seq(oc), seq(gl), cst(bg), cst(emask), cst(t_sel), cst(t_new),
                      cst(t_win), win_spec] + _page_specs(layer, n_pages, 1),
            out_specs=pl.BlockSpec((SEQS_PER_STEP, SAMPLE_ROWS, NSA_WIDTH), lambda b, pt: (b, 0, 0))),
        out_shape=jax.ShapeDtypeStruct((nseq, SAMPLE_ROWS, NSA_WIDTH), F32),
        compiler_params=_params(("arbitrary",)),
        name="attn_sample",
    )(pt_flat, q, kv, unsel, oc, gl, bg, emask, t_sel, t_new, t_win, win_t,
      *([cache_t] * (SEQS_PER_STEP * n_pages)))


def _proj_c_body(x_ref, g_ref, w_ref, xb_ref, z_ref):
    y = _mm(_rms(x_ref[...], g_ref[...]).astype(BF16), w_ref[...])
    xb_ref[...] = y[:, :LRU_WIDTH]
    z_ref[...] = y[:, LRU_WIDTH:]


def _proj_c(x, norm_g, w, tm):
    n = x.shape[0]
    row = pl.BlockSpec((tm, LRU_WIDTH), lambda i: (i, 0))
    return pl.pallas_call(
        _proj_c_body,
        grid=(n // tm,),
        in_specs=[row, _const_spec((1, D_MODEL)), _const_spec((D_MODEL, 2 * LRU_WIDTH))],
        out_specs=[row, row],
        out_shape=[jax.ShapeDtypeStruct((n, LRU_WIDTH), F32)] * 2,
        compiler_params=_params(("arbitrary",)),
        name="proj_c",
    )(x, norm_g, w)


def _lru_coeffs(xc, wra_ref, bra_ref, wrx_ref, brx_ref, lam_ref):
    xb = xc.astype(BF16)
    blocks = lambda w_ref: jnp.concatenate(
        [_mm(xb[:, k * LRU_BLOCK:(k + 1) * LRU_BLOCK], w_ref[k]) for k in range(LRU_HEADS)], axis=-1)
    r = jax.nn.sigmoid(blocks(wra_ref) + bra_ref[...])
    ig = jax.nn.sigmoid(blocks(wrx_ref) + brx_ref[...])
    log_a = -LRU_C * r * jax.nn.softplus(-lam_ref[...])
    a = jnp.exp(log_a)
    return a, jnp.sqrt(jnp.tanh(-log_a) * (1.0 + a * a)) * ig * xc


def _mix_c_prompt_body(x_ref, ng_ref, wi_ref, cw_ref, cb_ref, wra_ref, bra_ref, wrx_ref, brx_ref, lam_ref, wo_ref,
                       fg_ref, y_ref, hl_ref, tail_ref, xin_scr, a_scr, b_scr, z_scr, h_scr, *, final_norm):
    tt = x_ref.shape[1]
    pad = 8
    hist = CONV_WIDTH - 1

    @pl.when(pl.program_id(1) == 0)
    def _():
        xin_scr[0:pad] = jnp.zeros((pad, LRU_WIDTH), F32)
        h_scr[...] = jnp.zeros(h_scr.shape, F32)

    proj = _mm(_rms(x_ref[0], ng_ref[...]).astype(BF16), wi_ref[...])
    xin_scr[pad:pad + tt] = proj[:, :LRU_WIDTH]
    z_scr[...] = proj[:, LRU_WIDTH:]
    xc = cb_ref[...] + xin_scr[pad - hist:pad - hist + tt] * cw_ref[0:1]
    for k in range(1, CONV_WIDTH):
        xc = xc + xin_scr[pad - hist + k:pad - hist + k + tt] * cw_ref[k:k + 1]
    a, b = _lru_coeffs(xc, wra_ref, bra_ref, wrx_ref, brx_ref, lam_ref)
    a_scr[...] = a
    b_scr[...] = b

    row = lax.broadcasted_iota(jnp.int32, (8, LRU_WIDTH), 0)

    def step(k, h_prev):
        r0 = pl.multiple_of(k * 8, 8)
        ca, cbv = a_scr[pl.ds(r0, 8)], b_scr[pl.ds(r0, 8)]
        for s in (1, 2, 4):
            keep = row >= s
            cbv = cbv + ca * jnp.where(keep, pltpu.roll(cbv, s, axis=0), 0.0)
            ca = ca * jnp.where(keep, pltpu.roll(ca, s, axis=0), 1.0)
        hs = ca * h_prev + cbv
        b_scr[pl.ds(r0, 8)] = hs
        return jnp.broadcast_to(hs[7:8], (8, LRU_WIDTH))

    h_scr[...] = lax.fori_loop(0, tt // 8, step, h_scr[...])
    y = (b_scr[...] * _silu(z_scr[...])).astype(BF16)
    out = x_ref[0] + _mm(y, wo_ref[...])
    y_ref[0] = _rms(out, fg_ref[...]) if final_norm else out
    hl_ref[0] = h_scr[0:1]
    tail_ref[0] = xin_scr[pad + tt - hist:pad + tt]
    xin_scr[0:pad] = xin_scr[tt:tt + pad]


def _mix_c_prompt(x, ng, wi, cw, cb, wra, bra, wrx, brx, lam, wo, fg, final_norm, tt=512):
    nb, nt, _ = x.shape
    row = pl.BlockSpec((1, tt, LRU_WIDTH), lambda b, i: (b, i, 0))
    cst = lambda a: pl.BlockSpec(a.shape, lambda b, i: (0,) * a.ndim, pipeline_mode=pl.Buffered(1))
    hist = CONV_WIDTH - 1
    return pl.pallas_call(
        functools.partial(_mix_c_prompt_body, final_norm=final_norm),
        grid=(nb, nt // tt),
        in_specs=[row, cst(ng), cst(wi), cst(cw), cst(cb), cst(wra), cst(bra), cst(wrx), cst(brx), cst(lam), cst(wo),
                  cst(fg)],
        out_specs=[row, pl.BlockSpec((1, 1, LRU_WIDTH), lambda b, i: (b, 0, 0)),
                   pl.BlockSpec((1, hist, LRU_WIDTH), lambda b, i: (b, 0, 0))],
        out_shape=[jax.ShapeDtypeStruct(x.shape, F32), jax.ShapeDtypeStruct((nb, 1, LRU_WIDTH), F32),
                   jax.ShapeDtypeStruct((nb, hist, LRU_WIDTH), F32)],
        scratch_shapes=[pltpu.VMEM((tt + 8, LRU_WIDTH), F32)] + [pltpu.VMEM((tt, LRU_WIDTH), F32)] * 3
                       + [pltpu.VMEM((8, LRU_WIDTH), F32)],
        compiler_params=_params(("arbitrary", "arbitrary")),
        name="mix_c_prompt",
    )(x, ng, wi, cw, cb, wra, bra, wrx, brx, lam, wo, fg)


def _mix_c_sample_body(xb_ref, z_ref, x_ref, h0_ref, c0_ref, cw_ref, cb_ref, wra_ref, bra_ref, wrx_ref, brx_ref,
                       lam_ref, wo_ref, fg_ref, y_ref, hl_ref, tail_ref, *, final_norm):
    nt = xb_ref.shape[0]
    hist = CONV_WIDTH - 1
    xin = [c0_ref[k] for k in range(hist)] + [xb_ref[t] for t in range(nt)]
    h = h0_ref[...]
    for t in range(nt):
        xc = cb_ref[...] + xin[t] * cw_ref[0:1]
        for k in range(1, CONV_WIDTH):
            xc = xc + xin[t + k] * cw_ref[k:k + 1]
        a, b = _lru_coeffs(xc, wra_ref, bra_ref, wrx_ref, brx_ref, lam_ref)
        h = a * h + b
        out = x_ref[t] + _mm((h * _silu(z_ref[t])).astype(BF16), wo_ref[...])
        y_ref[t] = _rms(out, fg_ref[...]) if final_norm else out
    hl_ref[...] = h
    for k in range(hist):
        tail_ref[k] = xin[nt + k]


def _mix_c_sample(xb, z, x, h0, c0, cw, cb, wra, bra, wrx, brx, lam, wo, fg, final_norm):
    args = (xb, z, x, h0, c0, cw, cb, wra, bra, wrx, brx, lam, wo, fg)
    return pl.pallas_call(
        functools.partial(_mix_c_sample_body, final_norm=final_norm),
        grid=(1,),
        in_specs=[_const_spec(a.shape) for a in args],
        out_specs=[_const_spec(x.shape), _const_spec(h0.shape), _const_spec(c0.shape)],
        out_shape=[jax.ShapeDtypeStruct(x.shape, F32), jax.ShapeDtypeStruct(h0.shape, F32),
                   jax.ShapeDtypeStruct(c0.shape, F32)],
        compiler_params=_params(("arbitrary",)),
        name="mix_c_sample",
    )(*args)


def _pair_perm():
    cols = []
    for c in range(HPG):
        cols += list(range(c * HEAD_DIM, (c + 1) * HEAD_DIM))
        cols += list(range((HPG + c) * HEAD_DIM, (HPG + c + 1) * HEAD_DIM))
    return np.asarray(cols)


def _layout_w_in_a(w):
    o_kv = NSA_WIDTH
    o_g = o_kv + 6 * KV_WIDTH
    o_z = o_g + 3 * N_HEADS
    o_rest = o_z + NSA_WIDTH
    wq = w[:, :NSA_WIDTH].reshape(D_MODEL, N_HEADS, HEAD_DIM)
    zeros = jnp.zeros_like(wq)
    first = (np.arange(N_HEADS) < HPG)[None, :, None]
    wq = jnp.concatenate([jnp.where(first, wq, zeros), jnp.where(first, zeros, wq)], axis=-1)
    wg = jnp.pad(w[:, o_g:o_z], ((0, 0), (0, LANE - 3 * N_HEADS)))
    wz = w[:, o_z:o_rest][:, _pair_perm()]
    return jnp.concatenate([wq.reshape(D_MODEL, QPAD), w[:, o_kv:o_g], wz, w[:, o_rest:], wg], axis=1).astype(BF16)


def _block_diag(w, n):
    return jnp.kron(jnp.eye(n, dtype=w.dtype), w)


def _even_layer(a, yp, ys, prev_nsa, cache_t, win_t, pt_flat, n_pages, tmpl, p):
    nb, nt, _ = yp.shape
    nseq, dec_t, _ = ys.shape
    past_len = n_pages * PAGE
    w_in = _layout_w_in_a(p['w_in_a'][a])
    norm_g = p['norm_a'][a][None]
    ln_g, ln_b = p['gmlp_ln_g'][a][None], p['gmlp_ln_b'][a][None]
    bg = jnp.pad(p['b_gate_a'][a], (0, LANE - 3 * N_HEADS))[None]
    w_out = jnp.concatenate([p['w_out_a'][a][:NSA_WIDTH][_pair_perm()], p['w_out_a'][a][NSA_WIDTH:]], axis=0).astype(BF16)
    pe, wc = p['pe_cmp'][a], p['w_cmp'][a]

    q, nsa_t, wrows_t, kvtb, gl, z, u, vn, zg = _proj_a(yp, norm_g, w_in, ln_g, ln_b, tm=min(512, nt),
                                                        feature_major=True, prev_nsa=prev_nsa)
    pe_rows = jnp.concatenate([pe[0], pe[0], pe[1], pe[1]], axis=1)
    w_bd = jnp.zeros((2 * KV_WIDTH, 2 * KV_WIDTH), F32)
    for k in range(4):
        w_bd = w_bd.at[k * HEAD_DIM:(k + 1) * HEAD_DIM, k * HEAD_DIM:(k + 1) * HEAD_DIM].set(wc[k // 2])
    cmp = _cmp_prompt(nsa_t, pe_rows, w_bd.astype(BF16))
    kaug, vaug, kwin, vwaug = _prompt_kv_operands(kvtb, tmpl['e_prompt'])
    yp_new = _attn_prompt(q, tmpl['far_lanes'], cmp, kaug, vaug, kwin, vwaug, gl, bg, tmpl['cmp'], tmpl['near'],
                          tmpl['win'], z, u, vn, zg, yp, p['w_spatial'][a], p['b_spatial'][a].T, w_out)
    wlen = min(WINDOW, nt)
    win_p = jnp.transpose(wrows_t[:, :, nt - wlen:].reshape(nb, 2, N_KV, HEAD_DIM, wlen), (0, 4, 1, 2, 3))

    xs = ys.reshape(1, nseq * dec_t, D_MODEL)
    q, kv, gl, z, u, vn, zg = _proj_a(xs, norm_g, w_in, ln_g, ln_b, tm=nseq * dec_t, feature_major=False)
    pad_t = lambda x: jnp.pad(x.reshape(nseq, dec_t, -1), ((0, 0), (0, SAMPLE_ROWS - dec_t), (0, 0)))
    q8, kv8, gl8 = pad_t(q), pad_t(kv), pad_t(gl)
    pe_t = jnp.concatenate([jnp.swapaxes(pe, 1, 2)] * N_KV, axis=1)
    wk_t = _block_diag(wc[0].T, N_KV).astype(BF16)
    wv_t = _block_diag(wc[1].T, N_KV).astype(BF16)
    oc, ps = _cmp_sample(a, pt_flat, n_pages, cache_t, q8, kv8, pe_t, wk_t, wv_t, tmpl['s_cmp'])
    unsel_t = _select_sample(ps.reshape(nseq * N_KV * SAMPLE_ROWS, LANE).T, past_len)
    unsel = unsel_t.T.reshape(nseq, N_KV * SAMPLE_ROWS, LANE)
    o8 = _attn_sample(a, pt_flat, n_pages, cache_t, win_t, q8, kv8, unsel, oc, gl8, bg, tmpl['e_sample'],
                      tmpl['s_sel'], tmpl['s_new'], tmpl['s_win'])
    o = o8[:, :dec_t].reshape(1, nseq * dec_t, NSA_WIDTH)
    w_small = p['w_spatial'][a][:, :dec_t, :dec_t]
    above = lambda k: jnp.pad(jnp.diagonal(w_small, offset=-k, axis1=1, axis2=2), ((0, 0), (k, 0))).T
    w_sp = jnp.stack([jnp.tile(above(k), (nseq, 1)) for k in range(dec_t)])
    b_sp = jnp.tile(p['b_spatial'][a][:, :dec_t].T, (nseq, 1))
    ys_new = _post_a(o, z, u, vn, zg, xs, w_sp, b_sp, w_out, short_chunks=True).reshape(nseq, dec_t, D_MODEL)
    rows_s = kv[0, :, :4 * KV_WIDTH].reshape(nseq, dec_t, 4, N_KV, HEAD_DIM)
    win_s = kv[0, :, 4 * KV_WIDTH:].reshape(nseq, dec_t, 2, N_KV, HEAD_DIM)
    return yp_new, ys_new, nsa_t, rows_s, win_p, win_s, vn.reshape(nseq, dec_t, GMLP_WIDTH)


def _odd_layer(c, yp, ys, h0, conv0, p, final):
    nb, nt, _ = yp.shape
    nseq, dec_t, _ = ys.shape
    norm_g = p['norm_c'][c][None]
    w_in = p['w_in_c'][c].astype(BF16)
    consts = (p['conv_w'][c], p['conv_b'][c][None], p['w_rg_a'][c].astype(BF16), p['b_rg_a'][c][None],
              p['w_rg_x'][c].astype(BF16), p['b_rg_x'][c][None], p['lru_lambda'][c][None],
              p['w_out_c'][c].astype(BF16), p['final_norm'][None])
    yp_new, h_p, tail_p = _mix_c_prompt(yp, norm_g, w_in, *consts, final_norm=final)
    xs = jnp.swapaxes(ys, 0, 1)
    xb, z = _proj_c(xs.reshape(dec_t * nseq, D_MODEL), norm_g, w_in, tm=dec_t * nseq)
    tm3 = lambda x: x.reshape(dec_t, nseq, -1)
    ys_new, h_s, tail_s = _mix_c_sample(tm3(xb), tm3(z), xs, h0, jnp.swapaxes(conv0, 0, 1), *consts,
                                        final_norm=final)
    return yp_new, jnp.swapaxes(ys_new, 0, 1), h_p[:, 0], h_s, tail_p, jnp.swapaxes(tail_s, 0, 1)


def kernel(x_prompt, x_sample, cache_nsa, cache_win, state_lru_h, state_lru_conv, page_table, norm_a, w_in_a, b_gate_a, pe_cmp, w_cmp, gmlp_ln_g, gmlp_ln_b, w_spatial, b_spatial, w_out_a, norm_c, w_in_c, conv_w, conv_b, w_rg_a, b_rg_a, w_rg_x, b_rg_x, lru_lambda, w_out_c, rel_bias, final_norm):
    p = dict(norm_a=norm_a, w_in_a=w_in_a, b_gate_a=b_gate_a, pe_cmp=pe_cmp, w_cmp=w_cmp, gmlp_ln_g=gmlp_ln_g,
             gmlp_ln_b=gmlp_ln_b, w_spatial=w_spatial, b_spatial=b_spatial, w_out_a=w_out_a, norm_c=norm_c,
             w_in_c=w_in_c, conv_w=conv_w, conv_b=conv_b, w_rg_a=w_rg_a, b_rg_a=b_rg_a, w_rg_x=w_rg_x,
             b_rg_x=b_rg_x, lru_lambda=lru_lambda, w_out_c=w_out_c, rel_bias=rel_bias, final_norm=final_norm)
    nb, nt, _ = x_prompt.shape
    nseq, n_pages = page_table.shape
    past_len = n_pages * PAGE
    wlen = cache_win.shape[2]
    depth = norm_a.shape[0] + norm_c.shape[0]
    assert nt % KV_TILE == 0 and nt >= WIN_KEYS and nt // CMP_BLOCK <= LANE
    assert past_len // CMP_BLOCK < LANE and x_sample.shape[1] <= SAMPLE_ROWS and n_pages <= N_PAGES_MAX

    cache_t = jnp.transpose(cache_nsa, (0, 1, 3, 4, 5, 2))
    win_t = jnp.transpose(cache_win, (0, 1, 3, 4, 5, 2))
    pt_flat = page_table.reshape(-1).astype(jnp.int32)

    blk_of_key = lambda n: (np.arange(n) // CMP_BLOCK)[None, :] == np.arange(LANE)[:, None]
    rows_s = SAMPLE_ROWS
    stack = lambda t: t.reshape(N_HEADS * rows_s, t.shape[-1])
    by_group = lambda t: t.reshape(N_KV, GROUP_ROWS, t.shape[-1])
    assert NEAR_SPAN + LANE - (KV_TILE - 1) >= _UPPER[-1] and nt % KV_TILE == 0
    tmpl = dict(
        far_lanes=_far_bias_lanes(rel_bias),
        e_prompt=jnp.asarray(np.where(blk_of_key(nt), NEG, 0.0), BF16),
        e_sample=jnp.asarray(np.where(blk_of_key(past_len), NEG, 0.0), BF16),
        cmp=_bias_template(rel_bias, Q_BLOCK, LANE, -CMP_BLOCK, CMP_BLOCK * (LANE - 2) - (CMP_BLOCK - 1)),
        near=by_group(_bias_template(rel_bias, Q_BLOCK, NEAR_SPAN + KV_TILE, -1, NEAR_SPAN)),
        win=by_group(_bias_template(rel_bias, Q_BLOCK, WINDOW + WIN_KEYS, -1, WINDOW, hi=WINDOW)),
        s_cmp=stack(_bias_template(rel_bias, rows_s, LANE, -CMP_BLOCK, past_len - (CMP_BLOCK - 1))),
        s_sel=stack(_bias_template(rel_bias, rows_s, past_len, -1, past_len)),
        s_new=stack(_bias_template(rel_bias, rows_s, LANE, -1, 0, cmax=x_sample.shape[1])),
        s_win=stack(_bias_template(rel_bias, rows_s, wlen, -1, wlen, hi=WINDOW)),
    )

    yp, ys = x_prompt, x_sample
    outs = [[] for _ in range(8)]
    nsa_all = None
    for layer in range(depth):
        if layer % 2 == 0:
            yp, ys, nsa_all, *leaves = _even_layer(layer // 2, yp, ys, nsa_all, cache_t, win_t, pt_flat, n_pages,
                                                   tmpl, p)
            for dst, leaf in zip(outs[:4], leaves):
                dst.append(leaf)
        else:
            c = layer // 2
            yp, ys, *leaves = _odd_layer(c, yp, ys, state_lru_h[c], state_lru_conv[c], p, final=layer == depth - 1)
            for dst, leaf in zip(outs[4:], leaves):
                dst.append(leaf)
    nsa_p = jnp.transpose(nsa_all.reshape(nsa_all.shape[0], nb, 4, N_KV, HEAD_DIM, nt), (0, 1, 5, 2, 3, 4))
    return (yp, ys, nsa_p) + tuple(jnp.stack(o) for o in outs)
```

```python
import functools
import math

import numpy as np
import jax
import jax.numpy as jnp
from jax import lax
from jax.experimental import pallas as pl
from jax.experimental.pallas import tpu as pltpu

F32 = jnp.float32
BF16 = jnp.bfloat16

D_MODEL = 1024
N_HEADS = 8
N_KV = 2
HPG = N_HEADS // N_KV
HEAD_DIM = 64
NSA_WIDTH = N_HEADS * HEAD_DIM
KV_WIDTH = N_KV * HEAD_DIM
CMP_BLOCK = 64
N_SEL = 16
WINDOW = 512
GMLP_GROUPS = 4
GMLP_WIDTH = 512
GMLP_CHUNK = 128
LRU_WIDTH = 1024
LRU_HEADS = 8
LRU_BLOCK = 128
CONV_WIDTH = 4
LRU_C = 8.0
REL_BUCKETS = 32
REL_MAX_DIST = 1024
EPS = 1e-6
NEG = -1e30
PAGE = 128

LANE = 128
Q_BLOCK = 128
KV_TILE = 512
NEAR_SPAN = 1280
WIN_KEYS = WINDOW + Q_BLOCK
VMEM_LIMIT = 56 * 1024 * 1024

QPAD = N_HEADS * LANE
C_KV = QPAD
C_Z = C_KV + 6 * KV_WIDTH
C_U = C_Z + NSA_WIDTH
C_V = C_U + GMLP_WIDTH
C_ZG = C_V + GMLP_WIDTH
C_G = C_ZG + GMLP_WIDTH
PROJ_COLS = C_G + LANE


def _nt(a, b):
    return lax.dot_general(a, b, (((1,), (1,)), ((), ())), preferred_element_type=F32)


def _mm(a, b):
    return jnp.dot(a, b, preferred_element_type=F32)


def _params(sem):
    return pltpu.CompilerParams(dimension_semantics=sem, vmem_limit_bytes=VMEM_LIMIT)


def _const_spec(shape):
    n = len(shape)
    return pl.BlockSpec(shape, lambda *_: (0,) * n)


def _bucket_upper_bounds():
    max_exact = REL_BUCKETS // 2
    d = np.arange(0, 4 * REL_MAX_DIST, dtype=np.int64)

    def buckets(ft):
        df = np.maximum(d, 1).astype(ft)
        large = max_exact + (np.log(df / ft(max_exact)) / ft(math.log(REL_MAX_DIST / max_exact))
                             * ft(REL_BUCKETS - max_exact)).astype(np.int32)
        return np.where(d < max_exact, d, np.minimum(large, REL_BUCKETS - 1))

    b32, b64 = buckets(np.float32), buckets(np.float64)
    assert (b32 == b64).all() and (np.diff(b32) >= 0).all() and b32[-1] == REL_BUCKETS - 1
    return [int(np.argmax(b32 > k)) for k in range(REL_BUCKETS - 1)]


_UPPER = _bucket_upper_bounds()


def _bias_tmpl_body(tab_ref, o_ref, *, rows, cs, off, hi, cmax):
    h = pl.program_id(0)
    t = lax.broadcasted_iota(jnp.int32, (rows, LANE), 0)
    lane = lax.broadcasted_iota(jnp.int32, (rows, LANE), 1)

    def chunk(k, carry):
        c0 = pl.multiple_of(k * LANE, LANE)
        c = lane + c0
        d = t + cs * c + off
        dd = jnp.maximum(d, 0)
        val = jnp.full((rows, LANE), tab_ref[REL_BUCKETS - 1, h], F32)
        for b in range(REL_BUCKETS - 2, -1, -1):
            val = jnp.where(dd < _UPPER[b], tab_ref[b, h], val)
        ok = (d >= 0) & (d <= hi) & (c < cmax)
        o_ref[0, :, pl.ds(c0, LANE)] = jnp.where(ok, val, NEG)
        return carry

    lax.fori_loop(0, o_ref.shape[2] // LANE, chunk, 0)


def _bias_template(rel_bias, rows, width, cs, off, hi=1 << 30, cmax=1 << 30):
    return pl.pallas_call(
        functools.partial(_bias_tmpl_body, rows=rows, cs=cs, off=off, hi=hi, cmax=cmax),
        grid=(N_HEADS,),
        in_specs=[pl.BlockSpec(memory_space=pltpu.SMEM)],
        out_specs=pl.BlockSpec((1, rows, width), lambda h: (h, 0, 0)),
        out_shape=jax.ShapeDtypeStruct((N_HEADS, rows, width), F32),
        compiler_params=_params(("arbitrary",)),
        name="bias_template",
    )(rel_bias)


def _rms(x, g):
    return x * lax.rsqrt(jnp.mean(x * x, axis=-1, keepdims=True) + EPS) * g


def _proj_a_body(x_ref, g_ref, w_ref, lng_ref, lnb_ref, *rest, feature_major, n_prev):
    if n_prev:
        prev_ref, rest = rest[0], rest[1:]
    q_ref, rest = rest[0], rest[1:]
    xn = _rms(x_ref[0], g_ref[...])
    y = _mm(xn.astype(BF16), w_ref[...])
    q_ref[0] = (y[:, :QPAD] * (HEAD_DIM ** -0.5)).astype(q_ref.dtype)
    kv = y[:, C_KV:C_Z]
    if feature_major:
        nsa_ref, win_ref, kvtb_ref, gl_ref, z_ref, u_ref, vn_ref, zg_ref = rest
        kvt = kv.T
        if n_prev:
            nsa_ref[0:n_prev, 0] = prev_ref[:, 0]
        nsa_ref[n_prev, 0] = kvt[:4 * KV_WIDTH]
        win_ref[0] = kvt[4 * KV_WIDTH:]
        kvtb_ref[0] = kvt.astype(BF16)
    else:
        kv_ref, gl_ref, z_ref, u_ref, vn_ref, zg_ref = rest
        kv_ref[0] = kv
    z_ref[0] = y[:, C_Z:C_U]
    u_ref[0] = y[:, C_U:C_V]
    v = y[:, C_V:C_ZG]
    mu = jnp.mean(v, axis=-1, keepdims=True)
    var = jnp.mean(jnp.square(v - mu), axis=-1, keepdims=True)
    vn_ref[0] = (v - mu) * lax.rsqrt(var + EPS) * lng_ref[...] + lnb_ref[...]
    zg_ref[0] = y[:, C_ZG:C_G]
    gl_ref[0] = y[:, C_G:]


def _proj_a(x, norm_g, w, ln_g, ln_b, tm, feature_major, prev_nsa=None):
    nb, nt, _ = x.shape
    n_prev = 0 if prev_nsa is None else prev_nsa.shape[0]
    row = lambda width: pl.BlockSpec((1, tm, width), lambda b, i: (b, i, 0))
    colT = pl.BlockSpec((1, 6 * KV_WIDTH, tm), lambda b, i: (b, 0, i))
    sds = lambda shape, dt=F32: jax.ShapeDtypeStruct(shape, dt)
    once = lambda shape: pl.BlockSpec(shape, lambda b, i: (0,) * len(shape), pipeline_mode=pl.Buffered(1))
    layers = lambda n: pl.BlockSpec((n, 1, 4 * KV_WIDTH, tm), lambda b, i: (0, b, 0, i))
    if feature_major:
        col = lambda r: pl.BlockSpec((1, r, tm), lambda b, i: (b, 0, i))
        kv_specs = [layers(n_prev + 1), col(2 * KV_WIDTH), colT]
        kv_shapes = [sds((n_prev + 1, nb, 4 * KV_WIDTH, nt)), sds((nb, 2 * KV_WIDTH, nt)),
                     sds((nb, 6 * KV_WIDTH, nt), BF16)]
    else:
        kv_specs, kv_shapes = [row(6 * KV_WIDTH)], [sds((nb, nt, 6 * KV_WIDTH))]
    prev_args, prev_specs = ([prev_nsa], [layers(n_prev)]) if n_prev else ([], [])
    return pl.pallas_call(
        functools.partial(_proj_a_body, feature_major=feature_major, n_prev=n_prev),
        grid=(nb, nt // tm),
        in_specs=[row(D_MODEL), once((1, D_MODEL)), once((D_MODEL, PROJ_COLS)),
                  once((1, GMLP_WIDTH)), once((1, GMLP_WIDTH))] + prev_specs,
        out_specs=[row(QPAD)] + kv_specs + [row(LANE), row(NSA_WIDTH), row(GMLP_WIDTH), row(GMLP_WIDTH),
                                            row(GMLP_WIDTH)],
        out_shape=[sds((nb, nt, QPAD), BF16 if feature_major else F32)] + kv_shapes
                  + [sds((nb, nt, LANE)), sds((nb, nt, NSA_WIDTH)), sds((nb, nt, GMLP_WIDTH)),
                     sds((nb, nt, GMLP_WIDTH)), sds((nb, nt, GMLP_WIDTH))],
        compiler_params=_params(("arbitrary", "arbitrary")),
        name="proj_a",
    )(x, norm_g, w, ln_g, ln_b, *prev_args)


def _split_bf16(x):
    hi = x.astype(BF16)
    return hi, (x - hi.astype(F32)).astype(BF16)


def _cmp_prompt_body(kvt_ref, pe_ref, w_ref, o_ref):
    nt = kvt_ref.shape[2]
    blk = lax.broadcasted_iota(jnp.int32, (LANE, nt), 0)
    pos = lax.broadcasted_iota(jnp.int32, (LANE, nt), 1)
    pool = jnp.where(pos // CMP_BLOCK == blk, 1.0 / CMP_BLOCK, 0.0).astype(BF16)
    hi, lo = _split_bf16(kvt_ref[0])
    mean = _nt(pool, hi) + _nt(pool, lo)
    mean = mean + jnp.mean(pe_ref[...], axis=0, keepdims=True)
    o_ref[0] = _mm(mean.astype(BF16), w_ref[...])


def _cmp_prompt(kvt, pe_rows, w_bd):
    n_layers, nb, _, nt = kvt.shape
    return pl.pallas_call(
        _cmp_prompt_body,
        grid=(nb,),
        in_specs=[pl.BlockSpec((None, 1, 2 * KV_WIDTH, nt), lambda b: (n_layers - 1, b, 0, 0)),
                  _const_spec((CMP_BLOCK, 2 * KV_WIDTH)), _const_spec((2 * KV_WIDTH, 2 * KV_WIDTH))],
        out_specs=pl.BlockSpec((1, LANE, 2 * KV_WIDTH), lambda b: (b, 0, 0)),
        out_shape=jax.ShapeDtypeStruct((nb, LANE, 2 * KV_WIDTH), F32),
        compiler_params=_params(("arbitrary",)),
        name="cmp_prompt",
    )(kvt, pe_rows, w_bd)


def _select_blocks(score_t, cur):
    shape = score_t.shape
    blk = lax.broadcasted_iota(jnp.int32, shape, 0)
    forced = (blk == 0) | (blk == cur) | (blk == cur - 1)
    score = jnp.where(forced | (blk >= cur), -jnp.inf, score_t)

    def pick(_, carry):
        score, unsel = carry
        mx = jnp.max(score, axis=0, keepdims=True)
        idx = jnp.min(jnp.where(score == mx, blk, shape[0]), axis=0, keepdims=True)
        hit = blk == idx
        unsel = jnp.where(hit & (mx > -jnp.inf), 0.0, unsel)
        return jnp.where(hit, -jnp.inf, score), unsel

    _, unsel = lax.fori_loop(0, N_SEL - 3, pick, (score, jnp.where(forced, 0.0, 1.0)))
    return unsel


def _softmax_rows(s):
    m = jnp.max(s, axis=-1, keepdims=True)
    e = jnp.exp(s - m)
    return e / jnp.sum(e, axis=-1, keepdims=True)


GROUP_ROWS = HPG * Q_BLOCK


def _rowmax(s):
    m = s[:, :LANE]
    for k in range(1, s.shape[1] // LANE):
        m = jnp.maximum(m, s[:, k * LANE:(k + 1) * LANE])
    return jnp.broadcast_to(jnp.max(m, axis=-1, keepdims=True), (s.shape[0], LANE))


def _tile_lanes(x, n):
    return jnp.concatenate([x] * n, axis=1)


def _attn_prompt_body(q_ref, bl_ref, cmp_ref, ka_ref, va_ref, kw_ref, vw_ref, gl_ref, bg_ref,
                      tc_ref, tn_ref, tw_ref, z_ref, u_ref, vn_ref, zg_ref, x_ref, wsp_ref, bsp_ref, wo_ref,
                      y_ref, lhs_scr, m_scr, acc_scr, part_scr, gsel_scr, ypart_scr):
    i = pl.program_id(1)
    qf = q_ref[0]

    y_g = u_ref[0] * _gmlp_mix(vn_ref[0].astype(BF16), wsp_ref, bsp_ref) * _silu(zg_ref[0])
    ypart_scr[...] = x_ref[0] + _mm(y_g.astype(BF16), wo_ref[NSA_WIDTH:])
    lane = lax.broadcasted_iota(jnp.int32, (Q_BLOCK, LANE), 1)
    stack = lambda x, g: jnp.concatenate([x[:, h * LANE:(h + 1) * LANE] for h in range(g * HPG, (g + 1) * HPG)], axis=0)
    qs = [stack(qf, g) for g in range(N_KV)]
    qs_far = [stack(qf + bl_ref[...], g) for g in range(N_KV)]

    kc = cmp_ref[0, :, :KV_WIDTH].astype(BF16)
    vc = cmp_ref[0, :, KV_WIDTH:].astype(BF16)
    seen = lane <= 2 * i + 1
    shift = (2 * i + 2) % LANE
    o_cmp, p_sum = [], []
    for g in range(N_KV):
        tmpl = jnp.concatenate([jnp.where(seen, pltpu.roll(tc_ref[h], shift, axis=1), NEG)
                                for h in range(g * HPG, (g + 1) * HPG)], axis=0)
        ok = tmpl > 0.5 * NEG
        p = jnp.where(ok, _softmax_rows(_nt(qs[g], kc) + tmpl), 0.0)
        o_cmp.append(_mm(p.astype(BF16), vc))
        p_sum.append(p.reshape(HPG, Q_BLOCK, LANE).sum(axis=0))

    w0 = pl.multiple_of(jnp.maximum(i * Q_BLOCK - WINDOW, 0), LANE)
    woff = pl.multiple_of(WINDOW - (i * Q_BLOCK - w0), LANE)
    kwt = kw_ref[0, :, pl.ds(w0, WIN_KEYS)]
    gate = jax.nn.sigmoid(gl_ref[0] + bg_ref[...])
    for g in range(N_KV):
        col = lambda k: jnp.concatenate([gate[:, k * N_HEADS + h:k * N_HEADS + h + 1]
                                         for h in range(g * HPG, (g + 1) * HPG)], axis=0)
        s = _mm(qs[g], kwt) + tw_ref[g, :, pl.ds(woff, WIN_KEYS)]
        e = jnp.exp(s - _tile_lanes(_rowmax(s), WIN_KEYS // LANE)).astype(BF16)
        pv = _nt(e, vw_ref[0, g, :, pl.ds(w0, WIN_KEYS)])
        part_scr[g] = col(0) * o_cmp[g] + col(2) * (pv / pltpu.roll(pv, HEAD_DIM, axis=1))
        gsel_scr[g] = jnp.broadcast_to(col(1), (GROUP_ROWS, LANE))

    tpos = i * Q_BLOCK + lax.broadcasted_iota(jnp.int32, (1, N_KV * Q_BLOCK), 1) % Q_BLOCK
    score_t = jnp.concatenate([ps.T for ps in p_sum], axis=1)
    cur = tpos // CMP_BLOCK
    half = LANE // 2

    def few_blocks():
        return jnp.concatenate([_select_blocks(score_t[:half], cur), jnp.ones((LANE - half, N_KV * Q_BLOCK), F32)],
                               axis=0)

    unsel_t = lax.cond((i + 1) * Q_BLOCK <= half * CMP_BLOCK, few_blocks, lambda: _select_blocks(score_t, cur))
    for g in range(N_KV):
        un = _tile_rows(unsel_t[:, g * Q_BLOCK:(g + 1) * Q_BLOCK].T.astype(BF16), HPG)
        lhs_scr[0, g] = jnp.concatenate([qs_far[g], un], axis=1)
        lhs_scr[1, g] = jnp.concatenate([qs[g], un], axis=1)

    m_scr[...] = jnp.full(m_scr.shape, NEG, F32)
    acc_scr[...] = jnp.zeros(acc_scr.shape, F32)

    def tile(j, near):
        k0 = pl.multiple_of(j * KV_TILE, KV_TILE)
        s = [_mm(lhs_scr[1 if near else 0, g], ka_ref[0, g, :, pl.ds(k0, KV_TILE)]) for g in range(N_KV)]
        if near:
            off = pl.multiple_of(NEAR_SPAN - (i * Q_BLOCK - j * KV_TILE), LANE)
            s = [s[g] + tn_ref[g, :, pl.ds(off, KV_TILE)] for g in range(N_KV)]
        m_old = [m_scr[g] for g in range(N_KV)]
        m_new = [jnp.maximum(m_old[g], _rowmax(s[g])) for g in range(N_KV)]
        p = [jnp.exp(s[g] - _tile_lanes(m_new[g], KV_TILE // LANE)).astype(BF16) for g in range(N_KV)]
        for g in range(N_KV):
            acc_scr[g] = jnp.exp(m_old[g] - m_new[g]) * acc_scr[g] + _nt(p[g], va_ref[0, g, :, pl.ds(k0, KV_TILE)])
            m_scr[g] = m_new[g]

    n_far = jnp.maximum(i * Q_BLOCK - (NEAR_SPAN + LANE) + KV_TILE, 0) // KV_TILE
    n_tiles = i // (KV_TILE // Q_BLOCK) + 1
    def far_quad(jj, c):
        for k in range(4):
            tile(4 * jj + k, False)
        return c

    lax.fori_loop(0, n_far // 4, far_quad, 0)
    done = n_far // 4 * 4

    @pl.when(n_far - done >= 2)
    def _():
        tile(done, False)
        tile(done + 1, False)

    @pl.when(n_far % 2 == 1)
    def _():
        tile(n_far - 1, False)

    def near_pair(jj, c):
        tile(n_far + 2 * jj, True)
        tile(n_far + 2 * jj + 1, True)
        return c

    n_near = n_tiles - n_far
    lax.fori_loop(0, n_near // 2, near_pair, 0)

    @pl.when(n_near % 2 == 1)
    def _():
        tile(n_tiles - 1, True)

    merged = []
    for g in range(N_KV):
        acc = acc_scr[g]
        merged.append(part_scr[g] + gsel_scr[g] * (acc / pltpu.roll(acc, HEAD_DIM, axis=1)))
    y_nsa = []
    for c in range(HPG):
        rows = slice(c * Q_BLOCK, (c + 1) * Q_BLOCK)
        o = jnp.where(lane < HEAD_DIM, merged[0][rows], merged[1][rows])
        y_nsa.append((o * _silu(z_ref[0, :, c * LANE:(c + 1) * LANE])).astype(BF16))
    y_ref[0] = ypart_scr[...] + _mm(jnp.concatenate(y_nsa, axis=-1), wo_ref[:NSA_WIDTH])


def _tile_rows(x, n):
    return jnp.concatenate([x] * n, axis=0)


def _attn_prompt(q, bias_lanes, cmp, kaug, vaug, kwin, vwaug, gl, bg, t_cmp, t_near, t_win,
                 z, u, vn, zg, x, wsp, bsp, wo):
    nb, nt, _ = q.shape
    assert wsp.shape[1] == Q_BLOCK
    once = pl.Buffered(1)
    per_b = lambda a: pl.BlockSpec((1,) + a.shape[1:], lambda b, i: (b,) + (0,) * (a.ndim - 1), pipeline_mode=once)
    whole = lambda a: pl.BlockSpec(a.shape, lambda b, i: (0,) * a.ndim, pipeline_mode=once)
    row = lambda width: pl.BlockSpec((1, Q_BLOCK, width), lambda b, i: (b, i, 0))
    return pl.pallas_call(
        _attn_prompt_body,
        grid=(nb, nt // Q_BLOCK),
        in_specs=[row(QPAD), whole(bias_lanes), per_b(cmp), per_b(kaug), per_b(vaug), per_b(kwin), per_b(vwaug),
                  row(LANE), _const_spec((1, LANE)), whole(t_cmp), whole(t_near), whole(t_win),
                  row(NSA_WIDTH), row(GMLP_WIDTH), row(GMLP_WIDTH), row(GMLP_WIDTH), row(D_MODEL),
                  whole(wsp), whole(bsp), whole(wo)],
        out_specs=row(D_MODEL),
        out_shape=jax.ShapeDtypeStruct((nb, nt, D_MODEL), F32),
        scratch_shapes=[pltpu.VMEM((2, N_KV, GROUP_ROWS, 2 * LANE), BF16)]
                       + [pltpu.VMEM((N_KV, GROUP_ROWS, LANE), F32)] * 4 + [pltpu.VMEM((Q_BLOCK, D_MODEL), F32)],
        compiler_params=_params(("arbitrary", "arbitrary")),
        name="attn_prompt",
    )(q, bias_lanes, cmp, kaug, vaug, kwin, vwaug, gl, bg, t_cmp, t_near, t_win, z, u, vn, zg, x, wsp, bsp, wo)


def _prompt_kv_operands(kvtb, emask):
    nb, _, nt = kvtb.shape
    half = lambda kind, g: kvtb[:, kind * KV_WIDTH + g * HEAD_DIM:kind * KV_WIDTH + (g + 1) * HEAD_DIM]
    ones = jnp.ones((nb, HEAD_DIM, nt), BF16)
    extra = ones.at[:, 2:].set(0)
    e = jnp.broadcast_to(emask[None], (nb, LANE, nt))
    kaug = jnp.stack([jnp.concatenate([half(2, 0), extra, e], axis=1),
                      jnp.concatenate([extra, half(2, 1), e], axis=1)], axis=1)
    aug = lambda kind: jnp.stack([jnp.concatenate([half(kind, 0), ones], axis=1),
                                  jnp.concatenate([ones, half(kind, 1)], axis=1)], axis=1)
    return kaug, aug(3), kvtb[:, 4 * KV_WIDTH:5 * KV_WIDTH], aug(5)


def _far_bias_lanes(rel_bias):
    b = rel_bias[REL_BUCKETS - 1]
    hi = b.astype(BF16).astype(F32)
    pair = jnp.stack([hi, b - hi], axis=1)
    tile = jnp.zeros((N_HEADS, LANE), F32)
    first = tile.at[:, HEAD_DIM:HEAD_DIM + 2].set(pair)
    second = tile.at[:, 0:2].set(pair)
    return jnp.where((np.arange(N_HEADS) < HPG)[:, None], first, second).reshape(1, QPAD).astype(BF16)


def _silu(x):
    return x * jax.nn.sigmoid(x)


def _gmlp_mix(vn, wsp_ref, bsp_ref):
    chunk = wsp_ref.shape[1]
    r = lax.broadcasted_iota(jnp.int32, (chunk, chunk), 0)
    c = lax.broadcasted_iota(jnp.int32, (chunk, chunk), 1)
    mix = []
    for g in range(GMLP_GROUPS):
        wm = jnp.where(c <= r, wsp_ref[g], 0.0).astype(BF16)
        mix.append(jnp.concatenate(
            [_mm(wm, vn[k * chunk:(k + 1) * chunk, g * LANE:(g + 1) * LANE]) + bsp_ref[:, g:g + 1]
             for k in range(vn.shape[0] // chunk)], axis=0))
    return jnp.concatenate(mix, axis=-1)


def _post_a_body(o_ref, z_ref, u_ref, vn_ref, zg_ref, x_ref, wsp_ref, bsp_ref, wo_ref, y_ref, *, short_chunks):
    if short_chunks:
        vn = vn_ref[0]
        mix = []
        for g in range(GMLP_GROUPS):
            vg = vn[:, g * LANE:(g + 1) * LANE]
            acc = bsp_ref[:, g:g + 1] + wsp_ref[0][:, g:g + 1] * vg
            for k in range(1, wsp_ref.shape[0]):
                acc = acc + wsp_ref[k][:, g:g + 1] * pltpu.roll(vg, k, axis=0)
            mix.append(acc)
        mix = jnp.concatenate(mix, axis=-1)
    else:
        mix = _gmlp_mix(vn_ref[0].astype(BF16), wsp_ref, bsp_ref)
    y_g = u_ref[0] * mix * _silu(zg_ref[0])
    y_nsa = o_ref[0] * _silu(z_ref[0])
    cat = jnp.concatenate([y_nsa, y_g], axis=-1).astype(BF16)
    y_ref[0] = x_ref[0] + _mm(cat, wo_ref[...])


def _post_a(o, z, u, vn, zg, x, wsp, bsp, wo, short_chunks=False):
    nb, nt, _ = x.shape
    rows = wsp.shape[1]
    row = lambda width: pl.BlockSpec((1, rows, width), lambda b, i: (b, i, 0))
    return pl.pallas_call(
        functools.partial(_post_a_body, short_chunks=short_chunks),
        grid=(nb, nt // rows),
        in_specs=[row(NSA_WIDTH), row(NSA_WIDTH), row(GMLP_WIDTH), row(GMLP_WIDTH), row(GMLP_WIDTH),
                  row(D_MODEL), _const_spec(wsp.shape), _const_spec(bsp.shape), _const_spec(wo.shape)],
        out_specs=row(D_MODEL),
        out_shape=jax.ShapeDtypeStruct(x.shape, F32),
        compiler_params=_params(("arbitrary", "arbitrary")),
        name="post_a",
    )(o, z, u, vn, zg, x, wsp, bsp, wo)


SAMPLE_ROWS = 8
N_PAGES_MAX = 16


SEQS_PER_STEP = 2


def _stack_heads(q_ref, s):
    return jnp.concatenate([q_ref[s, :, h * LANE:(h + 1) * LANE] for h in range(N_HEADS)], axis=0).astype(BF16)


def _pad_rows(x, rows):
    return jnp.concatenate([x, jnp.zeros((rows - x.shape[0], x.shape[1]), x.dtype)], axis=0)


def _cmp_sample_body(pt_ref, q_ref, kv_ref, pe_ref, wk_ref, wv_ref, tc_ref, pool_ref, *rest):
    n_pages = (len(rest) - 4) // SEQS_PER_STEP
    for s in range(SEQS_PER_STEP):
        _cmp_sample_one(s, rest[s * n_pages:(s + 1) * n_pages], q_ref, kv_ref, pe_ref, wk_ref, wv_ref, tc_ref,
                        pool_ref, *rest[-4:])


def _cmp_sample_one(s, pages, q_ref, kv_ref, pe_ref, wk_ref, wv_ref, tc_ref, pool_ref, oc_ref, ps_ref, ks_ref, vs_ref):
    n_pages = len(pages)
    lane = lax.broadcasted_iota(jnp.int32, (KV_WIDTH, LANE), 1)
    eye = lax.broadcasted_iota(jnp.int32, (KV_WIDTH, LANE), 0) == lane
    qs = _stack_heads(q_ref, s)
    for p, pg in enumerate(pages):
        ks_ref[s, :, p * PAGE:(p + 1) * PAGE] = pg[2].reshape(KV_WIDTH, PAGE).astype(BF16)
        vs_ref[s, :, p * PAGE:(p + 1) * PAGE] = pg[3].reshape(KV_WIDTH, PAGE).astype(BF16)
    summ = []
    for kind in range(2):
        sums = jnp.zeros((KV_WIDTH, LANE), F32)
        for p in range(0, n_pages, 2):
            x = jnp.concatenate([pages[p][kind].reshape(KV_WIDTH, PAGE), pages[p + 1][kind].reshape(KV_WIDTH, PAGE)],
                                axis=1)
            hi, lo = _split_bf16(x)
            pool = pool_ref[p * PAGE:(p + 2) * PAGE]
            sums = sums + _mm(hi, pool) + _mm(lo, pool)
        new = jnp.sum(kv_ref[s, :, kind * KV_WIDTH:(kind + 1) * KV_WIDTH], axis=0, keepdims=True)
        new_col = jnp.sum(jnp.where(eye, new, 0.0), axis=1, keepdims=True)
        sums = jnp.where(lane == 2 * n_pages, new_col, sums)
        mean_t = sums * (1.0 / CMP_BLOCK) + jnp.mean(pe_ref[kind], axis=1, keepdims=True)
        w_ref = wk_ref if kind == 0 else wv_ref
        summ.append(_mm(w_ref[...], mean_t.astype(BF16)).astype(BF16))
    tmpl = tc_ref[...]
    ok = tmpl > 0.5 * NEG
    p = jnp.where(ok, _softmax_rows(_mm(qs, summ[0]) + tmpl), 0.0)
    oc_ref[s] = _nt(p.astype(BF16), summ[1])
    ps_ref[s] = p.reshape(N_KV, HPG, SAMPLE_ROWS, LANE).sum(axis=1).reshape(N_KV * SAMPLE_ROWS, LANE)


def _page_specs(layer, n_pages):
    def spec(s, p):
        return pl.BlockSpec((None, None, 4, N_KV, HEAD_DIM, PAGE),
                            lambda b, pt: (layer, pt[(b * SEQS_PER_STEP + s) * n_pages + p], 0, 0, 0, 0))
    return [spec(s, p) for s in range(SEQS_PER_STEP) for p in range(n_pages)]


def _cmp_sample(layer, pt_flat, n_pages, cache_t, q, kv, pe_t, wk_t, wv_t, t_cmp):
    nseq = q.shape[0]
    rows = N_HEADS * SAMPLE_ROWS
    past = n_pages * PAGE
    seq = lambda r, w: pl.BlockSpec((SEQS_PER_STEP, r, w), lambda b, pt: (b, 0, 0))
    cst = lambda a: pl.BlockSpec(a.shape, lambda b, pt: (0,) * a.ndim)
    assert n_pages % 2 == 0 and nseq % SEQS_PER_STEP == 0
    pool = jnp.asarray((np.arange(past) // CMP_BLOCK)[:, None] == np.arange(LANE)[None, :], BF16)
    return pl.pallas_call(
        _cmp_sample_body,
        grid_spec=pltpu.PrefetchScalarGridSpec(
            num_scalar_prefetch=1, grid=(nseq // SEQS_PER_STEP,),
            in_specs=[seq(SAMPLE_ROWS, QPAD), seq(SAMPLE_ROWS, 6 * KV_WIDTH), cst(pe_t), cst(wk_t), cst(wv_t),
                      cst(t_cmp), cst(pool)] + _page_specs(layer, n_pages),
            out_specs=[seq(rows, KV_WIDTH), seq(N_KV * SAMPLE_ROWS, LANE), seq(KV_WIDTH, past), seq(KV_WIDTH, past)]),
        out_shape=[jax.ShapeDtypeStruct((nseq, rows, KV_WIDTH), F32),
                   jax.ShapeDtypeStruct((nseq, N_KV * SAMPLE_ROWS, LANE), F32),
                   jax.ShapeDtypeStruct((nseq, KV_WIDTH, past), BF16),
                   jax.ShapeDtypeStruct((nseq, KV_WIDTH, past), BF16)],
        compiler_params=_params(("arbitrary",)),
        name="cmp_sample",
    )(pt_flat, q, kv, pe_t, wk_t, wv_t, t_cmp, pool, *([cache_t] * (SEQS_PER_STEP * n_pages)))


def _select_sample_body(ps_ref, o_ref, *, past_len):
    cols = ps_ref.shape[1]
    t = lax.broadcasted_iota(jnp.int32, (1, cols), 1) % SAMPLE_ROWS
    o_ref[...] = _select_blocks(ps_ref[...], (past_len + t) // CMP_BLOCK)


def _select_sample(ps_t, past_len):
    cols = ps_t.shape[1]
    tile = min(cols, 2 * LANE)
    spec = pl.BlockSpec((LANE, tile), lambda i: (0, i))
    return pl.pallas_call(
        functools.partial(_select_sample_body, past_len=past_len),
        grid=(cols // tile,), in_specs=[spec], out_specs=spec,
        out_shape=jax.ShapeDtypeStruct(ps_t.shape, F32),
        compiler_params=_params(("arbitrary",)),
        name="select_sample",
    )(ps_t)


def _attn_sample_body(*refs):
    for s in range(SEQS_PER_STEP):
        _attn_sample_one(s, *refs)


def _attn_sample_one(s, q_ref, kv_ref, un_ref, oc_ref, gl_ref, bg_ref, e_ref, ts_ref, tn_ref, tw_ref,
                     win_ref, ks_ref, vs_ref, o_ref):
    qs = _stack_heads(q_ref, s)
    lane = lax.broadcasted_iota(jnp.int32, (SAMPLE_ROWS, LANE), 1)
    tn = tn_ref[...]

    def new_rows(col):
        return _pad_rows(kv_ref[s, :, col * KV_WIDTH:(col + 1) * KV_WIDTH], LANE).astype(BF16)

    def branch(k_past, v_past, bias, k_col, v_col):
        s = _mm(qs, k_past) + bias
        s_new = _nt(qs, new_rows(k_col)) + tn
        m = jnp.maximum(jnp.max(s, axis=-1, keepdims=True), jnp.max(s_new, axis=-1, keepdims=True))
        p, p_new = jnp.exp(s - m), jnp.exp(s_new - m)
        den = jnp.sum(p, axis=-1, keepdims=True) + jnp.sum(p_new, axis=-1, keepdims=True)
        return (_nt(p.astype(BF16), v_past) + _mm(p_new.astype(BF16), new_rows(v_col))) / den

    madd = _mm(un_ref[s].astype(BF16), e_ref[...])
    madd = jnp.concatenate([madd[g * SAMPLE_ROWS:(g + 1) * SAMPLE_ROWS] for g in range(N_KV) for _ in range(HPG)],
                           axis=0)
    o_sel = branch(ks_ref[s], vs_ref[s], ts_ref[...] + madd, 2, 3)
    win = lambda kind: win_ref[s, kind].reshape(KV_WIDTH, win_ref.shape[-1]).astype(BF16)
    o_win = branch(win(0), win(1), tw_ref[...], 4, 5)

    gate = jax.nn.sigmoid(gl_ref[s] + bg_ref[...])
    oc = oc_ref[s]

    def merged(h):
        rows = slice(h * SAMPLE_ROWS, (h + 1) * SAMPLE_ROWS)
        col = lambda k: gate[:, k * N_HEADS + h:k * N_HEADS + h + 1]
        return col(0) * oc[rows] + col(1) * o_sel[rows] + col(2) * o_win[rows]

    for c in range(HPG):
        o_ref[s, :, c * LANE:(c + 1) * LANE] = jnp.where(lane < HEAD_DIM, merged(c), merged(HPG + c))


def _attn_sample(layer, win_t, q, kv, unsel, oc, gl, bg, emask, t_sel, t_new, t_win, ks, vs):
    nseq = q.shape[0]
    seq = lambda a: pl.BlockSpec((SEQS_PER_STEP,) + a.shape[1:], lambda b: (b,) + (0,) * (a.ndim - 1))
    cst = lambda a: pl.BlockSpec(a.shape, lambda b: (0,) * a.ndim)
    win_spec = pl.BlockSpec((None, SEQS_PER_STEP, 2, N_KV, HEAD_DIM, win_t.shape[-1]),
                            lambda b: (layer, b, 0, 0, 0, 0))
    return pl.pallas_call(
        _attn_sample_body,
        grid=(nseq // SEQS_PER_STEP,),
        in_specs=[seq(q), seq(kv), seq(unsel), seq(oc), seq(gl), cst(bg), cst(emask), cst(t_sel), cst(t_new),
                  cst(t_win), win_spec, seq(ks), seq(vs)],
        out_specs=pl.BlockSpec((SEQS_PER_STEP, SAMPLE_ROWS, NSA_WIDTH), lambda b: (b, 0, 0)),
        out_shape=jax.ShapeDtypeStruct((nseq, SAMPLE_ROWS, NSA_WIDTH), F32),
        compiler_params=_params(("arbitrary",)),
        name="attn_sample",
    )(q, kv, unsel, oc, gl, bg, emask, t_sel, t_new, t_win, win_t, ks, vs)


def _proj_c_body(x_ref, g_ref, w_ref, xb_ref, z_ref):
    y = _mm(_rms(x_ref[...], g_ref[...]).astype(BF16), w_ref[...])
    xb_ref[...] = y[:, :LRU_WIDTH]
    z_ref[...] = y[:, LRU_WIDTH:]


def _proj_c(x, norm_g, w, tm):
    n = x.shape[0]
    row = pl.BlockSpec((tm, LRU_WIDTH), lambda i: (i, 0))
    return pl.pallas_call(
        _proj_c_body,
        grid=(n // tm,),
        in_specs=[row, _const_spec((1, D_MODEL)), _const_spec((D_MODEL, 2 * LRU_WIDTH))],
        out_specs=[row, row],
        out_shape=[jax.ShapeDtypeStruct((n, LRU_WIDTH), F32)] * 2,
        compiler_params=_params(("arbitrary",)),
        name="proj_c",
    )(x, norm_g, w)


def _lru_coeffs(xc, wra_ref, bra_ref, wrx_ref, brx_ref, lam_ref):
    xb = xc.astype(BF16)
    blocks = lambda w_ref: jnp.concatenate(
        [_mm(xb[:, k * LRU_BLOCK:(k + 1) * LRU_BLOCK], w_ref[k]) for k in range(LRU_HEADS)], axis=-1)
    r = jax.nn.sigmoid(blocks(wra_ref) + bra_ref[...])
    ig = jax.nn.sigmoid(blocks(wrx_ref) + brx_ref[...])
    log_a = -LRU_C * r * jax.nn.softplus(-lam_ref[...])
    a = jnp.exp(log_a)
    return a, jnp.sqrt(jnp.tanh(-log_a) * (1.0 + a * a)) * ig * xc


def _mix_c_prompt_body(x_ref, ng_ref, wi_ref, cw_ref, cb_ref, wra_ref, bra_ref, wrx_ref, brx_ref, lam_ref, wo_ref,
                       fg_ref, y_ref, hl_ref, tail_ref, xin_scr, a_scr, b_scr, z_scr, h_scr, *, final_norm):
    tt = x_ref.shape[1]
    pad = 8
    hist = CONV_WIDTH - 1

    @pl.when(pl.program_id(1) == 0)
    def _():
        xin_scr[0:pad] = jnp.zeros((pad, LRU_WIDTH), F32)
        h_scr[...] = jnp.zeros(h_scr.shape, F32)

    proj = _mm(_rms(x_ref[0], ng_ref[...]).astype(BF16), wi_ref[...])
    xin_scr[pad:pad + tt] = proj[:, :LRU_WIDTH]
    z_scr[...] = proj[:, LRU_WIDTH:]
    xc = cb_ref[...] + xin_scr[pad - hist:pad - hist + tt] * cw_ref[0:1]
    for k in range(1, CONV_WIDTH):
        xc = xc + xin_scr[pad - hist + k:pad - hist + k + tt] * cw_ref[k:k + 1]
    a, b = _lru_coeffs(xc, wra_ref, bra_ref, wrx_ref, brx_ref, lam_ref)
    a_scr[...] = a
    b_scr[...] = b

    row = lax.broadcasted_iota(jnp.int32, (8, LRU_WIDTH), 0)

    def step(k, h_prev):
        r0 = pl.multiple_of(k * 8, 8)
        ca, cbv = a_scr[pl.ds(r0, 8)], b_scr[pl.ds(r0, 8)]
        for s in (1, 2, 4):
            keep = row >= s
            cbv = cbv + ca * jnp.where(keep, pltpu.roll(cbv, s, axis=0), 0.0)
            ca = ca * jnp.where(keep, pltpu.roll(ca, s, axis=0), 1.0)
        hs = ca * h_prev + cbv
        b_scr[pl.ds(r0, 8)] = hs
        return jnp.broadcast_to(hs[7:8], (8, LRU_WIDTH))

    h_scr[...] = lax.fori_loop(0, tt // 8, step, h_scr[...])
    y = (b_scr[...] * _silu(z_scr[...])).astype(BF16)
    out = x_ref[0] + _mm(y, wo_ref[...])
    y_ref[0] = _rms(out, fg_ref[...]) if final_norm else out
    hl_ref[0] = h_scr[0:1]
    tail_ref[0] = xin_scr[pad + tt - hist:pad + tt]
    xin_scr[0:pad] = xin_scr[tt:tt + pad]


def _mix_c_prompt(x, ng, wi, cw, cb, wra, bra, wrx, brx, lam, wo, fg, final_norm, tt=512):
    nb, nt, _ = x.shape
    row = pl.BlockSpec((1, tt, LRU_WIDTH), lambda b, i: (b, i, 0))
    cst = lambda a: pl.BlockSpec(a.shape, lambda b, i: (0,) * a.ndim, pipeline_mode=pl.Buffered(1))
    hist = CONV_WIDTH - 1
    return pl.pallas_call(
        functools.partial(_mix_c_prompt_body, final_norm=final_norm),
        grid=(nb, nt // tt),
        in_specs=[row, cst(ng), cst(wi), cst(cw), cst(cb), cst(wra), cst(bra), cst(wrx), cst(brx), cst(lam), cst(wo),
                  cst(fg)],
        out_specs=[row, pl.BlockSpec((1, 1, LRU_WIDTH), lambda b, i: (b, 0, 0)),
                   pl.BlockSpec((1, hist, LRU_WIDTH), lambda b, i: (b, 0, 0))],
        out_shape=[jax.ShapeDtypeStruct(x.shape, F32), jax.ShapeDtypeStruct((nb, 1, LRU_WIDTH), F32),
                   jax.ShapeDtypeStruct((nb, hist, LRU_WIDTH), F32)],
        scratch_shapes=[pltpu.VMEM((tt + 8, LRU_WIDTH), F32)] + [pltpu.VMEM((tt, LRU_WIDTH), F32)] * 3
                       + [pltpu.VMEM((8, LRU_WIDTH), F32)],
        compiler_params=_params(("arbitrary", "arbitrary")),
        name="mix_c_prompt",
    )(x, ng, wi, cw, cb, wra, bra, wrx, brx, lam, wo, fg)


def _mix_c_sample_body(xb_ref, z_ref, x_ref, h0_ref, c0_ref, cw_ref, cb_ref, wra_ref, bra_ref, wrx_ref, brx_ref,
                       lam_ref, wo_ref, fg_ref, y_ref, hl_ref, tail_ref, *, final_norm):
    nt = xb_ref.shape[0]
    hist = CONV_WIDTH - 1
    xin = [c0_ref[k] for k in range(hist)] + [xb_ref[t] for t in range(nt)]
    h = h0_ref[...]
    for t in range(nt):
        xc = cb_ref[...] + xin[t] * cw_ref[0:1]
        for k in range(1, CONV_WIDTH):
            xc = xc + xin[t + k] * cw_ref[k:k + 1]
        a, b = _lru_coeffs(xc, wra_ref, bra_ref, wrx_ref, brx_ref, lam_ref)
        h = a * h + b
        out = x_ref[t] + _mm((h * _silu(z_ref[t])).astype(BF16), wo_ref[...])
        y_ref[t] = _rms(out, fg_ref[...]) if final_norm else out
    hl_ref[...] = h
    for k in range(hist):
        tail_ref[k] = xin[nt + k]


def _mix_c_sample(xb, z, x, h0, c0, cw, cb, wra, bra, wrx, brx, lam, wo, fg, final_norm):
    args = (xb, z, x, h0, c0, cw, cb, wra, bra, wrx, brx, lam, wo, fg)
    return pl.pallas_call(
        functools.partial(_mix_c_sample_body, final_norm=final_norm),
        grid=(1,),
        in_specs=[_const_spec(a.shape) for a in args],
        out_specs=[_const_spec(x.shape), _const_spec(h0.shape), _const_spec(c0.shape)],
        out_shape=[jax.ShapeDtypeStruct(x.shape, F32), jax.ShapeDtypeStruct(h0.shape, F32),
                   jax.ShapeDtypeStruct(c0.shape, F32)],
        compiler_params=_params(("arbitrary",)),
        name="mix_c_sample",
    )(*args)


def _pair_perm():
    cols = []
    for c in range(HPG):
        cols += list(range(c * HEAD_DIM, (c + 1) * HEAD_DIM))
        cols += list(range((HPG + c) * HEAD_DIM, (HPG + c + 1) * HEAD_DIM))
    return np.asarray(cols)


def _layout_w_in_a(w):
    o_kv = NSA_WIDTH
    o_g = o_kv + 6 * KV_WIDTH
    o_z = o_g + 3 * N_HEADS
    o_rest = o_z + NSA_WIDTH
    wq = w[:, :NSA_WIDTH].reshape(D_MODEL, N_HEADS, HEAD_DIM)
    zeros = jnp.zeros_like(wq)
    first = (np.arange(N_HEADS) < HPG)[None, :, None]
    wq = jnp.concatenate([jnp.where(first, wq, zeros), jnp.where(first, zeros, wq)], axis=-1)
    wg = jnp.pad(w[:, o_g:o_z], ((0, 0), (0, LANE - 3 * N_HEADS)))
    wz = w[:, o_z:o_rest][:, _pair_perm()]
    return jnp.concatenate([wq.reshape(D_MODEL, QPAD), w[:, o_kv:o_g], wz, w[:, o_rest:], wg], axis=1).astype(BF16)


def _block_diag(w, n):
    return jnp.kron(jnp.eye(n, dtype=w.dtype), w)


def _even_layer(a, yp, ys, prev_nsa, cache_t, win_t, pt_flat, n_pages, tmpl, p):
    nb, nt, _ = yp.shape
    nseq, dec_t, _ = ys.shape
    past_len = n_pages * PAGE
    w_in = _layout_w_in_a(p['w_in_a'][a])
    norm_g = p['norm_a'][a][None]
    ln_g, ln_b = p['gmlp_ln_g'][a][None], p['gmlp_ln_b'][a][None]
    bg = jnp.pad(p['b_gate_a'][a], (0, LANE - 3 * N_HEADS))[None]
    w_out = jnp.concatenate([p['w_out_a'][a][:NSA_WIDTH][_pair_perm()], p['w_out_a'][a][NSA_WIDTH:]], axis=0).astype(BF16)
    pe, wc = p['pe_cmp'][a], p['w_cmp'][a]

    q, nsa_t, wrows_t, kvtb, gl, z, u, vn, zg = _proj_a(yp, norm_g, w_in, ln_g, ln_b, tm=min(512, nt),
                                                        feature_major=True, prev_nsa=prev_nsa)
    pe_rows = jnp.concatenate([pe[0], pe[0], pe[1], pe[1]], axis=1)
    w_bd = jnp.zeros((2 * KV_WIDTH, 2 * KV_WIDTH), F32)
    for k in range(4):
        w_bd = w_bd.at[k * HEAD_DIM:(k + 1) * HEAD_DIM, k * HEAD_DIM:(k + 1) * HEAD_DIM].set(wc[k // 2])
    cmp = _cmp_prompt(nsa_t, pe_rows, w_bd.astype(BF16))
    kaug, vaug, kwin, vwaug = _prompt_kv_operands(kvtb, tmpl['e_prompt'])
    yp_new = _attn_prompt(q, tmpl['far_lanes'], cmp, kaug, vaug, kwin, vwaug, gl, bg, tmpl['cmp'], tmpl['near'],
                          tmpl['win'], z, u, vn, zg, yp, p['w_spatial'][a], p['b_spatial'][a].T, w_out)
    wlen = min(WINDOW, nt)
    win_p = jnp.transpose(wrows_t[:, :, nt - wlen:].reshape(nb, 2, N_KV, HEAD_DIM, wlen), (0, 4, 1, 2, 3))

    xs = ys.reshape(1, nseq * dec_t, D_MODEL)
    q, kv, gl, z, u, vn, zg = _proj_a(xs, norm_g, w_in, ln_g, ln_b, tm=nseq * dec_t, feature_major=False)
    pad_t = lambda x: jnp.pad(x.reshape(nseq, dec_t, -1), ((0, 0), (0, SAMPLE_ROWS - dec_t), (0, 0)))
    q8, kv8, gl8 = pad_t(q), pad_t(kv), pad_t(gl)
    pe_t = jnp.concatenate([jnp.swapaxes(pe, 1, 2)] * N_KV, axis=1)
    wk_t = _block_diag(wc[0].T, N_KV).astype(BF16)
    wv_t = _block_diag(wc[1].T, N_KV).astype(BF16)
    oc, ps, ks, vs = _cmp_sample(a, pt_flat, n_pages, cache_t, q8, kv8, pe_t, wk_t, wv_t, tmpl['s_cmp'])
    unsel_t = _select_sample(ps.reshape(nseq * N_KV * SAMPLE_ROWS, LANE).T, past_len)
    unsel = unsel_t.T.reshape(nseq, N_KV * SAMPLE_ROWS, LANE)
    o8 = _attn_sample(a, win_t, q8, kv8, unsel, oc, gl8, bg, tmpl['e_sample'], tmpl['s_sel'], tmpl['s_new'],
                      tmpl['s_win'], ks, vs)
    o = o8[:, :dec_t].reshape(1, nseq * dec_t, NSA_WIDTH)
    w_small = p['w_spatial'][a][:, :dec_t, :dec_t]
    above = lambda k: jnp.pad(jnp.diagonal(w_small, offset=-k, axis1=1, axis2=2), ((0, 0), (k, 0))).T
    w_sp = jnp.stack([jnp.tile(above(k), (nseq, 1)) for k in range(dec_t)])
    b_sp = jnp.tile(p['b_spatial'][a][:, :dec_t].T, (nseq, 1))
    ys_new = _post_a(o, z, u, vn, zg, xs, w_sp, b_sp, w_out, short_chunks=True).reshape(nseq, dec_t, D_MODEL)
    rows_s = kv[0, :, :4 * KV_WIDTH].reshape(nseq, dec_t, 4, N_KV, HEAD_DIM)
    win_s = kv[0, :, 4 * KV_WIDTH:].reshape(nseq, dec_t, 2, N_KV, HEAD_DIM)
    return yp_new, ys_new, nsa_t, rows_s, win_p, win_s, vn.reshape(nseq, dec_t, GMLP_WIDTH)


def _odd_layer(c, yp, ys, h0, conv0, p, final):
    nb, nt, _ = yp.shape
    nseq, dec_t, _ = ys.shape
    norm_g = p['norm_c'][c][None]
    w_in = p['w_in_c'][c].astype(BF16)
    consts = (p['conv_w'][c], p['conv_b'][c][None], p['w_rg_a'][c].astype(BF16), p['b_rg_a'][c][None],
              p['w_rg_x'][c].astype(BF16), p['b_rg_x'][c][None], p['lru_lambda'][c][None],
              p['w_out_c'][c].astype(BF16), p['final_norm'][None])
    yp_new, h_p, tail_p = _mix_c_prompt(yp, norm_g, w_in, *consts, final_norm=final)
    xs = jnp.swapaxes(ys, 0, 1)
    xb, z = _proj_c(xs.reshape(dec_t * nseq, D_MODEL), norm_g, w_in, tm=dec_t * nseq)
    tm3 = lambda x: x.reshape(dec_t, nseq, -1)
    ys_new, h_s, tail_s = _mix_c_sample(tm3(xb), tm3(z), xs, h0, jnp.swapaxes(conv0, 0, 1), *consts,
                                        final_norm=final)
    return yp_new, jnp.swapaxes(ys_new, 0, 1), h_p[:, 0], h_s, tail_p, jnp.swapaxes(tail_s, 0, 1)


def kernel(x_prompt, x_sample, cache_nsa, cache_win, state_lru_h, state_lru_conv, page_table, norm_a, w_in_a, b_gate_a, pe_cmp, w_cmp, gmlp_ln_g, gmlp_ln_b, w_spatial, b_spatial, w_out_a, norm_c, w_in_c, conv_w, conv_b, w_rg_a, b_rg_a, w_rg_x, b_rg_x, lru_lambda, w_out_c, rel_bias, final_norm):
    p = dict(norm_a=norm_a, w_in_a=w_in_a, b_gate_a=b_gate_a, pe_cmp=pe_cmp, w_cmp=w_cmp, gmlp_ln_g=gmlp_ln_g,
             gmlp_ln_b=gmlp_ln_b, w_spatial=w_spatial, b_spatial=b_spatial, w_out_a=w_out_a, norm_c=norm_c,
             w_in_c=w_in_c, conv_w=conv_w, conv_b=conv_b, w_rg_a=w_rg_a, b_rg_a=b_rg_a, w_rg_x=w_rg_x,
             b_rg_x=b_rg_x, lru_lambda=lru_lambda, w_out_c=w_out_c, rel_bias=rel_bias, final_norm=final_norm)
    nb, nt, _ = x_prompt.shape
    nseq, n_pages = page_table.shape
    past_len = n_pages * PAGE
    wlen = cache_win.shape[2]
    depth = norm_a.shape[0] + norm_c.shape[0]
    assert nt % KV_TILE == 0 and nt >= WIN_KEYS and nt // CMP_BLOCK <= LANE
    assert past_len // CMP_BLOCK < LANE and x_sample.shape[1] <= SAMPLE_ROWS and n_pages <= N_PAGES_MAX

    cache_t = jnp.transpose(cache_nsa, (0, 1, 3, 4, 5, 2))
    win_t = jnp.transpose(cache_win, (0, 1, 3, 4, 5, 2))
    pt_flat = page_table.reshape(-1).astype(jnp.int32)

    blk_of_key = lambda n: (np.arange(n) // CMP_BLOCK)[None, :] == np.arange(LANE)[:, None]
    rows_s = SAMPLE_ROWS
    stack = lambda t: t.reshape(N_HEADS * rows_s, t.shape[-1])
    by_group = lambda t: t.reshape(N_KV, GROUP_ROWS, t.shape[-1])
    assert NEAR_SPAN + LANE - (KV_TILE - 1) >= _UPPER[-1] and nt % KV_TILE == 0
    tmpl = dict(
        far_lanes=_far_bias_lanes(rel_bias),
        e_prompt=jnp.asarray(np.where(blk_of_key(nt), NEG, 0.0), BF16),
        e_sample=jnp.asarray(np.where(blk_of_key(past_len), NEG, 0.0), BF16),
        cmp=_bias_template(rel_bias, Q_BLOCK, LANE, -CMP_BLOCK, CMP_BLOCK * (LANE - 2) - (CMP_BLOCK - 1)),
        near=by_group(_bias_template(rel_bias, Q_BLOCK, NEAR_SPAN + KV_TILE, -1, NEAR_SPAN)),
        win=by_group(_bias_template(rel_bias, Q_BLOCK, WINDOW + WIN_KEYS, -1, WINDOW, hi=WINDOW)),
        s_cmp=stack(_bias_template(rel_bias, rows_s, LANE, -CMP_BLOCK, past_len - (CMP_BLOCK - 1))),
        s_sel=stack(_bias_template(rel_bias, rows_s, past_len, -1, past_len)),
        s_new=stack(_bias_template(rel_bias, rows_s, LANE, -1, 0, cmax=x_sample.shape[1])),
        s_win=stack(_bias_template(rel_bias, rows_s, wlen, -1, wlen, hi=WINDOW)),
    )

    yp, ys = x_prompt, x_sample
    outs = [[] for _ in range(8)]
    nsa_all = None
    for layer in range(depth):
        if layer % 2 == 0:
            yp, ys, nsa_all, *leaves = _even_layer(layer // 2, yp, ys, nsa_all, cache_t, win_t, pt_flat, n_pages,
                                                   tmpl, p)
            for dst, leaf in zip(outs[:4], leaves):
                dst.append(leaf)
        else:
            c = layer // 2
            yp, ys, *leaves = _odd_layer(c, yp, ys, state_lru_h[c], state_lru_conv[c], p, final=layer == depth - 1)
            for dst, leaf in zip(outs[4:], leaves):
                dst.append(leaf)
    nsa_p = jnp.transpose(nsa_all.reshape(nsa_all.shape[0], nb, 4, N_KV, HEAD_DIM, nt), (0, 1, 5, 2, 3, 4))
    return (yp, ys, nsa_p) + tuple(jnp.stack(o) for o in outs)
```

```python
import functools
import math

import numpy as np
import jax
import jax.numpy as jnp
from jax import lax
from jax.experimental import pallas as pl
from jax.experimental.pallas import tpu as pltpu

F32 = jnp.float32
BF16 = jnp.bfloat16

D_MODEL = 1024
N_HEADS = 8
N_KV = 2
HPG = N_HEADS // N_KV
HEAD_DIM = 64
NSA_WIDTH = N_HEADS * HEAD_DIM
KV_WIDTH = N_KV * HEAD_DIM
CMP_BLOCK = 64
N_SEL = 16
WINDOW = 512
GMLP_GROUPS = 4
GMLP_WIDTH = 512
GMLP_CHUNK = 128
LRU_WIDTH = 1024
LRU_HEADS = 8
LRU_BLOCK = 128
CONV_WIDTH = 4
LRU_C = 8.0
REL_BUCKETS = 32
REL_MAX_DIST = 1024
EPS = 1e-6
NEG = -1e30
PAGE = 128

LANE = 128
SUBLANE = 8
PROJ_ROWS = 512
MIX_ROWS = 512
Q_BLOCK = 128
KV_TILE = 512
NEAR_SPAN = 1280
WIN_KEYS = WINDOW + Q_BLOCK
VMEM_LIMIT = 56 * 1024 * 1024

QPAD = N_HEADS * LANE
C_KV = QPAD
C_Z = C_KV + 6 * KV_WIDTH
C_U = C_Z + NSA_WIDTH
C_V = C_U + GMLP_WIDTH
C_ZG = C_V + GMLP_WIDTH
C_G = C_ZG + GMLP_WIDTH
PROJ_COLS = C_G + LANE


def _nt(a, b):
    return lax.dot_general(a, b, (((1,), (1,)), ((), ())), preferred_element_type=F32)


def _mm(a, b):
    return jnp.dot(a, b, preferred_element_type=F32)


def _params(sem):
    return pltpu.CompilerParams(dimension_semantics=sem, vmem_limit_bytes=VMEM_LIMIT)


def _const_spec(shape):
    n = len(shape)
    return pl.BlockSpec(shape, lambda *_: (0,) * n)


def _bucket_upper_bounds():
    max_exact = REL_BUCKETS // 2
    d = np.arange(0, 4 * REL_MAX_DIST, dtype=np.int64)

    def buckets(ft):
        df = np.maximum(d, 1).astype(ft)
        large = max_exact + (np.log(df / ft(max_exact)) / ft(math.log(REL_MAX_DIST / max_exact))
                             * ft(REL_BUCKETS - max_exact)).astype(np.int32)
        return np.where(d < max_exact, d, np.minimum(large, REL_BUCKETS - 1))

    b32, b64 = buckets(np.float32), buckets(np.float64)
    assert (b32 == b64).all() and (np.diff(b32) >= 0).all() and b32[-1] == REL_BUCKETS - 1
    return [int(np.argmax(b32 > k)) for k in range(REL_BUCKETS - 1)]


_UPPER = _bucket_upper_bounds()


def _bias_tmpl_body(tab_ref, o_ref, *, rows, cs, off, hi, cmax):
    h = pl.program_id(0)
    t = lax.broadcasted_iota(jnp.int32, (rows, LANE), 0)
    lane = lax.broadcasted_iota(jnp.int32, (rows, LANE), 1)

    def chunk(k, carry):
        c0 = pl.multiple_of(k * LANE, LANE)
        c = lane + c0
        d = t + cs * c + off
        dd = jnp.maximum(d, 0)
        val = jnp.full((rows, LANE), tab_ref[REL_BUCKETS - 1, h], F32)
        for b in range(REL_BUCKETS - 2, -1, -1):
            val = jnp.where(dd < _UPPER[b], tab_ref[b, h], val)
        ok = (d >= 0) & (d <= hi) & (c < cmax)
        o_ref[0, :, pl.ds(c0, LANE)] = jnp.where(ok, val, NEG)
        return carry

    lax.fori_loop(0, o_ref.shape[2] // LANE, chunk, 0)


def _bias_template(rel_bias, rows, width, cs, off, hi=1 << 30, cmax=1 << 30):
    return pl.pallas_call(
        functools.partial(_bias_tmpl_body, rows=rows, cs=cs, off=off, hi=hi, cmax=cmax),
        grid=(N_HEADS,),
        in_specs=[pl.BlockSpec(memory_space=pltpu.SMEM)],
        out_specs=pl.BlockSpec((1, rows, width), lambda h: (h, 0, 0)),
        out_shape=jax.ShapeDtypeStruct((N_HEADS, rows, width), F32),
        compiler_params=_params(("arbitrary",)),
        name="bias_template",
    )(rel_bias)


def _rms(x, g):
    return x * lax.rsqrt(jnp.mean(x * x, axis=-1, keepdims=True) + EPS) * g


def _proj_a_body(x_ref, g_ref, w_ref, lng_ref, lnb_ref, *rest, feature_major, n_prev):
    if n_prev:
        prev_ref, rest = rest[0], rest[1:]
    q_ref, rest = rest[0], rest[1:]
    xn = _rms(x_ref[0], g_ref[...])
    y = _mm(xn.astype(BF16), w_ref[...])
    q_ref[0] = (y[:, :QPAD] * (HEAD_DIM ** -0.5)).astype(q_ref.dtype)
    kv = y[:, C_KV:C_Z]
    if feature_major:
        nsa_ref, win_ref, kvtb_ref, gl_ref, z_ref, u_ref, vn_ref, zg_ref = rest
        kvt = kv.T
        if n_prev:
            nsa_ref[0:n_prev, 0] = prev_ref[:, 0]
        nsa_ref[n_prev, 0] = kvt[:4 * KV_WIDTH]
        win_ref[0] = kvt[4 * KV_WIDTH:]
        kvtb_ref[0] = kvt.astype(BF16)
    else:
        kv_ref, gl_ref, z_ref, u_ref, vn_ref, zg_ref = rest
        kv_ref[0] = kv
    z_ref[0] = y[:, C_Z:C_U]
    u_ref[0] = y[:, C_U:C_V]
    v = y[:, C_V:C_ZG]
    mu = jnp.mean(v, axis=-1, keepdims=True)
    var = jnp.mean(jnp.square(v - mu), axis=-1, keepdims=True)
    vn_ref[0] = (v - mu) * lax.rsqrt(var + EPS) * lng_ref[...] + lnb_ref[...]
    zg_ref[0] = y[:, C_ZG:C_G]
    gl_ref[0] = y[:, C_G:]


def _proj_a(x, norm_g, w, ln_g, ln_b, tm, feature_major, prev_nsa=None):
    nb, nt, _ = x.shape
    n_prev = 0 if prev_nsa is None else prev_nsa.shape[0]
    row = lambda width: pl.BlockSpec((1, tm, width), lambda b, i: (b, i, 0))
    colT = pl.BlockSpec((1, 6 * KV_WIDTH, tm), lambda b, i: (b, 0, i))
    sds = lambda shape, dt=F32: jax.ShapeDtypeStruct(shape, dt)
    once = lambda shape: pl.BlockSpec(shape, lambda b, i: (0,) * len(shape), pipeline_mode=pl.Buffered(1))
    layers = lambda n: pl.BlockSpec((n, 1, 4 * KV_WIDTH, tm), lambda b, i: (0, b, 0, i))
    if feature_major:
        col = lambda r: pl.BlockSpec((1, r, tm), lambda b, i: (b, 0, i))
        kv_specs = [layers(n_prev + 1), col(2 * KV_WIDTH), colT]
        kv_shapes = [sds((n_prev + 1, nb, 4 * KV_WIDTH, nt)), sds((nb, 2 * KV_WIDTH, nt)),
                     sds((nb, 6 * KV_WIDTH, nt), BF16)]
    else:
        kv_specs, kv_shapes = [row(6 * KV_WIDTH)], [sds((nb, nt, 6 * KV_WIDTH))]
    prev_args, prev_specs = ([prev_nsa], [layers(n_prev)]) if n_prev else ([], [])
    return pl.pallas_call(
        functools.partial(_proj_a_body, feature_major=feature_major, n_prev=n_prev),
        grid=(nb, nt // tm),
        in_specs=[row(D_MODEL), once((1, D_MODEL)), once((D_MODEL, PROJ_COLS)),
                  once((1, GMLP_WIDTH)), once((1, GMLP_WIDTH))] + prev_specs,
        out_specs=[row(QPAD)] + kv_specs + [row(LANE), row(NSA_WIDTH), row(GMLP_WIDTH), row(GMLP_WIDTH),
                                            row(GMLP_WIDTH)],
        out_shape=[sds((nb, nt, QPAD), BF16 if feature_major else F32)] + kv_shapes
                  + [sds((nb, nt, LANE)), sds((nb, nt, NSA_WIDTH)), sds((nb, nt, GMLP_WIDTH)),
                     sds((nb, nt, GMLP_WIDTH)), sds((nb, nt, GMLP_WIDTH))],
        compiler_params=_params(("arbitrary", "arbitrary")),
        name="proj_a",
    )(x, norm_g, w, ln_g, ln_b, *prev_args)


def _split_bf16(x):
    hi = x.astype(BF16)
    return hi, (x - hi.astype(F32)).astype(BF16)


def _cmp_prompt_body(kvt_ref, pe_ref, w_ref, o_ref):
    nt = kvt_ref.shape[2]
    blk = lax.broadcasted_iota(jnp.int32, (LANE, nt), 0)
    pos = lax.broadcasted_iota(jnp.int32, (LANE, nt), 1)
    pool = jnp.where(pos // CMP_BLOCK == blk, 1.0 / CMP_BLOCK, 0.0).astype(BF16)
    hi, lo = _split_bf16(kvt_ref[0])
    mean = _nt(pool, hi) + _nt(pool, lo)
    mean = mean + jnp.mean(pe_ref[...], axis=0, keepdims=True)
    o_ref[0] = _mm(mean.astype(BF16), w_ref[...])


def _cmp_prompt(kvt, pe_rows, w_bd):
    n_layers, nb, _, nt = kvt.shape
    return pl.pallas_call(
        _cmp_prompt_body,
        grid=(nb,),
        in_specs=[pl.BlockSpec((None, 1, 2 * KV_WIDTH, nt), lambda b: (n_layers - 1, b, 0, 0)),
                  _const_spec((CMP_BLOCK, 2 * KV_WIDTH)), _const_spec((2 * KV_WIDTH, 2 * KV_WIDTH))],
        out_specs=pl.BlockSpec((1, LANE, 2 * KV_WIDTH), lambda b: (b, 0, 0)),
        out_shape=jax.ShapeDtypeStruct((nb, LANE, 2 * KV_WIDTH), F32),
        compiler_params=_params(("arbitrary",)),
        name="cmp_prompt",
    )(kvt, pe_rows, w_bd)


def _select_blocks(score_t, cur):
    shape = score_t.shape
    blk = lax.broadcasted_iota(jnp.int32, shape, 0)
    forced = (blk == 0) | (blk == cur) | (blk == cur - 1)
    score = jnp.where(forced | (blk >= cur), -jnp.inf, score_t)

    def pick(_, carry):
        score, unsel = carry
        mx = jnp.max(score, axis=0, keepdims=True)
        idx = jnp.min(jnp.where(score == mx, blk, shape[0]), axis=0, keepdims=True)
        hit = blk == idx
        unsel = jnp.where(hit & (mx > -jnp.inf), 0.0, unsel)
        return jnp.where(hit, -jnp.inf, score), unsel

    _, unsel = lax.fori_loop(0, N_SEL - 3, pick, (score, jnp.where(forced, 0.0, 1.0)))
    return unsel


def _softmax_rows(s):
    m = jnp.max(s, axis=-1, keepdims=True)
    e = jnp.exp(s - m)
    return e / jnp.sum(e, axis=-1, keepdims=True)


GROUP_ROWS = HPG * Q_BLOCK


def _rowmax(s):
    m = s[:, :LANE]
    for k in range(1, s.shape[1] // LANE):
        m = jnp.maximum(m, s[:, k * LANE:(k + 1) * LANE])
    return jnp.broadcast_to(jnp.max(m, axis=-1, keepdims=True), (s.shape[0], LANE))


def _tile_lanes(x, n):
    return jnp.concatenate([x] * n, axis=1)


def _attn_prompt_body(q_ref, bl_ref, cmp_ref, ka_ref, va_ref, kw_ref, vw_ref, gl_ref, bg_ref,
                      tc_ref, tn_ref, tw_ref, z_ref, u_ref, vn_ref, zg_ref, x_ref, wsp_ref, bsp_ref, wo_ref,
                      y_ref, lhs_scr, m_scr, acc_scr, part_scr, gsel_scr, ypart_scr):
    i = pl.program_id(1)
    qf = q_ref[0]

    y_g = u_ref[0] * _gmlp_mix(vn_ref[0].astype(BF16), wsp_ref, bsp_ref) * _silu(zg_ref[0])
    ypart_scr[...] = x_ref[0] + _mm(y_g.astype(BF16), wo_ref[NSA_WIDTH:])
    lane = lax.broadcasted_iota(jnp.int32, (Q_BLOCK, LANE), 1)
    stack = lambda x, g: jnp.concatenate([x[:, h * LANE:(h + 1) * LANE] for h in range(g * HPG, (g + 1) * HPG)], axis=0)
    qs = [stack(qf, g) for g in range(N_KV)]
    qs_far = [stack(qf + bl_ref[...], g) for g in range(N_KV)]

    kc = cmp_ref[0, :, :KV_WIDTH].astype(BF16)
    vc = cmp_ref[0, :, KV_WIDTH:].astype(BF16)
    seen = lane <= 2 * i + 1
    shift = (2 * i + 2) % LANE
    o_cmp, p_sum = [], []
    for g in range(N_KV):
        tmpl = jnp.concatenate([jnp.where(seen, pltpu.roll(tc_ref[h], shift, axis=1), NEG)
                                for h in range(g * HPG, (g + 1) * HPG)], axis=0)
        ok = tmpl > 0.5 * NEG
        p = jnp.where(ok, _softmax_rows(_nt(qs[g], kc) + tmpl), 0.0)
        o_cmp.append(_mm(p.astype(BF16), vc))
        p_sum.append(p.reshape(HPG, Q_BLOCK, LANE).sum(axis=0))

    w0 = pl.multiple_of(jnp.maximum(i * Q_BLOCK - WINDOW, 0), LANE)
    woff = pl.multiple_of(WINDOW - (i * Q_BLOCK - w0), LANE)
    kwt = kw_ref[0, :, pl.ds(w0, WIN_KEYS)]
    gate = jax.nn.sigmoid(gl_ref[0] + bg_ref[...])
    for g in range(N_KV):
        col = lambda k: jnp.concatenate([gate[:, k * N_HEADS + h:k * N_HEADS + h + 1]
                                         for h in range(g * HPG, (g + 1) * HPG)], axis=0)
        s = _mm(qs[g], kwt) + tw_ref[g, :, pl.ds(woff, WIN_KEYS)]
        e = jnp.exp(s - _tile_lanes(_rowmax(s), WIN_KEYS // LANE)).astype(BF16)
        pv = _nt(e, vw_ref[0, g, :, pl.ds(w0, WIN_KEYS)])
        part_scr[g] = col(0) * o_cmp[g] + col(2) * (pv / pltpu.roll(pv, HEAD_DIM, axis=1))
        gsel_scr[g] = jnp.broadcast_to(col(1), (GROUP_ROWS, LANE))

    tpos = i * Q_BLOCK + lax.broadcasted_iota(jnp.int32, (1, N_KV * Q_BLOCK), 1) % Q_BLOCK
    score_t = jnp.concatenate([ps.T for ps in p_sum], axis=1)
    cur = tpos // CMP_BLOCK
    def first_blocks(n):
        return lambda: jnp.concatenate([_select_blocks(score_t[:n], cur), jnp.ones((LANE - n, N_KV * Q_BLOCK), F32)],
                                       axis=0)

    reach = (i + 1) * Q_BLOCK
    unsel_t = lax.cond(reach <= (LANE // 4) * CMP_BLOCK, first_blocks(LANE // 4),
                       lambda: lax.cond(reach <= (LANE // 2) * CMP_BLOCK, first_blocks(LANE // 2),
                                        lambda: _select_blocks(score_t, cur)))
    for g in range(N_KV):
        un = _tile_rows(unsel_t[:, g * Q_BLOCK:(g + 1) * Q_BLOCK].T.astype(BF16), HPG)
        lhs_scr[0, g] = jnp.concatenate([qs_far[g], un], axis=1)
        lhs_scr[1, g] = jnp.concatenate([qs[g], un], axis=1)

    m_scr[...] = jnp.full(m_scr.shape, NEG, F32)
    acc_scr[...] = jnp.zeros(acc_scr.shape, F32)

    def tile(j, near):
        k0 = pl.multiple_of(j * KV_TILE, KV_TILE)
        s = [_mm(lhs_scr[1 if near else 0, g], ka_ref[0, g, :, pl.ds(k0, KV_TILE)]) for g in range(N_KV)]
        if near:
            off = pl.multiple_of(NEAR_SPAN - (i * Q_BLOCK - j * KV_TILE), LANE)
            s = [s[g] + tn_ref[g, :, pl.ds(off, KV_TILE)] for g in range(N_KV)]
        m_old = [m_scr[g] for g in range(N_KV)]
        m_new = [jnp.maximum(m_old[g], _rowmax(s[g])) for g in range(N_KV)]
        p = [jnp.exp(s[g] - _tile_lanes(m_new[g], KV_TILE // LANE)).astype(BF16) for g in range(N_KV)]
        for g in range(N_KV):
            acc_scr[g] = jnp.exp(m_old[g] - m_new[g]) * acc_scr[g] + _nt(p[g], va_ref[0, g, :, pl.ds(k0, KV_TILE)])
            m_scr[g] = m_new[g]

    n_far = jnp.maximum(i * Q_BLOCK - (NEAR_SPAN + LANE) + KV_TILE, 0) // KV_TILE
    n_tiles = i // (KV_TILE // Q_BLOCK) + 1
    def far_quad(jj, c):
        for k in range(4):
            tile(4 * jj + k, False)
        return c

    lax.fori_loop(0, n_far // 4, far_quad, 0)
    done = n_far // 4 * 4

    @pl.when(n_far - done >= 2)
    def _():
        tile(done, False)
        tile(done + 1, False)

    @pl.when(n_far % 2 == 1)
    def _():
        tile(n_far - 1, False)

    def near_pair(jj, c):
        tile(n_far + 2 * jj, True)
        tile(n_far + 2 * jj + 1, True)
        return c

    n_near = n_tiles - n_far
    lax.fori_loop(0, n_near // 2, near_pair, 0)

    @pl.when(n_near % 2 == 1)
    def _():
        tile(n_tiles - 1, True)

    merged = []
    for g in range(N_KV):
        acc = acc_scr[g]
        merged.append(part_scr[g] + gsel_scr[g] * (acc / pltpu.roll(acc, HEAD_DIM, axis=1)))
    y_nsa = []
    for c in range(HPG):
        rows = slice(c * Q_BLOCK, (c + 1) * Q_BLOCK)
        o = jnp.where(lane < HEAD_DIM, merged[0][rows], merged[1][rows])
        y_nsa.append((o * _silu(z_ref[0, :, c * LANE:(c + 1) * LANE])).astype(BF16))
    y_ref[0] = ypart_scr[...] + _mm(jnp.concatenate(y_nsa, axis=-1), wo_ref[:NSA_WIDTH])


def _tile_rows(x, n):
    return jnp.concatenate([x] * n, axis=0)


def _attn_prompt(q, bias_lanes, cmp, kaug, vaug, kwin, vwaug, gl, bg, t_cmp, t_near, t_win,
                 z, u, vn, zg, x, wsp, bsp, wo):
    nb, nt, _ = q.shape
    assert wsp.shape[1] == GMLP_CHUNK == Q_BLOCK
    once = pl.Buffered(1)
    per_b = lambda a: pl.BlockSpec((1,) + a.shape[1:], lambda b, i: (b,) + (0,) * (a.ndim - 1), pipeline_mode=once)
    whole = lambda a: pl.BlockSpec(a.shape, lambda b, i: (0,) * a.ndim, pipeline_mode=once)
    row = lambda width: pl.BlockSpec((1, Q_BLOCK, width), lambda b, i: (b, i, 0))
    return pl.pallas_call(
        _attn_prompt_body,
        grid=(nb, nt // Q_BLOCK),
        in_specs=[row(QPAD), whole(bias_lanes), per_b(cmp), per_b(kaug), per_b(vaug), per_b(kwin), per_b(vwaug),
                  row(LANE), _const_spec((1, LANE)), whole(t_cmp), whole(t_near), whole(t_win),
                  row(NSA_WIDTH), row(GMLP_WIDTH), row(GMLP_WIDTH), row(GMLP_WIDTH), row(D_MODEL),
                  whole(wsp), whole(bsp), whole(wo)],
        out_specs=row(D_MODEL),
        out_shape=jax.ShapeDtypeStruct((nb, nt, D_MODEL), F32),
        scratch_shapes=[pltpu.VMEM((2, N_KV, GROUP_ROWS, 2 * LANE), BF16)]
                       + [pltpu.VMEM((N_KV, GROUP_ROWS, LANE), F32)] * 4 + [pltpu.VMEM((Q_BLOCK, D_MODEL), F32)],
        compiler_params=_params(("arbitrary", "arbitrary")),
        name="attn_prompt",
    )(q, bias_lanes, cmp, kaug, vaug, kwin, vwaug, gl, bg, t_cmp, t_near, t_win, z, u, vn, zg, x, wsp, bsp, wo)


def _prompt_kv_operands(kvtb, emask):
    nb, _, nt = kvtb.shape
    half = lambda kind, g: kvtb[:, kind * KV_WIDTH + g * HEAD_DIM:kind * KV_WIDTH + (g + 1) * HEAD_DIM]
    ones = jnp.ones((nb, HEAD_DIM, nt), BF16)
    extra = ones.at[:, 2:].set(0)
    e = jnp.broadcast_to(emask[None], (nb, LANE, nt))
    kaug = jnp.stack([jnp.concatenate([half(2, 0), extra, e], axis=1),
                      jnp.concatenate([extra, half(2, 1), e], axis=1)], axis=1)
    aug = lambda kind: jnp.stack([jnp.concatenate([half(kind, 0), ones], axis=1),
                                  jnp.concatenate([ones, half(kind, 1)], axis=1)], axis=1)
    return kaug, aug(3), kvtb[:, 4 * KV_WIDTH:5 * KV_WIDTH], aug(5)


def _far_bias_lanes(rel_bias):
    b = rel_bias[REL_BUCKETS - 1]
    hi = b.astype(BF16).astype(F32)
    pair = jnp.stack([hi, b - hi], axis=1)
    tile = jnp.zeros((N_HEADS, LANE), F32)
    first = tile.at[:, HEAD_DIM:HEAD_DIM + 2].set(pair)
    second = tile.at[:, 0:2].set(pair)
    return jnp.where((np.arange(N_HEADS) < HPG)[:, None], first, second).reshape(1, QPAD).astype(BF16)


def _silu(x):
    return x * jax.nn.sigmoid(x)


def _gmlp_mix(vn, wsp_ref, bsp_ref):
    chunk = wsp_ref.shape[1]
    r = lax.broadcasted_iota(jnp.int32, (chunk, chunk), 0)
    c = lax.broadcasted_iota(jnp.int32, (chunk, chunk), 1)
    mix = []
    for g in range(GMLP_GROUPS):
        wm = jnp.where(c <= r, wsp_ref[g], 0.0).astype(BF16)
        mix.append(jnp.concatenate(
            [_mm(wm, vn[k * chunk:(k + 1) * chunk, g * LANE:(g + 1) * LANE]) + bsp_ref[:, g:g + 1]
             for k in range(vn.shape[0] // chunk)], axis=0))
    return jnp.concatenate(mix, axis=-1)


def _post_a_body(o_ref, z_ref, u_ref, vn_ref, zg_ref, x_ref, wsp_ref, bsp_ref, wo_ref, y_ref, *, short_chunks):
    if short_chunks:
        vn = vn_ref[0]
        mix = []
        for g in range(GMLP_GROUPS):
            vg = vn[:, g * LANE:(g + 1) * LANE]
            acc = bsp_ref[:, g:g + 1] + wsp_ref[0][:, g:g + 1] * vg
            for k in range(1, wsp_ref.shape[0]):
                acc = acc + wsp_ref[k][:, g:g + 1] * pltpu.roll(vg, k, axis=0)
            mix.append(acc)
        mix = jnp.concatenate(mix, axis=-1)
    else:
        mix = _gmlp_mix(vn_ref[0].astype(BF16), wsp_ref, bsp_ref)
    y_g = u_ref[0] * mix * _silu(zg_ref[0])
    y_nsa = o_ref[0] * _silu(z_ref[0])
    cat = jnp.concatenate([y_nsa, y_g], axis=-1).astype(BF16)
    y_ref[0] = x_ref[0] + _mm(cat, wo_ref[...])


def _post_a(o, z, u, vn, zg, x, wsp, bsp, wo, short_chunks=False):
    nb, nt, _ = x.shape
    rows = wsp.shape[1]
    row = lambda width: pl.BlockSpec((1, rows, width), lambda b, i: (b, i, 0))
    return pl.pallas_call(
        functools.partial(_post_a_body, short_chunks=short_chunks),
        grid=(nb, nt // rows),
        in_specs=[row(NSA_WIDTH), row(NSA_WIDTH), row(GMLP_WIDTH), row(GMLP_WIDTH), row(GMLP_WIDTH),
                  row(D_MODEL), _const_spec(wsp.shape), _const_spec(bsp.shape), _const_spec(wo.shape)],
        out_specs=row(D_MODEL),
        out_shape=jax.ShapeDtypeStruct(x.shape, F32),
        compiler_params=_params(("arbitrary", "arbitrary")),
        name="post_a",
    )(o, z, u, vn, zg, x, wsp, bsp, wo)


SAMPLE_ROWS = 8
N_PAGES_MAX = 16


SEQS_PER_STEP = 4


def _stack_heads(q_ref, s):
    return jnp.concatenate([q_ref[s, :, h * LANE:(h + 1) * LANE] for h in range(N_HEADS)], axis=0).astype(BF16)


def _pad_rows(x, rows):
    return jnp.concatenate([x, jnp.zeros((rows - x.shape[0], x.shape[1]), x.dtype)], axis=0)


def _cmp_sample_body(pt_ref, q_ref, kv_ref, pe_ref, wk_ref, wv_ref, tc_ref, pool_ref, *rest):
    n_pages = (len(rest) - 2) // SEQS_PER_STEP
    for s in range(SEQS_PER_STEP):
        _cmp_sample_one(s, rest[s * n_pages:(s + 1) * n_pages], q_ref, kv_ref, pe_ref, wk_ref, wv_ref, tc_ref,
                        pool_ref, *rest[-2:])


def _cmp_sample_one(s, pages, q_ref, kv_ref, pe_ref, wk_ref, wv_ref, tc_ref, pool_ref, oc_ref, ps_ref):
    n_pages = len(pages)
    lane = lax.broadcasted_iota(jnp.int32, (KV_WIDTH, LANE), 1)
    eye = lax.broadcasted_iota(jnp.int32, (KV_WIDTH, LANE), 0) == lane
    qs = _stack_heads(q_ref, s)
    summ = []
    for kind in range(2):
        sums = jnp.zeros((KV_WIDTH, LANE), F32)
        for p in range(0, n_pages, 2):
            x = jnp.concatenate([pages[p][kind].reshape(KV_WIDTH, PAGE), pages[p + 1][kind].reshape(KV_WIDTH, PAGE)],
                                axis=1)
            hi, lo = _split_bf16(x)
            pool = pool_ref[p * PAGE:(p + 2) * PAGE]
            sums = sums + _mm(hi, pool) + _mm(lo, pool)
        new = jnp.sum(kv_ref[s, :, kind * KV_WIDTH:(kind + 1) * KV_WIDTH], axis=0, keepdims=True)
        new_col = jnp.sum(jnp.where(eye, new, 0.0), axis=1, keepdims=True)
        sums = jnp.where(lane == 2 * n_pages, new_col, sums)
        mean_t = sums * (1.0 / CMP_BLOCK) + jnp.mean(pe_ref[kind], axis=1, keepdims=True)
        w_ref = wk_ref if kind == 0 else wv_ref
        summ.append(_mm(w_ref[...], mean_t.astype(BF16)).astype(BF16))
    tmpl = tc_ref[...]
    ok = tmpl > 0.5 * NEG
    p = jnp.where(ok, _softmax_rows(_mm(qs, summ[0]) + tmpl), 0.0)
    oc_ref[s] = _nt(p.astype(BF16), summ[1])
    ps_ref[s] = p.reshape(N_KV, HPG, SAMPLE_ROWS, LANE).sum(axis=1).reshape(N_KV * SAMPLE_ROWS, LANE)


def _page_specs(layer, n_pages, kind_block):
    def spec(s, p):
        return pl.BlockSpec((None, None, 2, N_KV, HEAD_DIM, PAGE),
                            lambda b, pt: (layer, pt[(b * SEQS_PER_STEP + s) * n_pages + p], kind_block, 0, 0, 0))
    return [spec(s, p) for s in range(SEQS_PER_STEP) for p in range(n_pages)]


def _cmp_sample(layer, pt_flat, n_pages, cache_t, q, kv, pe_t, wk_t, wv_t, t_cmp):
    nseq = q.shape[0]
    rows = N_HEADS * SAMPLE_ROWS
    seq = lambda r, w: pl.BlockSpec((SEQS_PER_STEP, r, w), lambda b, pt: (b, 0, 0))
    cst = lambda a: pl.BlockSpec(a.shape, lambda b, pt: (0,) * a.ndim)
    assert n_pages % 2 == 0 and nseq % SEQS_PER_STEP == 0
    pool = jnp.asarray((np.arange(n_pages * PAGE) // CMP_BLOCK)[:, None] == np.arange(LANE)[None, :], BF16)
    return pl.pallas_call(
        _cmp_sample_body,
        grid_spec=pltpu.PrefetchScalarGridSpec(
            num_scalar_prefetch=1, grid=(nseq // SEQS_PER_STEP,),
            in_specs=[seq(SAMPLE_ROWS, QPAD), seq(SAMPLE_ROWS, 6 * KV_WIDTH), cst(pe_t), cst(wk_t), cst(wv_t),
                      cst(t_cmp), cst(pool)] + _page_specs(layer, n_pages, 0),
            out_specs=[seq(rows, KV_WIDTH), seq(N_KV * SAMPLE_ROWS, LANE)]),
        out_shape=[jax.ShapeDtypeStruct((nseq, rows, KV_WIDTH), F32),
                   jax.ShapeDtypeStruct((nseq, N_KV * SAMPLE_ROWS, LANE), F32)],
        compiler_params=_params(("arbitrary",)),
        name="cmp_sample",
    )(pt_flat, q, kv, pe_t, wk_t, wv_t, t_cmp, pool, *([cache_t] * (SEQS_PER_STEP * n_pages)))


def _select_sample_body(ps_ref, o_ref, *, past_len):
    cols = ps_ref.shape[1]
    t = lax.broadcasted_iota(jnp.int32, (1, cols), 1) % SAMPLE_ROWS
    o_ref[...] = _select_blocks(ps_ref[...], (past_len + t) // CMP_BLOCK)


def _select_sample(ps_t, past_len):
    cols = ps_t.shape[1]
    tile = min(cols, 2 * LANE)
    spec = pl.BlockSpec((LANE, tile), lambda i: (0, i))
    return pl.pallas_call(
        functools.partial(_select_sample_body, past_len=past_len),
        grid=(cols // tile,), in_specs=[spec], out_specs=spec,
        out_shape=jax.ShapeDtypeStruct(ps_t.shape, F32),
        compiler_params=_params(("arbitrary",)),
        name="select_sample",
    )(ps_t)


def _attn_sample_body(pt_ref, q_ref, kv_ref, un_ref, oc_ref, gl_ref, bg_ref, e_ref, ts_ref, tn_ref, tw_ref,
                      win_ref, *rest):
    n_pages = (len(rest) - 1) // SEQS_PER_STEP
    for s in range(SEQS_PER_STEP):
        _attn_sample_one(s, rest[s * n_pages:(s + 1) * n_pages], q_ref, kv_ref, un_ref, oc_ref, gl_ref, bg_ref, e_ref,
                         ts_ref, tn_ref, tw_ref, win_ref, rest[-1])


def _attn_sample_one(s, pages, q_ref, kv_ref, un_ref, oc_ref, gl_ref, bg_ref, e_ref, ts_ref, tn_ref, tw_ref,
                     win_ref, o_ref):
    qs = _stack_heads(q_ref, s)
    lane = lax.broadcasted_iota(jnp.int32, (SAMPLE_ROWS, LANE), 1)
    tn = tn_ref[...]

    def new_rows(col):
        return _pad_rows(kv_ref[s, :, col * KV_WIDTH:(col + 1) * KV_WIDTH], LANE).astype(BF16)

    def branch(k_past, v_past, bias, k_col, v_col):
        s = _mm(qs, k_past) + bias
        s_new = _nt(qs, new_rows(k_col)) + tn
        m = jnp.maximum(jnp.max(s, axis=-1, keepdims=True), jnp.max(s_new, axis=-1, keepdims=True))
        p, p_new = jnp.exp(s - m), jnp.exp(s_new - m)
        den = jnp.sum(p, axis=-1, keepdims=True) + jnp.sum(p_new, axis=-1, keepdims=True)
        return (_nt(p.astype(BF16), v_past) + _mm(p_new.astype(BF16), new_rows(v_col))) / den

    cat = lambda kind: jnp.concatenate([pg[kind].reshape(KV_WIDTH, PAGE) for pg in pages], axis=1).astype(BF16)
    madd = _mm(un_ref[s].astype(BF16), e_ref[...])
    madd = jnp.concatenate([madd[g * SAMPLE_ROWS:(g + 1) * SAMPLE_ROWS] for g in range(N_KV) for _ in range(HPG)],
                           axis=0)
    o_sel = branch(cat(0), cat(1), ts_ref[...] + madd, 2, 3)
    win = lambda kind: win_ref[s, kind].reshape(KV_WIDTH, win_ref.shape[-1]).astype(BF16)
    o_win = branch(win(0), win(1), tw_ref[...], 4, 5)

    gate = jax.nn.sigmoid(gl_ref[s] + bg_ref[...])
    oc = oc_ref[s]

    def merged(h):
        rows = slice(h * SAMPLE_ROWS, (h + 1) * SAMPLE_ROWS)
        col = lambda k: gate[:, k * N_HEADS + h:k * N_HEADS + h + 1]
        return col(0) * oc[rows] + col(1) * o_sel[rows] + col(2) * o_win[rows]

    for c in range(HPG):
        o_ref[s, :, c * LANE:(c + 1) * LANE] = jnp.where(lane < HEAD_DIM, merged(c), merged(HPG + c))


def _attn_sample(layer, pt_flat, n_pages, cache_t, win_t, q, kv, unsel, oc, gl, bg, emask, t_sel, t_new, t_win):
    nseq = q.shape[0]
    seq = lambda a: pl.BlockSpec((SEQS_PER_STEP,) + a.shape[1:], lambda b, pt: (b,) + (0,) * (a.ndim - 1))
    cst = lambda a: pl.BlockSpec(a.shape, lambda b, pt: (0,) * a.ndim)
    win_spec = pl.BlockSpec((None, SEQS_PER_STEP, 2, N_KV, HEAD_DIM, win_t.shape[-1]),
                            lambda b, pt: (layer, b, 0, 0, 0, 0))
    return pl.pallas_call(
        _attn_sample_body,
        grid_spec=pltpu.PrefetchScalarGridSpec(
            num_scalar_prefetch=1, grid=(nseq // SEQS_PER_STEP,),
            in_specs=[seq(q), seq(kv), seq(unsel), seq(oc), seq(gl), cst(bg), cst(emask), cst(t_sel), cst(t_new),
                      cst(t_win), win_spec] + _page_specs(layer, n_pages, 1),
            out_specs=pl.BlockSpec((SEQS_PER_STEP, SAMPLE_ROWS, NSA_WIDTH), lambda b, pt: (b, 0, 0))),
        out_shape=jax.ShapeDtypeStruct((nseq, SAMPLE_ROWS, NSA_WIDTH), F32),
        compiler_params=_params(("arbitrary",)),
        name="attn_sample",
    )(pt_flat, q, kv, unsel, oc, gl, bg, emask, t_sel, t_new, t_win, win_t,
      *([cache_t] * (SEQS_PER_STEP * n_pages)))


def _proj_c_body(x_ref, g_ref, w_ref, xb_ref, z_ref):
    y = _mm(_rms(x_ref[...], g_ref[...]).astype(BF16), w_ref[...])
    xb_ref[...] = y[:, :LRU_WIDTH]
    z_ref[...] = y[:, LRU_WIDTH:]


def _proj_c(x, norm_g, w, tm):
    n = x.shape[0]
    row = pl.BlockSpec((tm, LRU_WIDTH), lambda i: (i, 0))
    return pl.pallas_call(
        _proj_c_body,
        grid=(n // tm,),
        in_specs=[row, _const_spec((1, D_MODEL)), _const_spec((D_MODEL, 2 * LRU_WIDTH))],
        out_specs=[row, row],
        out_shape=[jax.ShapeDtypeStruct((n, LRU_WIDTH), F32)] * 2,
        compiler_params=_params(("arbitrary",)),
        name="proj_c",
    )(x, norm_g, w)


def _lru_coeffs(xc, wra_ref, bra_ref, wrx_ref, brx_ref, lam_ref):
    xb = xc.astype(BF16)
    blocks = lambda w_ref: jnp.concatenate(
        [_mm(xb[:, k * LRU_BLOCK:(k + 1) * LRU_BLOCK], w_ref[k]) for k in range(LRU_HEADS)], axis=-1)
    r = jax.nn.sigmoid(blocks(wra_ref) + bra_ref[...])
    ig = jax.nn.sigmoid(blocks(wrx_ref) + brx_ref[...])
    log_a = -LRU_C * r * jax.nn.softplus(-lam_ref[...])
    a = jnp.exp(log_a)
    return a, jnp.sqrt(jnp.tanh(-log_a) * (1.0 + a * a)) * ig * xc


def _mix_c_prompt_body(x_ref, ng_ref, wi_ref, cw_ref, cb_ref, wra_ref, bra_ref, wrx_ref, brx_ref, lam_ref, wo_ref,
                       fg_ref, y_ref, hl_ref, tail_ref, xin_scr, a_scr, b_scr, z_scr, h_scr, *, final_norm):
    tt = x_ref.shape[1]
    pad = SUBLANE
    hist = CONV_WIDTH - 1

    @pl.when(pl.program_id(1) == 0)
    def _():
        xin_scr[0:pad] = jnp.zeros((pad, LRU_WIDTH), F32)
        h_scr[...] = jnp.zeros(h_scr.shape, F32)

    proj = _mm(_rms(x_ref[0], ng_ref[...]).astype(BF16), wi_ref[...])
    xb = proj[:, :LRU_WIDTH]
    xin_scr[pad:pad + tt] = xb
    z_scr[...] = proj[:, LRU_WIDTH:]
    first = lax.broadcasted_iota(jnp.int32, (pad, LRU_WIDTH), 0)
    xc = cb_ref[...] + xb * cw_ref[hist:hist + 1]
    for k in range(hist):
        back = hist - k
        rolled = pltpu.roll(xb, back, axis=0)
        head = jnp.where(first < back, pltpu.roll(xin_scr[0:pad], back, axis=0), rolled[:pad])
        xc = xc + jnp.concatenate([head, rolled[pad:]], axis=0) * cw_ref[k:k + 1]
    a, b = _lru_coeffs(xc, wra_ref, bra_ref, wrx_ref, brx_ref, lam_ref)
    a_scr[...] = a
    b_scr[...] = b

    row = lax.broadcasted_iota(jnp.int32, (8, LRU_WIDTH), 0)

    def step(k, h_prev):
        r0 = pl.multiple_of(k * 8, 8)
        ca, cbv = a_scr[pl.ds(r0, 8)], b_scr[pl.ds(r0, 8)]
        for s in (1, 2, 4):
            keep = row >= s
            cbv = cbv + ca * jnp.where(keep, pltpu.roll(cbv, s, axis=0), 0.0)
            ca = ca * jnp.where(keep, pltpu.roll(ca, s, axis=0), 1.0)
        hs = ca * h_prev + cbv
        b_scr[pl.ds(r0, 8)] = hs
        return jnp.broadcast_to(hs[7:8], (8, LRU_WIDTH))

    h_scr[...] = lax.fori_loop(0, tt // 8, step, h_scr[...])
    y = (b_scr[...] * _silu(z_scr[...])).astype(BF16)
    out = x_ref[0] + _mm(y, wo_ref[...])
    y_ref[0] = _rms(out, fg_ref[...]) if final_norm else out
    hl_ref[0] = h_scr[0:1]
    tail_ref[0] = xin_scr[pad + tt - hist:pad + tt]
    xin_scr[0:pad] = xin_scr[tt:tt + pad]


def _mix_c_prompt(x, ng, wi, cw, cb, wra, bra, wrx, brx, lam, wo, fg, final_norm):
    nb, nt, _ = x.shape
    tt = min(MIX_ROWS, nt)
    row = pl.BlockSpec((1, tt, LRU_WIDTH), lambda b, i: (b, i, 0))
    cst = lambda a: pl.BlockSpec(a.shape, lambda b, i: (0,) * a.ndim, pipeline_mode=pl.Buffered(1))
    hist = CONV_WIDTH - 1
    return pl.pallas_call(
        functools.partial(_mix_c_prompt_body, final_norm=final_norm),
        grid=(nb, nt // tt),
        in_specs=[row, cst(ng), cst(wi), cst(cw), cst(cb), cst(wra), cst(bra), cst(wrx), cst(brx), cst(lam), cst(wo),
                  cst(fg)],
        out_specs=[row, pl.BlockSpec((1, 1, LRU_WIDTH), lambda b, i: (b, 0, 0)),
                   pl.BlockSpec((1, hist, LRU_WIDTH), lambda b, i: (b, 0, 0))],
        out_shape=[jax.ShapeDtypeStruct(x.shape, F32), jax.ShapeDtypeStruct((nb, 1, LRU_WIDTH), F32),
                   jax.ShapeDtypeStruct((nb, hist, LRU_WIDTH), F32)],
        scratch_shapes=[pltpu.VMEM((tt + 8, LRU_WIDTH), F32)] + [pltpu.VMEM((tt, LRU_WIDTH), F32)] * 3
                       + [pltpu.VMEM((8, LRU_WIDTH), F32)],
        compiler_params=_params(("arbitrary", "arbitrary")),
        name="mix_c_prompt",
    )(x, ng, wi, cw, cb, wra, bra, wrx, brx, lam, wo, fg)


def _mix_c_sample_body(xb_ref, z_ref, x_ref, h0_ref, c0_ref, cw_ref, cb_ref, wra_ref, bra_ref, wrx_ref, brx_ref,
                       lam_ref, wo_ref, fg_ref, y_ref, hl_ref, tail_ref, *, final_norm):
    nt = xb_ref.shape[0]
    hist = CONV_WIDTH - 1
    xin = [c0_ref[k] for k in range(hist)] + [xb_ref[t] for t in range(nt)]
    h = h0_ref[...]
    for t in range(nt):
        xc = cb_ref[...] + xin[t] * cw_ref[0:1]
        for k in range(1, CONV_WIDTH):
            xc = xc + xin[t + k] * cw_ref[k:k + 1]
        a, b = _lru_coeffs(xc, wra_ref, bra_ref, wrx_ref, brx_ref, lam_ref)
        h = a * h + b
        out = x_ref[t] + _mm((h * _silu(z_ref[t])).astype(BF16), wo_ref[...])
        y_ref[t] = _rms(out, fg_ref[...]) if final_norm else out
    hl_ref[...] = h
    for k in range(hist):
        tail_ref[k] = xin[nt + k]


def _mix_c_sample(xb, z, x, h0, c0, cw, cb, wra, bra, wrx, brx, lam, wo, fg, final_norm):
    args = (xb, z, x, h0, c0, cw, cb, wra, bra, wrx, brx, lam, wo, fg)
    return pl.pallas_call(
        functools.partial(_mix_c_sample_body, final_norm=final_norm),
        grid=(1,),
        in_specs=[_const_spec(a.shape) for a in args],
        out_specs=[_const_spec(x.shape), _const_spec(h0.shape), _const_spec(c0.shape)],
        out_shape=[jax.ShapeDtypeStruct(x.shape, F32), jax.ShapeDtypeStruct(h0.shape, F32),
                   jax.ShapeDtypeStruct(c0.shape, F32)],
        compiler_params=_params(("arbitrary",)),
        name="mix_c_sample",
    )(*args)


def _pair_perm():
    cols = []
    for c in range(HPG):
        cols += list(range(c * HEAD_DIM, (c + 1) * HEAD_DIM))
        cols += list(range((HPG + c) * HEAD_DIM, (HPG + c + 1) * HEAD_DIM))
    return np.asarray(cols)


def _layout_w_in_a(w):
    o_kv = NSA_WIDTH
    o_g = o_kv + 6 * KV_WIDTH
    o_z = o_g + 3 * N_HEADS
    o_rest = o_z + NSA_WIDTH
    wq = w[:, :NSA_WIDTH].reshape(D_MODEL, N_HEADS, HEAD_DIM)
    zeros = jnp.zeros_like(wq)
    first = (np.arange(N_HEADS) < HPG)[None, :, None]
    wq = jnp.concatenate([jnp.where(first, wq, zeros), jnp.where(first, zeros, wq)], axis=-1)
    wg = jnp.pad(w[:, o_g:o_z], ((0, 0), (0, LANE - 3 * N_HEADS)))
    wz = w[:, o_z:o_rest][:, _pair_perm()]
    return jnp.concatenate([wq.reshape(D_MODEL, QPAD), w[:, o_kv:o_g], wz, w[:, o_rest:], wg], axis=1).astype(BF16)


def _block_diag(w, n):
    return jnp.kron(jnp.eye(n, dtype=w.dtype), w)


def _even_layer(a, yp, ys, prev_nsa, cache_t, win_t, pt_flat, n_pages, tmpl, p):
    nb, nt, _ = yp.shape
    nseq, dec_t, _ = ys.shape
    past_len = n_pages * PAGE
    w_in = _layout_w_in_a(p['w_in_a'][a])
    norm_g = p['norm_a'][a][None]
    ln_g, ln_b = p['gmlp_ln_g'][a][None], p['gmlp_ln_b'][a][None]
    bg = jnp.pad(p['b_gate_a'][a], (0, LANE - 3 * N_HEADS))[None]
    w_out = jnp.concatenate([p['w_out_a'][a][:NSA_WIDTH][_pair_perm()], p['w_out_a'][a][NSA_WIDTH:]], axis=0).astype(BF16)
    pe, wc = p['pe_cmp'][a], p['w_cmp'][a]

    q, nsa_t, wrows_t, kvtb, gl, z, u, vn, zg = _proj_a(yp, norm_g, w_in, ln_g, ln_b, tm=min(PROJ_ROWS, nt),
                                                        feature_major=True, prev_nsa=prev_nsa)
    pe_rows = jnp.concatenate([pe[0], pe[0], pe[1], pe[1]], axis=1)
    w_bd = jnp.zeros((2 * KV_WIDTH, 2 * KV_WIDTH), F32)
    for k in range(4):
        w_bd = w_bd.at[k * HEAD_DIM:(k + 1) * HEAD_DIM, k * HEAD_DIM:(k + 1) * HEAD_DIM].set(wc[k // 2])
    cmp = _cmp_prompt(nsa_t, pe_rows, w_bd.astype(BF16))
    kaug, vaug, kwin, vwaug = _prompt_kv_operands(kvtb, tmpl['e_prompt'])
    yp_new = _attn_prompt(q, tmpl['far_lanes'], cmp, kaug, vaug, kwin, vwaug, gl, bg, tmpl['cmp'], tmpl['near'],
                          tmpl['win'], z, u, vn, zg, yp, p['w_spatial'][a], p['b_spatial'][a].T, w_out)
    wlen = min(WINDOW, nt)
    win_p = jnp.transpose(wrows_t[:, :, nt - wlen:].reshape(nb, 2, N_KV, HEAD_DIM, wlen), (0, 4, 1, 2, 3))

    xs = ys.reshape(1, nseq * dec_t, D_MODEL)
    q, kv, gl, z, u, vn, zg = _proj_a(xs, norm_g, w_in, ln_g, ln_b, tm=nseq * dec_t, feature_major=False)
    pad_t = lambda x: jnp.pad(x.reshape(nseq, dec_t, -1), ((0, 0), (0, SAMPLE_ROWS - dec_t), (0, 0)))
    q8, kv8, gl8 = pad_t(q), pad_t(kv), pad_t(gl)
    pe_t = jnp.concatenate([jnp.swapaxes(pe, 1, 2)] * N_KV, axis=1)
    wk_t = _block_diag(wc[0].T, N_KV).astype(BF16)
    wv_t = _block_diag(wc[1].T, N_KV).astype(BF16)
    oc, ps = _cmp_sample(a, pt_flat, n_pages, cache_t, q8, kv8, pe_t, wk_t, wv_t, tmpl['s_cmp'])
    unsel_t = _select_sample(ps.reshape(nseq * N_KV * SAMPLE_ROWS, LANE).T, past_len)
    unsel = unsel_t.T.reshape(nseq, N_KV * SAMPLE_ROWS, LANE)
    o8 = _attn_sample(a, pt_flat, n_pages, cache_t, win_t, q8, kv8, unsel, oc, gl8, bg, tmpl['e_sample'],
                      tmpl['s_sel'], tmpl['s_new'], tmpl['s_win'])
    o = o8[:, :dec_t].reshape(1, nseq * dec_t, NSA_WIDTH)
    w_small = p['w_spatial'][a][:, :dec_t, :dec_t]
    above = lambda k: jnp.pad(jnp.diagonal(w_small, offset=-k, axis1=1, axis2=2), ((0, 0), (k, 0))).T
    w_sp = jnp.stack([jnp.tile(above(k), (nseq, 1)) for k in range(dec_t)])
    b_sp = jnp.tile(p['b_spatial'][a][:, :dec_t].T, (nseq, 1))
    ys_new = _post_a(o, z, u, vn, zg, xs, w_sp, b_sp, w_out, short_chunks=True).reshape(nseq, dec_t, D_MODEL)
    rows_s = kv[0, :, :4 * KV_WIDTH].reshape(nseq, dec_t, 4, N_KV, HEAD_DIM)
    win_s = kv[0, :, 4 * KV_WIDTH:].reshape(nseq, dec_t, 2, N_KV, HEAD_DIM)
    return yp_new, ys_new, nsa_t, rows_s, win_p, win_s, vn.reshape(nseq, dec_t, GMLP_WIDTH)


def _odd_layer(c, yp, ys, h0, conv0, p, final):
    nb, nt, _ = yp.shape
    nseq, dec_t, _ = ys.shape
    norm_g = p['norm_c'][c][None]
    w_in = p['w_in_c'][c].astype(BF16)
    consts = (p['conv_w'][c], p['conv_b'][c][None], p['w_rg_a'][c].astype(BF16), p['b_rg_a'][c][None],
              p['w_rg_x'][c].astype(BF16), p['b_rg_x'][c][None], p['lru_lambda'][c][None],
              p['w_out_c'][c].astype(BF16), p['final_norm'][None])
    yp_new, h_p, tail_p = _mix_c_prompt(yp, norm_g, w_in, *consts, final_norm=final)
    xs = jnp.swapaxes(ys, 0, 1)
    xb, z = _proj_c(xs.reshape(dec_t * nseq, D_MODEL), norm_g, w_in, tm=dec_t * nseq)
    tm3 = lambda x: x.reshape(dec_t, nseq, -1)
    ys_new, h_s, tail_s = _mix_c_sample(tm3(xb), tm3(z), xs, h0, jnp.swapaxes(conv0, 0, 1), *consts,
                                        final_norm=final)
    return yp_new, jnp.swapaxes(ys_new, 0, 1), h_p[:, 0], h_s, tail_p, jnp.swapaxes(tail_s, 0, 1)


def kernel(x_prompt, x_sample, cache_nsa, cache_win, state_lru_h, state_lru_conv, page_table, norm_a, w_in_a, b_gate_a, pe_cmp, w_cmp, gmlp_ln_g, gmlp_ln_b, w_spatial, b_spatial, w_out_a, norm_c, w_in_c, conv_w, conv_b, w_rg_a, b_rg_a, w_rg_x, b_rg_x, lru_lambda, w_out_c, rel_bias, final_norm):
    p = dict(norm_a=norm_a, w_in_a=w_in_a, b_gate_a=b_gate_a, pe_cmp=pe_cmp, w_cmp=w_cmp, gmlp_ln_g=gmlp_ln_g,
             gmlp_ln_b=gmlp_ln_b, w_spatial=w_spatial, b_spatial=b_spatial, w_out_a=w_out_a, norm_c=norm_c,
             w_in_c=w_in_c, conv_w=conv_w, conv_b=conv_b, w_rg_a=w_rg_a, b_rg_a=b_rg_a, w_rg_x=w_rg_x,
             b_rg_x=b_rg_x, lru_lambda=lru_lambda, w_out_c=w_out_c, rel_bias=rel_bias, final_norm=final_norm)
    nb, nt, _ = x_prompt.shape
    nseq, n_pages = page_table.shape
    past_len = n_pages * PAGE
    wlen = cache_win.shape[2]
    depth = norm_a.shape[0] + norm_c.shape[0]
    assert nt % KV_TILE == 0 and nt >= WIN_KEYS and nt // CMP_BLOCK <= LANE
    assert past_len // CMP_BLOCK < LANE and x_sample.shape[1] <= SAMPLE_ROWS and n_pages <= N_PAGES_MAX

    cache_t = jnp.transpose(cache_nsa, (0, 1, 3, 4, 5, 2))
    win_t = jnp.transpose(cache_win, (0, 1, 3, 4, 5, 2))
    pt_flat = page_table.reshape(-1).astype(jnp.int32)

    blk_of_key = lambda n: (np.arange(n) // CMP_BLOCK)[None, :] == np.arange(LANE)[:, None]
    rows_s = SAMPLE_ROWS
    stack = lambda t: t.reshape(N_HEADS * rows_s, t.shape[-1])
    by_group = lambda t: t.reshape(N_KV, GROUP_ROWS, t.shape[-1])
    assert NEAR_SPAN + LANE - (KV_TILE - 1) >= _UPPER[-1] and nt % KV_TILE == 0
    tmpl = dict(
        far_lanes=_far_bias_lanes(rel_bias),
        e_prompt=jnp.asarray(np.where(blk_of_key(nt), NEG, 0.0), BF16),
        e_sample=jnp.asarray(np.where(blk_of_key(past_len), NEG, 0.0), BF16),
        cmp=_bias_template(rel_bias, Q_BLOCK, LANE, -CMP_BLOCK, CMP_BLOCK * (LANE - 2) - (CMP_BLOCK - 1)),
        near=by_group(_bias_template(rel_bias, Q_BLOCK, NEAR_SPAN + KV_TILE, -1, NEAR_SPAN)),
        win=by_group(_bias_template(rel_bias, Q_BLOCK, WINDOW + WIN_KEYS, -1, WINDOW, hi=WINDOW)),
        s_cmp=stack(_bias_template(rel_bias, rows_s, LANE, -CMP_BLOCK, past_len - (CMP_BLOCK - 1))),
        s_sel=stack(_bias_template(rel_bias, rows_s, past_len, -1, past_len)),
        s_new=stack(_bias_template(rel_bias, rows_s, LANE, -1, 0, cmax=x_sample.shape[1])),
        s_win=stack(_bias_template(rel_bias, rows_s, wlen, -1, wlen, hi=WINDOW)),
    )

    yp, ys = x_prompt, x_sample
    outs = [[] for _ in range(8)]
    nsa_all = None
    for layer in range(depth):
        if layer % 2 == 0:
            yp, ys, nsa_all, *leaves = _even_layer(layer // 2, yp, ys, nsa_all, cache_t, win_t, pt_flat, n_pages,
                                                   tmpl, p)
            for dst, leaf in zip(outs[:4], leaves):
                dst.append(leaf)
        else:
            c = layer // 2
            yp, ys, *leaves = _odd_layer(c, yp, ys, state_lru_h[c], state_lru_conv[c], p, final=layer == depth - 1)
            for dst, leaf in zip(outs[4:], leaves):
                dst.append(leaf)
    nsa_p = jnp.transpose(nsa_all.reshape(nsa_all.shape[0], nb, 4, N_KV, HEAD_DIM, nt), (0, 1, 5, 2, 3, 4))
    return (yp, ys, nsa_p) + tuple(jnp.stack(o) for o in outs)
```
